```python
import math
import jax, jax.numpy as jnp
from jax import lax
import numpy as np

D_MODEL = 1024
BATCH = 2
SEQ = 16384
DEPTH = 4

D_MIX = D_MODEL
HEAD_DIM = 64
DIFF_HEADS = D_MIX // 256
DIFF_QK_DIM = HEAD_DIM
DIFF_V_DIM = 2 * HEAD_DIM
DSA_HEADS = D_MIX // 256
DSA_DIM = HEAD_DIM
IDX_HEADS = D_MIX // 128
IDX_DIM = HEAD_DIM
FOX_HEADS = D_MIX // 256
FOX_DIM = HEAD_DIM
DIFF_WIDTH = DIFF_HEADS * DIFF_V_DIM
DSA_WIDTH = DSA_HEADS * DSA_DIM
FOX_WIDTH = FOX_HEADS * FOX_DIM
DSA_TOPK_MAX = 256
Q_BLOCK = 128
ROPE_THETA = 10000.0
LN_EPS = 1e-5
SUBLN_EPS = 1e-5
NEG_INF = -1e30
DEEPNORM_ALPHA = (2 * DEPTH) ** 0.25
DEEPNORM_BETA = (8 * DEPTH) ** -0.25
IDX_WEIGHT_SCALE = (IDX_HEADS * IDX_DIM) ** -0.5

IN_SPLITS = (
    ('diff_q', DIFF_HEADS * 2 * DIFF_QK_DIM),
    ('diff_k', DIFF_HEADS * 2 * DIFF_QK_DIM),
    ('diff_v', DIFF_HEADS * DIFF_V_DIM),
    ('dsa_q', DSA_HEADS * DSA_DIM),
    ('dsa_k', DSA_DIM),
    ('dsa_v', DSA_DIM),
    ('idx_q', IDX_HEADS * IDX_DIM),
    ('idx_k', IDX_DIM),
    ('idx_w', IDX_HEADS),
    ('fox_q', FOX_HEADS * FOX_DIM),
    ('fox_k', FOX_HEADS * FOX_DIM),
    ('fox_v', FOX_HEADS * FOX_DIM),
    ('fox_f', FOX_HEADS),
    ('gate', D_MIX),
)
VALUE_COLS = ('diff_v', 'dsa_v', 'fox_v')
IN_WIDTH = sum(n for _, n in IN_SPLITS)

kernel_name = 'hymba_style_diff_dsa_fox_deepnorm'


def split_columns(h):
    out, off = [], 0
    for _, n in IN_SPLITS:
        out.append(h[..., off:off + n])
        off += n
    return out


def rope(x, pos):
    d = x.shape[-1]
    inv = ROPE_THETA ** (-jnp.arange(0, d, 2, dtype=jnp.float32) / d)
    ang = pos.astype(jnp.float32)[:, None] * inv[None, :]
    shp = (x.shape[1],) + (1,) * (x.ndim - 3) + (d // 2,)
    cos = jnp.cos(ang).reshape(shp).astype(x.dtype)
    sin = jnp.sin(ang).reshape(shp).astype(x.dtype)
    x1, x2 = x[..., : d // 2], x[..., d // 2:]
    return jnp.concatenate([x1 * cos - x2 * sin, x2 * cos + x1 * sin], axis=-1)


def to_blocks(a):
    B, S = a.shape[:2]
    return jnp.moveaxis(a.reshape((B, S // Q_BLOCK, Q_BLOCK) + a.shape[2:]), 1, 0)


def from_blocks(a):
    nb, B, qb = a.shape[:3]
    return jnp.moveaxis(a, 0, 1).reshape((B, nb * qb) + a.shape[3:])


def layer_norm(x, g, b):
    xf = x.astype(jnp.float32)
    mu = jnp.mean(xf, axis=-1, keepdims=True)
    var = jnp.mean(jnp.square(xf - mu), axis=-1, keepdims=True)
    y = (xf - mu) * lax.rsqrt(var + LN_EPS) * g.astype(jnp.float32) + b.astype(jnp.float32)
    return y.astype(x.dtype)


def diff_attention(q, k, v, lam, pos):
    scale = DIFF_QK_DIM ** -0.5
    S = q.shape[1]

    def block(args):
        qb, t = args
        s = jnp.einsum('bqhmd,bkhmd->bhmqk', qb, k).astype(jnp.float32) * scale
        causal = t[:, None] >= pos[None, :]
        p = jax.nn.softmax(jnp.where(causal, s, NEG_INF), axis=-1)
        p = p[:, :, 0] - lam * p[:, :, 1]
        return jnp.einsum('bhqk,bkhe->bqhe', p, v)

    out = lax.map(block, (to_blocks(q), pos.reshape(S // Q_BLOCK, Q_BLOCK)))
    return from_blocks(out)


def dsa_attention(q, k, v, qi, ki, wi, pos, topk):
    scale = DSA_DIM ** -0.5
    S = q.shape[1]

    def block(args):
        qb, qib, wib, t = args
        sc = jnp.einsum('bqhd,bsd->bqhs', qib, ki).astype(jnp.float32)
        idx_score = jnp.einsum('bqhs,bqh->bqs', jax.nn.relu(sc), wib.astype(jnp.float32))
        causal = t[:, None] >= pos[None, :]
        idx_score = jnp.where(causal[None], idx_score, -jnp.inf)
        _, idx = lax.top_k(idx_score, topk)
        kg = jax.vmap(lambda kk, ii: kk[ii])(k, idx)
        vg = jax.vmap(lambda vv, ii: vv[ii])(v, idx)
        s = jnp.einsum('bqhd,bqkd->bhqk', qb, kg).astype(jnp.float32) * scale
        valid = idx <= t[None, :, None]
        p = jax.nn.softmax(jnp.where(valid[:, None], s, NEG_INF), axis=-1)
        return jnp.einsum('bhqk,bqkd->bqhd', p, vg)

    out = lax.map(block, (to_blocks(q), to_blocks(qi), to_blocks(wi),
                          pos.reshape(S // Q_BLOCK, Q_BLOCK)))
    return from_blocks(out)


def forgetting_attention(q, k, v, logf, pos):
    scale = FOX_DIM ** -0.5
    S = q.shape[1]
    c = jnp.cumsum(logf, axis=1)
    c_keys = jnp.transpose(c, (0, 2, 1))

    def block(args):
        qb, cb, t = args
        s = jnp.einsum('bqhd,bkhd->bhqk', qb, k).astype(jnp.float32) * scale
        s = s + jnp.transpose(cb, (0, 2, 1))[..., None] - c_keys[:, :, None, :]
        causal = t[:, None] >= pos[None, :]
        p = jax.nn.softmax(jnp.where(causal, s, NEG_INF), axis=-1)
        return jnp.einsum('bhqk,bkhd->bqhd', p, v)

    out = lax.map(block, (to_blocks(q), to_blocks(c), pos.reshape(S // Q_BLOCK, Q_BLOCK)))
    return from_blocks(out)


def hybrid_layer(x, w_in, b_f, lam_q1, lam_k1, lam_q2, lam_k2, g_subln, w_out, ln_g, ln_b, layer_idx):
    B, S, _ = x.shape
    pos = jnp.arange(S, dtype=jnp.int32)
    h = jnp.einsum('bsd,de->bse', x, w_in)
    (dq, dk, dv, sq, sk, sv, iq, ik, iw, fq, fk, fv, ff, gate) = split_columns(h)

    dq = rope(dq.reshape(B, S, DIFF_HEADS * 2, DIFF_QK_DIM), pos).reshape(B, S, DIFF_HEADS, 2, DIFF_QK_DIM)
    dk = rope(dk.reshape(B, S, DIFF_HEADS * 2, DIFF_QK_DIM), pos).reshape(B, S, DIFF_HEADS, 2, DIFF_QK_DIM)
    dv = dv.reshape(B, S, DIFF_HEADS, DIFF_V_DIM)
    lambda_init = 0.8 - 0.6 * math.exp(-0.3 * layer_idx)
    lam = (jnp.exp(jnp.sum(lam_q1.astype(jnp.float32) * lam_k1.astype(jnp.float32)))
           - jnp.exp(jnp.sum(lam_q2.astype(jnp.float32) * lam_k2.astype(jnp.float32))) + lambda_init)
    a = diff_attention(dq, dk, dv, lam, pos)
    a = a * lax.rsqrt(jnp.mean(a * a, axis=-1, keepdims=True) + SUBLN_EPS) * g_subln.astype(jnp.float32) * (1.0 - lambda_init)
    a = a.reshape(B, S, DIFF_WIDTH)

    sq = rope(sq.reshape(B, S, DSA_HEADS, DSA_DIM), pos)
    sk = rope(sk, pos)
    iq = rope(iq.reshape(B, S, IDX_HEADS, IDX_DIM), pos)
    ik = rope(ik, pos)
    topk = min(DSA_TOPK_MAX, S // 4)
    bo = dsa_attention(sq, sk, sv, iq, ik, iw * IDX_WEIGHT_SCALE, pos, topk).reshape(B, S, DSA_WIDTH)

    logf = jax.nn.log_sigmoid(ff.astype(jnp.float32) + b_f.astype(jnp.float32))
    co = forgetting_attention(fq.reshape(B, S, FOX_HEADS, FOX_DIM), fk.reshape(B, S, FOX_HEADS, FOX_DIM),
                              fv.reshape(B, S, FOX_HEADS, FOX_DIM), logf, pos).reshape(B, S, FOX_WIDTH)

    mixed = jnp.concatenate([a, bo, co], axis=-1) * jax.nn.silu(gate.astype(jnp.float32))
    out = jnp.einsum('bse,ed->bsd', mixed.astype(x.dtype), w_out)
    return layer_norm(DEEPNORM_ALPHA * x + out, ln_g, ln_b)


def setup_inputs(seed: int = 0) -> dict:
    key = jax.random.key(seed)
    ks = jax.random.split(key, 12)
    x = jax.random.normal(ks[0], (BATCH, SEQ, D_MODEL), jnp.float32)
    col_scale = np.concatenate([
        np.full((n,), DEEPNORM_BETA if name in VALUE_COLS else 1.0, np.float32) for name, n in IN_SPLITS
    ]) * np.float32(D_MODEL ** -0.5)
    w_in = jax.random.normal(ks[1], (DEPTH, D_MODEL, IN_WIDTH), jnp.float32) * jnp.asarray(col_scale)
    b_f = 1.0 + 0.1 * jax.random.normal(ks[2], (DEPTH, FOX_HEADS), jnp.float32)
    lam_q1 = 0.1 * jax.random.normal(ks[3], (DEPTH, DIFF_QK_DIM), jnp.float32)
    lam_k1 = 0.1 * jax.random.normal(ks[4], (DEPTH, DIFF_QK_DIM), jnp.float32)
    lam_q2 = 0.1 * jax.random.normal(ks[5], (DEPTH, DIFF_QK_DIM), jnp.float32)
    lam_k2 = 0.1 * jax.random.normal(ks[6], (DEPTH, DIFF_QK_DIM), jnp.float32)
    g_subln = 1.0 + 0.02 * jax.random.normal(ks[7], (DEPTH, DIFF_V_DIM), jnp.float32)
    w_out = jax.random.normal(ks[8], (DEPTH, D_MIX, D_MODEL), jnp.float32) * (D_MIX ** -0.5) * DEEPNORM_BETA
    ln_g = 1.0 + 0.02 * jax.random.normal(ks[9], (DEPTH, D_MODEL), jnp.float32)
    ln_b = 0.02 * jax.random.normal(ks[10], (DEPTH, D_MODEL), jnp.float32)
    return {'x': x, 'w_in': w_in, 'b_f': b_f, 'lam_q1': lam_q1, 'lam_k1': lam_k1,
            'lam_q2': lam_q2, 'lam_k2': lam_k2, 'g_subln': g_subln, 'w_out': w_out,
            'ln_g': ln_g, 'ln_b': ln_b}


def reference(x, w_in, b_f, lam_q1, lam_k1, lam_q2, lam_k2, g_subln, w_out, ln_g, ln_b):
    for l in range(DEPTH):
        x = hybrid_layer(x, w_in[l], b_f[l], lam_q1[l], lam_k1[l], lam_q2[l], lam_k2[l],
                         g_subln[l], w_out[l], ln_g[l], ln_b[l], l)
    return x
```

```python
import functools
import math

import numpy as np
import jax
import jax.numpy as jnp
from jax import lax
from jax.experimental import pallas as pl
from jax.experimental.pallas import tpu as pltpu

D_MODEL = 1024
DEPTH = 4
HEAD_DIM = 64
DIFF_HEADS = 4
DSA_HEADS = 4
IDX_HEADS = 8
FOX_HEADS = 4
DSA_TOPK_MAX = 256
ROPE_THETA = 10000.0
LN_EPS = 1e-5
SUBLN_EPS = 1e-5
DEEPNORM_ALPHA = (2 * DEPTH) ** 0.25
IDX_WEIGHT_SCALE = (IDX_HEADS * HEAD_DIM) ** -0.5
QK_SCALE = HEAD_DIM ** -0.5

LANES = 128
VMEM_LIMIT_BYTES = 56 * 1024 * 1024

_OFF = {}
_o = 0
for _name, _n in (('diff_q', 512), ('diff_k', 512), ('diff_v', 512), ('dsa_q', 256), ('dsa_k', 64),
                  ('dsa_v', 64), ('idx_q', 512), ('idx_k', 64), ('idx_w', 8), ('fox_q', 256),
                  ('fox_k', 256), ('fox_v', 256), ('fox_f', 4), ('gate', 1024)):
    _OFF[_name] = (_o, _n)
    _o += _n
IN_WIDTH = _o


def _cols(name, scale=1.0, repeat=1):
    o, n = _OFF[name]
    idx = np.tile(np.arange(o, o + n), repeat)
    return idx, np.full(idx.shape, scale, np.float32)


def _pad(n):
    return np.zeros((n,), np.int64), np.zeros((n,), np.float32)


def _layout(parts):
    idx = np.concatenate([p[0] for p in parts])
    scale = np.concatenate([p[1] for p in parts])
    return idx, scale


A_IDX, A_SCALE = _layout([_cols('diff_q', QK_SCALE), _cols('diff_k'), _cols('idx_q'),
                          _cols('dsa_q', QK_SCALE), _cols('idx_k', repeat=2), _cols('dsa_k', repeat=2)])
A_WIDTH = A_IDX.shape[0]
B_IDX, B_SCALE = _layout([_cols('gate'), _cols('diff_v'), _cols('fox_q', QK_SCALE), _cols('fox_k'),
                          _cols('fox_v'), _cols('dsa_v', repeat=2), _pad(128)])
B_WIDTH = B_IDX.shape[0]
C_IDX, C_SCALE = _layout([_cols('idx_w'), _cols('fox_f'), _pad(LANES - 12)])
W_LANE0 = 0
F_LANE0 = 8

_c = np.arange(A_WIDTH)
_first_half = (_c % HEAD_DIM) < HEAD_DIM // 2
A_ROT_PARTNER = np.where(_first_half, _c + HEAD_DIM // 2, _c - HEAD_DIM // 2)
A_ROT_SIGN = np.where(_first_half, -1.0, 1.0).astype(np.float32)

_NT = (((1,), (1,)), ((), ()))
INT_MIN = -2 ** 31


def _cparams(sem):
    return pltpu.CompilerParams(dimension_semantics=sem, vmem_limit_bytes=VMEM_LIMIT_BYTES)


def _proj_rope_kernel(x_ref, w_ref, wr_ref, cos_ref, sin_ref, o_ref):
    xb = x_ref[...].astype(jnp.bfloat16)
    h = jnp.dot(xb, w_ref[...], preferred_element_type=jnp.float32)
    hr = jnp.dot(xb, wr_ref[...], preferred_element_type=jnp.float32)
    cos = cos_ref[...]
    sin = sin_ref[...]
    for c in range(o_ref.shape[1] // LANES):
        sl = slice(c * LANES, (c + 1) * LANES)
        o_ref[:, sl] = (h[:, sl] * cos + hr[:, sl] * sin).astype(o_ref.dtype)


def _proj_rope(x2d, w, wr, cos, sin, seq, tm=512, tn=512):
    m = x2d.shape[0]
    nseq = seq // tm
    return pl.pallas_call(
        _proj_rope_kernel,
        grid=(m // tm, A_WIDTH // tn),
        in_specs=[
            pl.BlockSpec((tm, D_MODEL), lambda i, j: (i, 0)),
            pl.BlockSpec((D_MODEL, tn), lambda i, j: (0, j)),
            pl.BlockSpec((D_MODEL, tn), lambda i, j: (0, j)),
            pl.BlockSpec((tm, LANES), lambda i, j: (i % nseq, 0)),
            pl.BlockSpec((tm, LANES), lambda i, j: (i % nseq, 0)),
        ],
        out_specs=pl.BlockSpec((tm, tn), lambda i, j: (i, j)),
        out_shape=jax.ShapeDtypeStruct((m, A_WIDTH), jnp.bfloat16),
        compiler_params=_cparams(("parallel", "arbitrary")),
        name="proj_rope",
    )(x2d, w, wr, cos, sin)


def _proj_plain_kernel(x_ref, w_ref, o_ref):
    xb = x_ref[...].astype(jnp.bfloat16)
    o_ref[...] = jnp.dot(xb, w_ref[...], preferred_element_type=jnp.float32).astype(o_ref.dtype)


def _proj_plain(x2d, w, tm=512, tn=512):
    m = x2d.shape[0]
    return pl.pallas_call(
        _proj_plain_kernel,
        grid=(m // tm, B_WIDTH // tn),
        in_specs=[
            pl.BlockSpec((tm, D_MODEL), lambda i, j: (i, 0)),
            pl.BlockSpec((D_MODEL, tn), lambda i, j: (0, j)),
        ],
        out_specs=pl.BlockSpec((tm, tn), lambda i, j: (i, j)),
        out_shape=jax.ShapeDtypeStruct((m, B_WIDTH), jnp.bfloat16),
        compiler_params=_cparams(("parallel", "arbitrary")),
        name="proj_plain",
    )(x2d, w)


def _split3(v):
    hi = v.astype(jnp.bfloat16)
    r = v - hi.astype(jnp.float32)
    mid = r.astype(jnp.bfloat16)
    lo = (r - mid.astype(jnp.float32)).astype(jnp.bfloat16)
    return hi, mid, lo


def _proj_small_kernel(x_ref, w_ref, bf_ref, tri_ref, o_ref, c_ref, carry_ref, *, blocks_per_seq):
    i = pl.program_id(0)
    xb = x_ref[...].astype(jnp.bfloat16)
    h = jnp.dot(xb, w_ref[...], preferred_element_type=jnp.float32)
    lane = lax.broadcasted_iota(jnp.int32, h.shape, 1)
    ff = h + bf_ref[...]
    logf = jnp.minimum(ff, 0.0) - jnp.log(1.0 + jnp.exp(-jnp.abs(ff)))
    o = jnp.where(lane < F_LANE0, h * IDX_WEIGHT_SCALE, logf)
    o_ref[...] = o

    @pl.when(i % blocks_per_seq == 0)
    def _():
        carry_ref[...] = jnp.zeros_like(carry_ref)

    tri = tri_ref[...]
    hi, mid, lo = _split3(o)
    cs = (jnp.dot(tri, hi, preferred_element_type=jnp.float32)
          + jnp.dot(tri, mid, preferred_element_type=jnp.float32)
          + jnp.dot(tri, lo, preferred_element_type=jnp.float32))
    cs = cs + carry_ref[0:1, :]
    c_ref[...] = cs
    tm = cs.shape[0]
    carry_ref[...] = jnp.broadcast_to(cs[tm - 1:tm, :], carry_ref.shape)


def _proj_small(x2d, w, bf_row, seq, tm=512):
    m = x2d.shape[0]
    tri = jnp.asarray(np.tril(np.ones((tm, tm), np.float32)), jnp.bfloat16)
    kern = functools.partial(_proj_small_kernel, blocks_per_seq=seq // tm)
    return pl.pallas_call(
        kern,
        grid=(m // tm,),
        in_specs=[
            pl.BlockSpec((tm, D_MODEL), lambda i: (i, 0)),
            pl.BlockSpec((D_MODEL, LANES), lambda i: (0, 0)),
            pl.BlockSpec((1, LANES), lambda i: (0, 0)),
            pl.BlockSpec((tm, tm), lambda i: (0, 0)),
        ],
        out_specs=[pl.BlockSpec((tm, LANES), lambda i: (i, 0)),
                   pl.BlockSpec((tm, LANES), lambda i: (i, 0))],
        out_shape=[jax.ShapeDtypeStruct((m, LANES), jnp.float32),
                   jax.ShapeDtypeStruct((m, LANES), jnp.float32)],
        scratch_shapes=[pltpu.VMEM((8, LANES), jnp.float32)],
        compiler_params=_cparams(("arbitrary",)),
        name="proj_small",
    )(x2d, w, bf_row, tri)


def _flash_init(m_ref, l_ref, acc_ref):
    m_ref[...] = jnp.full(m_ref.shape, -1e30, jnp.float32)
    l_ref[...] = jnp.zeros(l_ref.shape, jnp.float32)
    acc_ref[...] = jnp.zeros(acc_ref.shape, jnp.float32)


def _flash_update(s, v, m_ref, l_ref, acc_ref):
    tk = s.shape[1]
    m_prev = m_ref[...]
    l_prev = l_ref[...]
    m_cur = jnp.max(s, axis=1, keepdims=True)
    m_next = jnp.maximum(m_prev, m_cur)
    p = jnp.exp(s - pltpu.repeat(m_next, tk // LANES, axis=1))
    alpha = jnp.exp(m_prev - m_next)
    l_ref[...] = alpha * l_prev + jnp.sum(p, axis=1, keepdims=True)
    m_ref[...] = m_next
    pv = jnp.dot(p.astype(jnp.bfloat16), v, preferred_element_type=jnp.float32)
    acc_ref[...] = acc_ref[...] * alpha + pv


def _causal_mask(rows, tq, tk):
    r = lax.broadcasted_iota(jnp.int32, (rows, tk), 0) & (tq - 1)
    c = lax.broadcasted_iota(jnp.int32, (rows, tk), 1)
    return c <= r


def _diff_kernel(lam_ref, g_ref, q_ref, k_ref, v_ref, o_ref, qs_ref, m_ref, l_ref, acc_ref, *, t):
    i = pl.program_id(2)
    q = q_ref[0]
    lane = lax.broadcasted_iota(jnp.int32, q.shape, 1)
    zero = jnp.zeros_like(q)
    qs_ref[0:t, :] = jnp.where(lane < HEAD_DIM, q, zero)
    qs_ref[t:2 * t, :] = jnp.where(lane >= HEAD_DIM, q, zero)
    _flash_init(m_ref, l_ref, acc_ref)

    def block(j, masked):
        start = pl.multiple_of(j * t, t)
        k = k_ref[0, pl.ds(start, t), :]
        v = v_ref[0, pl.ds(start, t), :]
        s = lax.dot_general(qs_ref[...], k, _NT, preferred_element_type=jnp.float32)
        if masked:
            s = jnp.where(_causal_mask(2 * t, t, t), s, -jnp.inf)
        _flash_update(s, v, m_ref, l_ref, acc_ref)

    def body(j, carry):
        block(j, False)
        return carry

    lax.fori_loop(0, i, body, 0)
    block(i, True)

    lam_rows = lam_ref[...]
    s1 = jnp.sum(lam_rows[0:1, :] * lam_rows[1:2, :], axis=1, keepdims=True)
    s2 = jnp.sum(lam_rows[2:3, :] * lam_rows[3:4, :], axis=1, keepdims=True)
    lam_init = lam_rows[4:5, 0:1]
    lam = jnp.exp(s1) - jnp.exp(s2) + lam_init
    o0 = acc_ref[0:t, :] / l_ref[0:t, :]
    o1 = acc_ref[t:2 * t, :] / l_ref[t:2 * t, :]
    a = o0 - lam * o1
    ms = jnp.mean(a * a, axis=1, keepdims=True)
    a = a * lax.rsqrt(ms + SUBLN_EPS) * g_ref[...] * (1.0 - lam_init)
    o_ref[0] = a.astype(o_ref.dtype)


def _diff_attention(a3, b3, lam_rows, g_row, t=256):
    bsz, seq, _ = a3.shape
    kern = functools.partial(_diff_kernel, t=t)
    return pl.pallas_call(
        kern,
        grid=(bsz, DIFF_HEADS, seq // t),
        in_specs=[
            pl.BlockSpec((8, LANES), lambda b, h, i: (0, 0)),
            pl.BlockSpec((1, LANES), lambda b, h, i: (0, 0)),
            pl.BlockSpec((1, t, LANES), lambda b, h, i: (b, i, h)),
            pl.BlockSpec((1, seq, LANES), lambda b, h, i: (b, 0, 4 + h)),
            pl.BlockSpec((1, seq, LANES), lambda b, h, i: (b, 0, 8 + h)),
        ],
        out_specs=pl.BlockSpec((1, t, LANES), lambda b, h, i: (b, i, h)),
        out_shape=jax.ShapeDtypeStruct((bsz, seq, DIFF_HEADS * LANES), jnp.bfloat16),
        scratch_shapes=[pltpu.VMEM((2 * t, LANES), jnp.bfloat16),
                        pltpu.VMEM((2 * t, LANES), jnp.float32),
                        pltpu.VMEM((2 * t, LANES), jnp.float32),
                        pltpu.VMEM((2 * t, LANES), jnp.float32)],
        compiler_params=_cparams(("parallel", "parallel", "arbitrary")),
        name="diff_attention",
    )(lam_rows, g_row, a3, a3, b3)


def _fox_kernel(q_ref, k_ref, v_ref, cq_ref, ck_ref, o_ref, qs_ref, cqs_ref, m_ref, l_ref, acc_ref, *, t):
    g = pl.program_id(1)
    i = pl.program_id(2)
    q = q_ref[0]
    lane = lax.broadcasted_iota(jnp.int32, q.shape, 1)
    zero = jnp.zeros_like(q)
    qs_ref[0:t, :] = jnp.where(lane < HEAD_DIM, q, zero)
    qs_ref[t:2 * t, :] = jnp.where(lane >= HEAD_DIM, q, zero)
    ctile = cq_ref[0]
    for a in range(2):
        sel = lane == (F_LANE0 + 2 * g + a)
        col = jnp.sum(jnp.where(sel, ctile, 0.0), axis=1, keepdims=True)
        cqs_ref[a * t:(a + 1) * t, :] = jnp.broadcast_to(col, (t, LANES))
    _flash_init(m_ref, l_ref, acc_ref)

    def block(j, masked):
        start = pl.multiple_of(j * t, t)
        k = k_ref[0, pl.ds(start, t), :]
        v = v_ref[0, pl.ds(start, t), :]
        s = lax.dot_general(qs_ref[...], k, _NT, preferred_element_type=jnp.float32)
        ck = ck_ref[0, 0, :, pl.ds(start, t)]
        ckr = jnp.concatenate([jnp.broadcast_to(ck[0:1, :], (t, t)),
                               jnp.broadcast_to(ck[1:2, :], (t, t))], axis=0)
        s = s + (pltpu.repeat(cqs_ref[...], t // LANES, axis=1) - ckr)
        if masked:
            s = jnp.where(_causal_mask(2 * t, t, t), s, -jnp.inf)
        _flash_update(s, v, m_ref, l_ref, acc_ref)

    def body(j, carry):
        block(j, False)
        return carry

    lax.fori_loop(0, i, body, 0)
    block(i, True)

    o0 = acc_ref[0:t, :] / l_ref[0:t, :]
    o1 = acc_ref[t:2 * t, :] / l_ref[t:2 * t, :]
    o_ref[0] = jnp.where(lane < HEAD_DIM, o0, o1).astype(o_ref.dtype)


def _fox_attention(b3, csum3, ck4, t=256):
    bsz, seq, _ = b3.shape
    kern = functools.partial(_fox_kernel, t=t)
    return pl.pallas_call(
        kern,
        grid=(bsz, FOX_HEADS // 2, seq // t),
        in_specs=[
            pl.BlockSpec((1, t, LANES), lambda b, g, i: (b, i, 12 + g)),
            pl.BlockSpec((1, seq, LANES), lambda b, g, i: (b, 0, 14 + g)),
            pl.BlockSpec((1, seq, LANES), lambda b, g, i: (b, 0, 16 + g)),
            pl.BlockSpec((1, t, LANES), lambda b, g, i: (b, i, 0)),
            pl.BlockSpec((1, 1, 8, seq), lambda b, g, i: (b, g, 0, 0)),
        ],
        out_specs=pl.BlockSpec((1, t, LANES), lambda b, g, i: (b, i, g)),
        out_shape=jax.ShapeDtypeStruct((bsz, seq, FOX_HEADS * HEAD_DIM), jnp.bfloat16),
        scratch_shapes=[pltpu.VMEM((2 * t, LANES), jnp.bfloat16),
                        pltpu.VMEM((2 * t, LANES), jnp.float32),
                        pltpu.VMEM((2 * t, LANES), jnp.float32),
                        pltpu.VMEM((2 * t, LANES), jnp.float32),
                        pltpu.VMEM((2 * t, LANES), jnp.float32)],
        compiler_params=_cparams(("parallel", "parallel", "arbitrary")),
        name="fox_attention",
    )(b3, b3, b3, csum3, ck4)


def _score_key(score):
    bits = pltpu.bitcast(score, jnp.int32)
    return bits ^ ((bits >> 31) & 0x7FFFFFFF)


def _dsa_kernel(iq_ref, sq_ref, w_ref, ik_ref, sk_ref, v_ref, o_ref,
                keys_ref, qi_ref, qd_ref, wb_ref, m_ref, l_ref, acc_ref, *, tq, tc, topk, idx_bits):
    i = pl.program_id(1)
    lane = lax.broadcasted_iota(jnp.int32, (tq, LANES), 1)
    low = lane < HEAD_DIM

    for h in range(IDX_HEADS):
        blk = iq_ref[0, :, (h // 2) * LANES:(h // 2 + 1) * LANES]
        qi_ref[h * tq:(h + 1) * tq, :] = jnp.where(low if h % 2 == 0 else ~low, blk, jnp.zeros_like(blk))
    for h in range(DSA_HEADS):
        blk = sq_ref[0, :, (h // 2) * LANES:(h // 2 + 1) * LANES]
        qd_ref[h * tq:(h + 1) * tq, :] = jnp.where(low if h % 2 == 0 else ~low, blk, jnp.zeros_like(blk))
    wt = w_ref[0]
    for h in range(IDX_HEADS):
        col = jnp.sum(jnp.where(lane == W_LANE0 + h, wt, 0.0), axis=1, keepdims=True)
        wb_ref[h] = jnp.broadcast_to(col, (tq, LANES))

    ratio = tc // tq
    nch = i // ratio + 1
    row_pos = i * tq + lax.broadcasted_iota(jnp.int32, (tq, tc), 0)
    col_iota = lax.broadcasted_iota(jnp.int32, (tq, tc), 1)

    def score_chunk(j, carry):
        start = pl.multiple_of(j * tc, tc)
        ik = ik_ref[0, pl.ds(start, tc), :]
        s = lax.dot_general(qi_ref[...], ik, _NT, preferred_element_type=jnp.float32)
        score = jnp.zeros((tq, tc), jnp.float32)
        for h in range(IDX_HEADS):
            score = score + jnp.maximum(s[h * tq:(h + 1) * tq, :], 0.0) * pltpu.repeat(wb_ref[h], tc // LANES, axis=1)
        key = _score_key(score)
        key = jnp.where(start + col_iota <= row_pos, key, INT_MIN)
        keys_ref[:, pl.ds(start, tc)] = key
        return carry

    lax.fori_loop(0, nch, score_chunk, 0)

    def count(pred):
        def body(c, acc):
            start = pl.multiple_of(c * tc, tc)
            hit = pred(keys_ref[:, pl.ds(start, tc)], start).astype(jnp.int32)
            for u in range(tc // LANES):
                acc = acc + hit[:, u * LANES:(u + 1) * LANES]
            return acc
        part = lax.fori_loop(0, nch, body, jnp.zeros((tq, LANES), jnp.int32))
        return jnp.broadcast_to(jnp.sum(part, axis=1, keepdims=True), (tq, LANES))

    def rep(x):
        return pltpu.repeat(x, tc // LANES, axis=1)

    def bit_step(bi, carry):
        thr, cnt_thr = carry
        cand = thr + lax.shift_left(jnp.int32(1), 31 - bi)
        cnt = count(lambda kc, start: kc >= rep(cand))
        ok = cnt >= topk
        return jnp.where(ok, cand, thr), jnp.where(ok, cnt, cnt_thr)

    n_keys = jnp.full((tq, LANES), 0, jnp.int32) + nch * tc
    thr, cnt_thr = lax.fori_loop(0, 32, bit_step, (jnp.full((tq, LANES), INT_MIN, jnp.int32), n_keys))

    excess = jnp.where(thr > INT_MIN, cnt_thr - topk, 0)

    @pl.when(jnp.max(excess) > 0)
    def _():
        c_gt = count(lambda kc, start: kc > rep(thr))
        keep = topk - c_gt

        def idx_step(bi, x):
            cand = x + lax.shift_left(jnp.int32(1), idx_bits - 1 - bi)
            below = count(lambda kc, start: (kc == rep(thr)) & (start + col_iota < rep(cand)))
            return jnp.where(below < keep, cand, x)

        last = lax.fori_loop(0, idx_bits, idx_step, jnp.zeros((tq, LANES), jnp.int32))
        tie_row = excess > 0

        def demote(c, carry):
            start = pl.multiple_of(c * tc, tc)
            kc = keys_ref[:, pl.ds(start, tc)]
            drop = rep(tie_row) & (kc == rep(thr)) & (start + col_iota > rep(last))
            keys_ref[:, pl.ds(start, tc)] = jnp.where(drop, kc - 1, kc)
            return carry

        lax.fori_loop(0, nch, demote, 0)

    sel_thr = jnp.maximum(thr, INT_MIN + 1)

    _flash_init(m_ref, l_ref, acc_ref)

    def attend(j, carry):
        start = pl.multiple_of(j * tc, tc)
        sk = sk_ref[0, pl.ds(start, tc), :]
        v = v_ref[0, pl.ds(start, tc), :]
        s = lax.dot_general(qd_ref[...], sk, _NT, preferred_element_type=jnp.float32)
        sel = keys_ref[:, pl.ds(start, tc)] >= rep(sel_thr)
        s = jnp.concatenate([jnp.where(sel, s[h * tq:(h + 1) * tq, :], -jnp.inf)
                             for h in range(DSA_HEADS)], axis=0)
        _flash_update(s, v, m_ref, l_ref, acc_ref)
        return carry

    lax.fori_loop(0, nch, attend, 0)

    outs = [acc_ref[h * tq:(h + 1) * tq, :] / l_ref[h * tq:(h + 1) * tq, :] for h in range(DSA_HEADS)]
    for p in range(DSA_HEADS // 2):
        o_ref[0, :, p * LANES:(p + 1) * LANES] = jnp.where(low, outs[2 * p], outs[2 * p + 1]).astype(o_ref.dtype)


def _dsa_attention(a3, b3, w3, tq=128, tc=256):
    bsz, seq, _ = a3.shape
    topk = min(DSA_TOPK_MAX, seq // 4)
    idx_bits = int(math.log2(seq))
    assert 2 ** idx_bits == seq and tc % tq == 0
    kern = functools.partial(_dsa_kernel, tq=tq, tc=tc, topk=topk, idx_bits=idx_bits)
    return pl.pallas_call(
        kern,
        grid=(bsz, seq // tq),
        in_specs=[
            pl.BlockSpec((1, tq, 4 * LANES), lambda b, i: (b, i, 2)),
            pl.BlockSpec((1, tq, 2 * LANES), lambda b, i: (b, i, 6)),
            pl.BlockSpec((1, tq, LANES), lambda b, i: (b, i, 0)),
            pl.BlockSpec((1, seq, LANES), lambda b, i: (b, 0, 14)),
            pl.BlockSpec((1, seq, LANES), lambda b, i: (b, 0, 15)),
            pl.BlockSpec((1, seq, LANES), lambda b, i: (b, 0, 18)),
        ],
        out_specs=pl.BlockSpec((1, tq, 2 * LANES), lambda b, i: (b, i, 0)),
        out_shape=jax.ShapeDtypeStruct((bsz, seq, DSA_HEADS * HEAD_DIM), jnp.bfloat16),
        scratch_shapes=[pltpu.VMEM((tq, seq), jnp.int32),
                        pltpu.VMEM((IDX_HEADS * tq, LANES), jnp.bfloat16),
                        pltpu.VMEM((DSA_HEADS * tq, LANES), jnp.bfloat16),
                        pltpu.VMEM((IDX_HEADS, tq, LANES), jnp.float32),
                        pltpu.VMEM((DSA_HEADS * tq, LANES), jnp.float32),
                        pltpu.VMEM((DSA_HEADS * tq, LANES), jnp.float32),
                        pltpu.VMEM((DSA_HEADS * tq, LANES), jnp.float32)],
        compiler_params=_cparams(("parallel", "arbitrary")),
        name="dsa_attention",
    )(a3, a3, w3, a3, a3, b3)


def _out_kernel(x_ref, gate_ref, a_ref, bo_ref, co_ref, wa_ref, wb_ref, wc_ref, g_ref, b_ref, o_ref):
    gate = gate_ref[...].astype(jnp.float32)
    silu = gate / (1.0 + jnp.exp(-gate))

    def gated(ref, lo, hi):
        return (ref[...].astype(jnp.float32) * silu[:, lo:hi]).astype(jnp.bfloat16)

    out = jnp.dot(gated(a_ref, 0, 512), wa_ref[...], preferred_element_type=jnp.float32)
    out = out + jnp.dot(gated(bo_ref, 512, 768), wb_ref[...], preferred_element_type=jnp.float32)
    out = out + jnp.dot(gated(co_ref, 768, 1024), wc_ref[...], preferred_element_type=jnp.float32)
    y = DEEPNORM_ALPHA * x_ref[...] + out
    mu = jnp.mean(y, axis=1, keepdims=True)
    d = y - mu
    var = jnp.mean(d * d, axis=1, keepdims=True)
    o_ref[...] = d * lax.rsqrt(var + LN_EPS) * g_ref[...] + b_ref[...]


def _out_proj(x2d, b2d, a2d, bo2d, co2d, w_out, ln_g, ln_b, tm=256):
    m = x2d.shape[0]
    row = lambda i: (i, 0)
    fixed = lambda i: (0, 0)
    return pl.pallas_call(
        _out_kernel,
        grid=(m // tm,),
        in_specs=[
            pl.BlockSpec((tm, D_MODEL), row),
            pl.BlockSpec((tm, D_MODEL), row),
            pl.BlockSpec((tm, 512), row),
            pl.BlockSpec((tm, 256), row),
            pl.BlockSpec((tm, 256), row),
            pl.BlockSpec((512, D_MODEL), fixed),
            pl.BlockSpec((256, D_MODEL), lambda i: (2, 0)),
            pl.BlockSpec((256, D_MODEL), lambda i: (3, 0)),
            pl.BlockSpec((1, D_MODEL), fixed),
            pl.BlockSpec((1, D_MODEL), fixed),
        ],
        out_specs=pl.BlockSpec((tm, D_MODEL), row),
        out_shape=jax.ShapeDtypeStruct((m, D_MODEL), jnp.float32),
        compiler_params=_cparams(("parallel",)),
        name="out_proj",
    )(x2d, b2d, a2d, bo2d, co2d, w_out, w_out, w_out, ln_g, ln_b)


def _rope_tables(seq):
    inv = ROPE_THETA ** (-jnp.arange(0, HEAD_DIM, 2, dtype=jnp.float32) / HEAD_DIM)
    ang = jnp.arange(seq, dtype=jnp.int32).astype(jnp.float32)[:, None] * inv[None, :]
    reps = LANES // (HEAD_DIM // 2)
    return jnp.tile(jnp.cos(ang), (1, reps)), jnp.tile(jnp.sin(ang), (1, reps))


def kernel(x, w_in, b_f, lam_q1, lam_k1, lam_q2, lam_k2, g_subln, w_out, ln_g, ln_b):
    bsz, seq, _ = x.shape
    m = bsz * seq
    cos, sin = _rope_tables(seq)

    w_a = w_in[:, :, A_IDX] * A_SCALE
    w_ar = (w_a[:, :, A_ROT_PARTNER] * A_ROT_SIGN).astype(jnp.bfloat16)
    w_a = w_a.astype(jnp.bfloat16)
    w_b = (w_in[:, :, B_IDX] * B_SCALE).astype(jnp.bfloat16)
    w_c = (w_in[:, :, C_IDX] * C_SCALE).astype(jnp.bfloat16)
    w_o = w_out.astype(jnp.bfloat16)
    bf_rows = jnp.zeros((DEPTH, 1, LANES), jnp.float32).at[:, 0, F_LANE0:F_LANE0 + FOX_HEADS].set(b_f)

    x2d = x.reshape(m, D_MODEL)
    for l in range(DEPTH):
        lam_init = 0.8 - 0.6 * math.exp(-0.3 * l)
        lam_rows = jnp.zeros((8, LANES), jnp.float32)
        lam_rows = lam_rows.at[0, :HEAD_DIM].set(lam_q1[l]).at[1, :HEAD_DIM].set(lam_k1[l])
        lam_rows = lam_rows.at[2, :HEAD_DIM].set(lam_q2[l]).at[3, :HEAD_DIM].set(lam_k2[l])
        lam_rows = lam_rows.at[4, :].set(lam_init)

        a2d = _proj_rope(x2d, w_a[l], w_ar[l], cos, sin, seq)
        b2d = _proj_plain(x2d, w_b[l])
        wf2d, csum2d = _proj_small(x2d, w_c[l], bf_rows[l], seq)

        a3 = a2d.reshape(bsz, seq, A_WIDTH)
        b3 = b2d.reshape(bsz, seq, B_WIDTH)
        wf3 = wf2d.reshape(bsz, seq, LANES)
        csum3 = csum2d.reshape(bsz, seq, LANES)
        ck = jnp.transpose(csum3[:, :, F_LANE0:F_LANE0 + FOX_HEADS], (0, 2, 1)).reshape(bsz, 2, 2, seq)
        ck4 = jnp.pad(ck, ((0, 0), (0, 0), (0, 6), (0, 0)))

        diff_o = _diff_attention(a3, b3, lam_rows, g_subln[l].reshape(1, LANES))
        fox_o = _fox_attention(b3, csum3, ck4)
        dsa_o = _dsa_attention(a3, b3, wf3)

        x2d = _out_proj(x2d, b2d, diff_o.reshape(m, 512), dsa_o.reshape(m, 256), fox_o.reshape(m, 256),
                        w_o[l], ln_g[l].reshape(1, D_MODEL), ln_b[l].reshape(1, D_MODEL))
    return x2d.reshape(bsz, seq, D_MODEL)
```

```python
import functools
import math

import numpy as np
import jax
import jax.numpy as jnp
from jax import lax
from jax.experimental import pallas as pl
from jax.experimental.pallas import tpu as pltpu

D_MODEL = 1024
DEPTH = 4
HEAD_DIM = 64
DIFF_HEADS = 4
DSA_HEADS = 4
IDX_HEADS = 8
FOX_HEADS = 4
DSA_TOPK_MAX = 256
ROPE_THETA = 10000.0
LN_EPS = 1e-5
SUBLN_EPS = 1e-5
DEEPNORM_ALPHA = (2 * DEPTH) ** 0.25
IDX_WEIGHT_SCALE = (IDX_HEADS * HEAD_DIM) ** -0.5
LOG2E = math.log2(math.e)
Q_SCALE = HEAD_DIM ** -0.5 * LOG2E

LANES = 128
VMEM_LIMIT_BYTES = 56 * 1024 * 1024

_OFF = {}
_o = 0
for _name, _n in (('diff_q', 512), ('diff_k', 512), ('diff_v', 512), ('dsa_q', 256), ('dsa_k', 64),
                  ('dsa_v', 64), ('idx_q', 512), ('idx_k', 64), ('idx_w', 8), ('fox_q', 256),
                  ('fox_k', 256), ('fox_v', 256), ('fox_f', 4), ('gate', 1024)):
    _OFF[_name] = (_o, _n)
    _o += _n
IN_WIDTH = _o


def _cols(name, scale=1.0, repeat=1):
    o, n = _OFF[name]
    idx = np.tile(np.arange(o, o + n), repeat)
    return idx, np.full(idx.shape, scale, np.float32)


def _pad(n):
    return np.zeros((n,), np.int64), np.zeros((n,), np.float32)


def _layout(parts):
    idx = np.concatenate([p[0] for p in parts])
    scale = np.concatenate([p[1] for p in parts])
    return idx, scale


A_IDX, A_SCALE = _layout([_cols('diff_q', Q_SCALE), _cols('diff_k'), _cols('idx_q'),
                          _cols('dsa_q', Q_SCALE), _cols('idx_k', repeat=2), _cols('dsa_k', repeat=2)])
A_WIDTH = A_IDX.shape[0]
B_IDX, B_SCALE = _layout([_cols('gate'), _cols('diff_v'), _cols('fox_q', Q_SCALE), _cols('fox_k'),
                          _cols('fox_v'), _cols('dsa_v', repeat=2), _pad(128)])
B_WIDTH = B_IDX.shape[0]
C_IDX, C_SCALE = _layout([_cols('idx_w'), _cols('fox_f'), _pad(LANES - 12)])
W_LANE0 = 0
F_LANE0 = 8

_c = np.arange(A_WIDTH)
_first_half = (_c % HEAD_DIM) < HEAD_DIM // 2
A_ROT_PARTNER = np.where(_first_half, _c + HEAD_DIM // 2, _c - HEAD_DIM // 2)
A_ROT_SIGN = np.where(_first_half, -1.0, 1.0).astype(np.float32)

_NT = (((1,), (1,)), ((), ()))
INT_MIN = -2 ** 31


def _cparams(sem):
    return pltpu.CompilerParams(dimension_semantics=sem, vmem_limit_bytes=VMEM_LIMIT_BYTES)


def _proj_rope_kernel(x_ref, w_ref, wr_ref, cos_ref, sin_ref, o_ref):
    xb = x_ref[...].astype(jnp.bfloat16)
    h = jnp.dot(xb, w_ref[...], preferred_element_type=jnp.float32)
    hr = jnp.dot(xb, wr_ref[...], preferred_element_type=jnp.float32)
    cos = cos_ref[...]
    sin = sin_ref[...]
    for c in range(o_ref.shape[1] // LANES):
        sl = slice(c * LANES, (c + 1) * LANES)
        o_ref[:, sl] = (h[:, sl] * cos + hr[:, sl] * sin).astype(o_ref.dtype)


def _proj_rope(x2d, w, wr, cos, sin, seq, tm=512, tn=512):
    m = x2d.shape[0]
    nseq = seq // tm
    return pl.pallas_call(
        _proj_rope_kernel,
        grid=(m // tm, A_WIDTH // tn),
        in_specs=[
            pl.BlockSpec((tm, D_MODEL), lambda i, j: (i, 0)),
            pl.BlockSpec((D_MODEL, tn), lambda i, j: (0, j)),
            pl.BlockSpec((D_MODEL, tn), lambda i, j: (0, j)),
            pl.BlockSpec((tm, LANES), lambda i, j: (i % nseq, 0)),
            pl.BlockSpec((tm, LANES), lambda i, j: (i % nseq, 0)),
        ],
        out_specs=pl.BlockSpec((tm, tn), lambda i, j: (i, j)),
        out_shape=jax.ShapeDtypeStruct((m, A_WIDTH), jnp.bfloat16),
        compiler_params=_cparams(("parallel", "arbitrary")),
        name="proj_rope",
    )(x2d, w, wr, cos, sin)


def _proj_plain_kernel(x_ref, w_ref, o_ref):
    xb = x_ref[...].astype(jnp.bfloat16)
    o_ref[...] = jnp.dot(xb, w_ref[...], preferred_element_type=jnp.float32).astype(o_ref.dtype)


def _proj_plain(x2d, w, tm=512, tn=512):
    m = x2d.shape[0]
    return pl.pallas_call(
        _proj_plain_kernel,
        grid=(m // tm, B_WIDTH // tn),
        in_specs=[
            pl.BlockSpec((tm, D_MODEL), lambda i, j: (i, 0)),
            pl.BlockSpec((D_MODEL, tn), lambda i, j: (0, j)),
        ],
        out_specs=pl.BlockSpec((tm, tn), lambda i, j: (i, j)),
        out_shape=jax.ShapeDtypeStruct((m, B_WIDTH), jnp.bfloat16),
        compiler_params=_cparams(("parallel", "arbitrary")),
        name="proj_plain",
    )(x2d, w)


def _split3(v):
    hi = v.astype(jnp.bfloat16)
    r = v - hi.astype(jnp.float32)
    mid = r.astype(jnp.bfloat16)
    lo = (r - mid.astype(jnp.float32)).astype(jnp.bfloat16)
    return hi, mid, lo


def _proj_small_kernel(x_ref, w_ref, bf_ref, tri_ref, o_ref, c_ref, carry_ref, *, blocks_per_seq):
    i = pl.program_id(0)
    xb = x_ref[...].astype(jnp.bfloat16)
    h = jnp.dot(xb, w_ref[...], preferred_element_type=jnp.float32)
    lane = lax.broadcasted_iota(jnp.int32, h.shape, 1)
    ff = h + bf_ref[...]
    logf = jnp.minimum(ff, 0.0) - jnp.log(1.0 + jnp.exp(-jnp.abs(ff)))
    o = jnp.where(lane < F_LANE0, h * IDX_WEIGHT_SCALE, logf * LOG2E)
    o_ref[...] = o

    @pl.when(i % blocks_per_seq == 0)
    def _():
        carry_ref[...] = jnp.zeros_like(carry_ref)

    tri = tri_ref[...]
    hi, mid, lo = _split3(o)
    cs = (jnp.dot(tri, hi, preferred_element_type=jnp.float32)
          + jnp.dot(tri, mid, preferred_element_type=jnp.float32)
          + jnp.dot(tri, lo, preferred_element_type=jnp.float32))
    cs = cs + carry_ref[0:1, :]
    c_ref[...] = cs
    tm = cs.shape[0]
    carry_ref[...] = jnp.broadcast_to(cs[tm - 1:tm, :], carry_ref.shape)


def _proj_small(x2d, w, bf_row, seq, tm=512):
    m = x2d.shape[0]
    tri = jnp.asarray(np.tril(np.ones((tm, tm), np.float32)), jnp.bfloat16)
    kern = functools.partial(_proj_small_kernel, blocks_per_seq=seq // tm)
    return pl.pallas_call(
        kern,
        grid=(m // tm,),
        in_specs=[
            pl.BlockSpec((tm, D_MODEL), lambda i: (i, 0)),
            pl.BlockSpec((D_MODEL, LANES), lambda i: (0, 0)),
            pl.BlockSpec((1, LANES), lambda i: (0, 0)),
            pl.BlockSpec((tm, tm), lambda i: (0, 0)),
        ],
        out_specs=[pl.BlockSpec((tm, LANES), lambda i: (i, 0)),
                   pl.BlockSpec((tm, LANES), lambda i: (i, 0))],
        out_shape=[jax.ShapeDtypeStruct((m, LANES), jnp.float32),
                   jax.ShapeDtypeStruct((m, LANES), jnp.float32)],
        scratch_shapes=[pltpu.VMEM((8, LANES), jnp.float32)],
        compiler_params=_cparams(("arbitrary",)),
        name="proj_small",
    )(x2d, w, bf_row, tri)


def _flash_init(m_ref, acc_ref):
    m_ref[...] = jnp.full(m_ref.shape, -1e30, jnp.float32)
    acc_ref[...] = jnp.zeros(acc_ref.shape, jnp.float32)


def _flash_rows(s, v1, m_ref, acc_ref, rows):
    tk = s.shape[1]
    m_prev = m_ref[rows, :]
    m_next = jnp.maximum(m_prev, jnp.max(s, axis=1, keepdims=True))
    p = jnp.exp2((s - pltpu.repeat(m_next, tk // LANES, axis=1)).astype(jnp.bfloat16))
    alpha = jnp.exp2(m_prev - m_next)
    m_ref[rows, :] = m_next
    pv = jnp.dot(p, v1, preferred_element_type=jnp.float32)
    acc_ref[rows, :] = acc_ref[rows, :] * pltpu.repeat(alpha, v1.shape[1] // LANES, axis=1) + pv


def _pipelined_chunks(n_full, produce, consume, buf0, buf1):
    produce(0, buf0)

    def body(p, carry):
        produce(2 * p + 1, buf1)
        consume(2 * p, buf0, False)
        produce(2 * p + 2, buf0)
        consume(2 * p + 1, buf1, False)
        return carry

    lax.fori_loop(0, n_full // 2, body, 0)

    @pl.when(n_full % 2 == 1)
    def _():
        produce(n_full, buf1)
        consume(n_full - 1, buf0, False)
        consume(n_full, buf1, True)

    @pl.when(n_full % 2 == 0)
    def _():
        consume(n_full, buf0, True)


def _stack_pair(q, qs_ref, t):
    lane = lax.broadcasted_iota(jnp.int32, q.shape, 1)
    zero = jnp.zeros_like(q)
    qs_ref[0:t, :] = jnp.where(lane < HEAD_DIM, q, zero)
    qs_ref[t:2 * t, :] = jnp.where(lane >= HEAD_DIM, q, zero)


def _with_ones(v):
    return jnp.concatenate([v, jnp.ones((v.shape[0], LANES), v.dtype)], axis=1)


def _diff_kernel(lam_ref, g_ref, q_ref, k_ref, v_ref, o_ref, qs_ref, m_ref, acc_ref, s0_ref, s1_ref,
                 *, tq, tk):
    i = pl.program_id(2)
    _stack_pair(q_ref[0], qs_ref, tq)
    _flash_init(m_ref, acc_ref)
    groups = (slice(0, tq), slice(tq, 2 * tq))

    def produce(j, s_ref):
        start = pl.multiple_of(j * tk, tk)
        k = k_ref[0, pl.ds(start, tk), :]
        for rows in groups:
            s_ref[rows, :] = lax.dot_general(qs_ref[rows, :], k, _NT, preferred_element_type=jnp.float32)

    def consume(j, s_ref, last):
        start = pl.multiple_of(j * tk, tk)
        v1 = _with_ones(v_ref[0, pl.ds(start, tk), :])
        for rows in groups:
            s = s_ref[rows, :]
            if last:
                r = lax.broadcasted_iota(jnp.int32, (tq, tk), 0)
                c = lax.broadcasted_iota(jnp.int32, (tq, tk), 1)
                s = jnp.where(start + c <= i * tq + r, s, -jnp.inf)
            _flash_rows(s, v1, m_ref, acc_ref, rows)

    _pipelined_chunks((i * tq) // tk, produce, consume, s0_ref, s1_ref)

    lam_rows = lam_ref[...]
    s1 = jnp.sum(lam_rows[0:1, :] * lam_rows[1:2, :], axis=1, keepdims=True)
    s2 = jnp.sum(lam_rows[2:3, :] * lam_rows[3:4, :], axis=1, keepdims=True)
    lam_init = lam_rows[4:5, 0:1]
    lam = jnp.exp(s1) - jnp.exp(s2) + lam_init
    o0 = acc_ref[0:tq, 0:LANES] / acc_ref[0:tq, LANES:2 * LANES]
    o1 = acc_ref[tq:2 * tq, 0:LANES] / acc_ref[tq:2 * tq, LANES:2 * LANES]
    a = o0 - lam * o1
    ms = jnp.mean(a * a, axis=1, keepdims=True)
    a = a * lax.rsqrt(ms + SUBLN_EPS) * g_ref[...] * (1.0 - lam_init)
    o_ref[0] = a.astype(o_ref.dtype)


def _diff_attention(a3, b3, lam_rows, g_row, tq=256, tk=1024):
    bsz, seq, _ = a3.shape
    tk = min(tk, seq)
    kern = functools.partial(_diff_kernel, tq=tq, tk=tk)
    return pl.pallas_call(
        kern,
        grid=(bsz, DIFF_HEADS, seq // tq),
        in_specs=[
            pl.BlockSpec((8, LANES), lambda b, h, i: (0, 0)),
            pl.BlockSpec((1, LANES), lambda b, h, i: (0, 0)),
            pl.BlockSpec((1, tq, LANES), lambda b, h, i: (b, i, h)),
            pl.BlockSpec((1, seq, LANES), lambda b, h, i: (b, 0, 4 + h)),
            pl.BlockSpec((1, seq, LANES), lambda b, h, i: (b, 0, 8 + h)),
        ],
        out_specs=pl.BlockSpec((1, tq, LANES), lambda b, h, i: (b, i, h)),
        out_shape=jax.ShapeDtypeStruct((bsz, seq, DIFF_HEADS * LANES), jnp.bfloat16),
        scratch_shapes=[pltpu.VMEM((2 * tq, LANES), jnp.bfloat16),
                        pltpu.VMEM((2 * tq, LANES), jnp.float32),
                        pltpu.VMEM((2 * tq, 2 * LANES), jnp.float32),
                        pltpu.VMEM((2 * tq, tk), jnp.float32),
                        pltpu.VMEM((2 * tq, tk), jnp.float32)],
        compiler_params=_cparams(("parallel", "parallel", "arbitrary")),
        name="diff_attention",
    )(lam_rows, g_row, a3, a3, b3)


def _fox_kernel(q_ref, k_ref, v_ref, cq_ref, ck_ref, o_ref, qs_ref, cqs_ref, m_ref, acc_ref, s0_ref, s1_ref,
                *, tq, tk):
    g = pl.program_id(1)
    i = pl.program_id(2)
    _stack_pair(q_ref[0], qs_ref, tq)
    lane = lax.broadcasted_iota(jnp.int32, (tq, LANES), 1)
    ctile = cq_ref[0]
    for a in range(2):
        col = jnp.sum(jnp.where(lane == F_LANE0 + 2 * g + a, ctile, 0.0), axis=1, keepdims=True)
        cqs_ref[a * tq:(a + 1) * tq, :] = jnp.broadcast_to(col, (tq, LANES))
    _flash_init(m_ref, acc_ref)
    groups = (slice(0, tq), slice(tq, 2 * tq))

    def produce(j, s_ref):
        start = pl.multiple_of(j * tk, tk)
        k = k_ref[0, pl.ds(start, tk), :]
        for rows in groups:
            s_ref[rows, :] = lax.dot_general(qs_ref[rows, :], k, _NT, preferred_element_type=jnp.float32)

    def consume(j, s_ref, last):
        start = pl.multiple_of(j * tk, tk)
        v1 = _with_ones(v_ref[0, pl.ds(start, tk), :])
        ck = ck_ref[0, 0, :, pl.ds(start, tk)]
        for a, rows in enumerate(groups):
            bias = pltpu.repeat(cqs_ref[rows, :], tk // LANES, axis=1) - jnp.broadcast_to(ck[a:a + 1, :], (tq, tk))
            s = s_ref[rows, :] + bias
            if last:
                r = lax.broadcasted_iota(jnp.int32, (tq, tk), 0)
                c = lax.broadcasted_iota(jnp.int32, (tq, tk), 1)
                s = jnp.where(start + c <= i * tq + r, s, -jnp.inf)
            _flash_rows(s, v1, m_ref, acc_ref, rows)

    _pipelined_chunks((i * tq) // tk, produce, consume, s0_ref, s1_ref)

    o0 = acc_ref[0:tq, 0:LANES] / acc_ref[0:tq, LANES:2 * LANES]
    o1 = acc_ref[tq:2 * tq, 0:LANES] / acc_ref[tq:2 * tq, LANES:2 * LANES]
    o_ref[0] = jnp.where(lane < HEAD_DIM, o0, o1).astype(o_ref.dtype)


def _fox_attention(b3, csum3, ck4, tq=256, tk=1024):
    bsz, seq, _ = b3.shape
    tk = min(tk, seq)
    kern = functools.partial(_fox_kernel, tq=tq, tk=tk)
    return pl.pallas_call(
        kern,
        grid=(bsz, FOX_HEADS // 2, seq // tq),
        in_specs=[
            pl.BlockSpec((1, tq, LANES), lambda b, g, i: (b, i, 12 + g)),
            pl.BlockSpec((1, seq, LANES), lambda b, g, i: (b, 0, 14 + g)),
            pl.BlockSpec((1, seq, LANES), lambda b, g, i: (b, 0, 16 + g)),
            pl.BlockSpec((1, tq, LANES), lambda b, g, i: (b, i, 0)),
            pl.BlockSpec((1, 1, 8, seq), lambda b, g, i: (b, g, 0, 0)),
        ],
        out_specs=pl.BlockSpec((1, tq, LANES), lambda b, g, i: (b, i, g)),
        out_shape=jax.ShapeDtypeStruct((bsz, seq, FOX_HEADS * HEAD_DIM), jnp.bfloat16),
        scratch_shapes=[pltpu.VMEM((2 * tq, LANES), jnp.bfloat16),
                        pltpu.VMEM((2 * tq, LANES), jnp.float32),
                        pltpu.VMEM((2 * tq, LANES), jnp.float32),
                        pltpu.VMEM((2 * tq, 2 * LANES), jnp.float32),
                        pltpu.VMEM((2 * tq, tk), jnp.float32),
                        pltpu.VMEM((2 * tq, tk), jnp.float32)],
        compiler_params=_cparams(("parallel", "parallel", "arbitrary")),
        name="fox_attention",
    )(b3, b3, b3, csum3, ck4)


def _score_key(score):
    bits = pltpu.bitcast(score, jnp.int32)
    return bits ^ ((bits >> 31) & 0x7FFFFFFF)


def _dsa_kernel(iq_ref, sq_ref, w_ref, ik_ref, sk_ref, v_ref, o_ref,
                keys_ref, qi_ref, qd_ref, wb_ref, m_ref, acc_ref, s0_ref, s1_ref,
                *, tq, tc, topk, idx_bits):
    i = pl.program_id(1)
    lane = lax.broadcasted_iota(jnp.int32, (tq, LANES), 1)
    low = lane < HEAD_DIM

    for h in range(IDX_HEADS):
        blk = iq_ref[0, :, (h // 2) * LANES:(h // 2 + 1) * LANES]
        qi_ref[h * tq:(h + 1) * tq, :] = jnp.where(low if h % 2 == 0 else ~low, blk, jnp.zeros_like(blk))
    for h in range(DSA_HEADS):
        blk = sq_ref[0, :, (h // 2) * LANES:(h // 2 + 1) * LANES]
        qd_ref[h * tq:(h + 1) * tq, :] = jnp.where(low if h % 2 == 0 else ~low, blk, jnp.zeros_like(blk))
    wt = w_ref[0]
    for h in range(IDX_HEADS):
        col = jnp.sum(jnp.where(lane == W_LANE0 + h, wt, 0.0), axis=1, keepdims=True)
        wb_ref[h] = jnp.broadcast_to(col, (tq, LANES))

    nlast = (i * tq) // tc
    nch = nlast + 1
    row_pos = i * tq + lax.broadcasted_iota(jnp.int32, (tq, tc), 0)
    col_iota = lax.broadcasted_iota(jnp.int32, (tq, tc), 1)
    idx_groups = (slice(0, 4 * tq), slice(4 * tq, 8 * tq))
    dsa_groups = (slice(0, 2 * tq), slice(2 * tq, 4 * tq))

    def rep(x):
        return pltpu.repeat(x, tc // LANES, axis=1)

    def idx_produce(j, s_ref):
        start = pl.multiple_of(j * tc, tc)
        ik = ik_ref[0, pl.ds(start, tc), :]
        for rows in idx_groups:
            s_ref[rows, :] = lax.dot_general(qi_ref[rows, :], ik, _NT, preferred_element_type=jnp.float32)

    def idx_consume(j, s_ref, last):
        start = pl.multiple_of(j * tc, tc)
        score = jnp.zeros((tq, tc), jnp.float32)
        for h in range(IDX_HEADS):
            score = score + jnp.maximum(s_ref[h * tq:(h + 1) * tq, :], 0.0) * rep(wb_ref[h])
        key = _score_key(score)
        if last:
            key = jnp.where(start + col_iota <= row_pos, key, INT_MIN)
        keys_ref[:, pl.ds(start, tc)] = key

    _pipelined_chunks(nlast, idx_produce, idx_consume, s0_ref, s1_ref)

    def count(pred):
        def body(c, acc):
            start = pl.multiple_of(c * tc, tc)
            hit = pred(keys_ref[:, pl.ds(start, tc)], start).astype(jnp.int32)
            for u in range(tc // LANES):
                acc = acc + hit[:, u * LANES:(u + 1) * LANES]
            return acc
        part = lax.fori_loop(0, nch, body, jnp.zeros((tq, LANES), jnp.int32))
        return jnp.broadcast_to(jnp.sum(part, axis=1, keepdims=True), (tq, LANES))

    n_row = i * tq + lax.broadcasted_iota(jnp.int32, (tq, LANES), 0) + 1

    def pending(state):
        bi, thr, cnt_thr = state
        return jnp.logical_and(bi < 32, jnp.max(jnp.where(n_row > topk, cnt_thr, topk)) > topk)

    def bit_step(state):
        bi, thr, cnt_thr = state
        cand = thr + lax.shift_left(jnp.int32(1), 31 - bi)
        cnt = count(lambda kc, start: kc >= rep(cand))
        ok = cnt >= topk
        return bi + 1, jnp.where(ok, cand, thr), jnp.where(ok, cnt, cnt_thr)

    n_keys = jnp.zeros((tq, LANES), jnp.int32) + nch * tc
    _, thr, cnt_thr = lax.while_loop(
        pending, bit_step, (jnp.int32(0), jnp.full((tq, LANES), INT_MIN, jnp.int32), n_keys))

    excess = jnp.where(thr > INT_MIN, cnt_thr - topk, 0)

    @pl.when(jnp.max(excess) > 0)
    def _():
        c_gt = count(lambda kc, start: kc > rep(thr))
        keep = topk - c_gt

        def idx_step(bi, x):
            cand = x + lax.shift_left(jnp.int32(1), idx_bits - 1 - bi)
            below = count(lambda kc, start: (kc == rep(thr)) & (start + col_iota < rep(cand)))
            return jnp.where(below < keep, cand, x)

        last = lax.fori_loop(0, idx_bits, idx_step, jnp.zeros((tq, LANES), jnp.int32))

        def demote(c, carry):
            start = pl.multiple_of(c * tc, tc)
            kc = keys_ref[:, pl.ds(start, tc)]
            drop = (rep(excess) > 0) & (kc == rep(thr)) & (start + col_iota > rep(last))
            keys_ref[:, pl.ds(start, tc)] = jnp.where(drop, kc - 1, kc)
            return carry

        lax.fori_loop(0, nch, demote, 0)

    sel_thr = jnp.maximum(thr, INT_MIN + 1)

    _flash_init(m_ref, acc_ref)
    vlow = lax.broadcasted_iota(jnp.int32, (tc, LANES), 1) < HEAD_DIM

    def att_produce(j, s_ref):
        start = pl.multiple_of(j * tc, tc)
        sk = sk_ref[0, pl.ds(start, tc), :]
        for rows in dsa_groups:
            s_ref[rows, :] = lax.dot_general(qd_ref[rows, :], sk, _NT, preferred_element_type=jnp.float32)

    def att_consume(j, s_ref, last):
        start = pl.multiple_of(j * tc, tc)
        v2 = v_ref[0, pl.ds(start, tc), :]
        one = jnp.ones_like(v2)
        v_even = jnp.where(vlow, v2, one)
        v_odd = jnp.where(vlow, one, v2)
        sel = keys_ref[:, pl.ds(start, tc)] >= rep(sel_thr)
        for h in range(DSA_HEADS):
            rows = slice(h * tq, (h + 1) * tq)
            s = jnp.where(sel, s_ref[rows, :], -jnp.inf)
            _flash_rows(s, v_even if h % 2 == 0 else v_odd, m_ref, acc_ref, rows)

    _pipelined_chunks(nlast, att_produce, att_consume, s0_ref, s1_ref)

    outs = []
    for h in range(DSA_HEADS):
        acc = acc_ref[h * tq:(h + 1) * tq, :]
        outs.append(acc / pltpu.roll(acc, HEAD_DIM, axis=1))
    for p in range(DSA_HEADS // 2):
        o_ref[0, :, p * LANES:(p + 1) * LANES] = jnp.where(low, outs[2 * p], outs[2 * p + 1]).astype(o_ref.dtype)


def _dsa_attention(a3, b3, w3, tq=128, tc=512):
    bsz, seq, _ = a3.shape
    topk = min(DSA_TOPK_MAX, seq // 4)
    idx_bits = int(math.log2(seq))
    assert 2 ** idx_bits == seq and tc % tq == 0
    kern = functools.partial(_dsa_kernel, tq=tq, tc=tc, topk=topk, idx_bits=idx_bits)
    return pl.pallas_call(
        kern,
        grid=(bsz, seq // tq),
        in_specs=[
            pl.BlockSpec((1, tq, 4 * LANES), lambda b, i: (b, i, 2)),
            pl.BlockSpec((1, tq, 2 * LANES), lambda b, i: (b, i, 6)),
            pl.BlockSpec((1, tq, LANES), lambda b, i: (b, i, 0)),
            pl.BlockSpec((1, seq, LANES), lambda b, i: (b, 0, 14)),
            pl.BlockSpec((1, seq, LANES), lambda b, i: (b, 0, 15)),
            pl.BlockSpec((1, seq, LANES), lambda b, i: (b, 0, 18)),
        ],
        out_specs=pl.BlockSpec((1, tq, 2 * LANES), lambda b, i: (b, i, 0)),
        out_shape=jax.ShapeDtypeStruct((bsz, seq, DSA_HEADS * HEAD_DIM), jnp.bfloat16),
        scratch_shapes=[pltpu.VMEM((tq, seq), jnp.int32),
                        pltpu.VMEM((IDX_HEADS * tq, LANES), jnp.bfloat16),
                        pltpu.VMEM((DSA_HEADS * tq, LANES), jnp.bfloat16),
                        pltpu.VMEM((IDX_HEADS, tq, LANES), jnp.float32),
                        pltpu.VMEM((DSA_HEADS * tq, LANES), jnp.float32),
                        pltpu.VMEM((DSA_HEADS * tq, LANES), jnp.float32),
                        pltpu.VMEM((IDX_HEADS * tq, tc), jnp.float32),
                        pltpu.VMEM((IDX_HEADS * tq, tc), jnp.float32)],
        compiler_params=_cparams(("parallel", "arbitrary")),
        name="dsa_attention",
    )(a3, a3, w3, a3, a3, b3)


def _out_kernel(x_ref, gate_ref, a_ref, bo_ref, co_ref, wa_ref, wb_ref, wc_ref, g_ref, b_ref, o_ref):
    gate = gate_ref[...].astype(jnp.float32)
    silu = gate / (1.0 + jnp.exp(-gate))

    def gated(ref, lo, hi):
        return (ref[...].astype(jnp.float32) * silu[:, lo:hi]).astype(jnp.bfloat16)

    out = jnp.dot(gated(a_ref, 0, 512), wa_ref[...], preferred_element_type=jnp.float32)
    out = out + jnp.dot(gated(bo_ref, 512, 768), wb_ref[...], preferred_element_type=jnp.float32)
    out = out + jnp.dot(gated(co_ref, 768, 1024), wc_ref[...], preferred_element_type=jnp.float32)
    y = DEEPNORM_ALPHA * x_ref[...] + out
    mu = jnp.mean(y, axis=1, keepdims=True)
    d = y - mu
    var = jnp.mean(d * d, axis=1, keepdims=True)
    o_ref[...] = d * lax.rsqrt(var + LN_EPS) * g_ref[...] + b_ref[...]


def _out_proj(x2d, b2d, a2d, bo2d, co2d, w_out, ln_g, ln_b, tm=256):
    m = x2d.shape[0]
    row = lambda i: (i, 0)
    fixed = lambda i: (0, 0)
    return pl.pallas_call(
        _out_kernel,
        grid=(m // tm,),
        in_specs=[
            pl.BlockSpec((tm, D_MODEL), row),
            pl.BlockSpec((tm, D_MODEL), row),
            pl.BlockSpec((tm, 512), row),
            pl.BlockSpec((tm, 256), row),
            pl.BlockSpec((tm, 256), row),
            pl.BlockSpec((512, D_MODEL), fixed),
            pl.BlockSpec((256, D_MODEL), lambda i: (2, 0)),
            pl.BlockSpec((256, D_MODEL), lambda i: (3, 0)),
            pl.BlockSpec((1, D_MODEL), fixed),
            pl.BlockSpec((1, D_MODEL), fixed),
        ],
        out_specs=pl.BlockSpec((tm, D_MODEL), row),
        out_shape=jax.ShapeDtypeStruct((m, D_MODEL), jnp.float32),
        compiler_params=_cparams(("parallel",)),
        name="out_proj",
    )(x2d, b2d, a2d, bo2d, co2d, w_out, w_out, w_out, ln_g, ln_b)


def _rope_tables(seq):
    inv = ROPE_THETA ** (-jnp.arange(0, HEAD_DIM, 2, dtype=jnp.float32) / HEAD_DIM)
    ang = jnp.arange(seq, dtype=jnp.int32).astype(jnp.float32)[:, None] * inv[None, :]
    reps = LANES // (HEAD_DIM // 2)
    return jnp.tile(jnp.cos(ang), (1, reps)), jnp.tile(jnp.sin(ang), (1, reps))


def kernel(x, w_in, b_f, lam_q1, lam_k1, lam_q2, lam_k2, g_subln, w_out, ln_g, ln_b):
    bsz, seq, _ = x.shape
    m = bsz * seq
    cos, sin = _rope_tables(seq)

    w_a = w_in[:, :, A_IDX] * A_SCALE
    w_ar = (w_a[:, :, A_ROT_PARTNER] * A_ROT_SIGN).astype(jnp.bfloat16)
    w_a = w_a.astype(jnp.bfloat16)
    w_b = (w_in[:, :, B_IDX] * B_SCALE).astype(jnp.bfloat16)
    w_c = (w_in[:, :, C_IDX] * C_SCALE).astype(jnp.bfloat16)
    w_o = w_out.astype(jnp.bfloat16)
    bf_rows = jnp.zeros((DEPTH, 1, LANES), jnp.float32).at[:, 0, F_LANE0:F_LANE0 + FOX_HEADS].set(b_f)

    x2d = x.reshape(m, D_MODEL)
    for l in range(DEPTH):
        lam_init = 0.8 - 0.6 * math.exp(-0.3 * l)
        lam_rows = jnp.zeros((8, LANES), jnp.float32)
        lam_rows = lam_rows.at[0, :HEAD_DIM].set(lam_q1[l]).at[1, :HEAD_DIM].set(lam_k1[l])
        lam_rows = lam_rows.at[2, :HEAD_DIM].set(lam_q2[l]).at[3, :HEAD_DIM].set(lam_k2[l])
        lam_rows = lam_rows.at[4, :].set(lam_init)

        a2d = _proj_rope(x2d, w_a[l], w_ar[l], cos, sin, seq)
        b2d = _proj_plain(x2d, w_b[l])
        wf2d, csum2d = _proj_small(x2d, w_c[l], bf_rows[l], seq)

        a3 = a2d.reshape(bsz, seq, A_WIDTH)
        b3 = b2d.reshape(bsz, seq, B_WIDTH)
        wf3 = wf2d.reshape(bsz, seq, LANES)
        csum3 = csum2d.reshape(bsz, seq, LANES)
        ck = jnp.transpose(csum3[:, :, F_LANE0:F_LANE0 + FOX_HEADS], (0, 2, 1)).reshape(bsz, 2, 2, seq)
        ck4 = jnp.pad(ck, ((0, 0), (0, 0), (0, 6), (0, 0)))

        diff_o = _diff_attention(a3, b3, lam_rows, g_subln[l].reshape(1, LANES))
        fox_o = _fox_attention(b3, csum3, ck4)
        dsa_o = _dsa_attention(a3, b3, wf3)

        x2d = _out_proj(x2d, b2d, diff_o.reshape(m, 512), dsa_o.reshape(m, 256), fox_o.reshape(m, 256),
                        w_o[l], ln_g[l].reshape(1, D_MODEL), ln_b[l].reshape(1, D_MODEL))
    return x2d.reshape(bsz, seq, D_MODEL)
```

```python
import functools
import math

import numpy as np
import jax
import jax.numpy as jnp
from jax import lax
from jax.experimental import pallas as pl
from jax.experimental.pallas import tpu as pltpu

D_MODEL = 1024
DEPTH = 4
HEAD_DIM = 64
DIFF_HEADS = 4
DSA_HEADS = 4
IDX_HEADS = 8
FOX_HEADS = 4
DSA_TOPK_MAX = 256
ROPE_THETA = 10000.0
LN_EPS = 1e-5
SUBLN_EPS = 1e-5
DEEPNORM_ALPHA = (2 * DEPTH) ** 0.25
IDX_WEIGHT_SCALE = (IDX_HEADS * HEAD_DIM) ** -0.5
LOG2E = math.log2(math.e)
Q_SCALE = HEAD_DIM ** -0.5 * LOG2E

LANES = 128
VMEM_LIMIT_BYTES = 56 * 1024 * 1024

_OFF = {}
_o = 0
for _name, _n in (('diff_q', 512), ('diff_k', 512), ('diff_v', 512), ('dsa_q', 256), ('dsa_k', 64),
                  ('dsa_v', 64), ('idx_q', 512), ('idx_k', 64), ('idx_w', 8), ('fox_q', 256),
                  ('fox_k', 256), ('fox_v', 256), ('fox_f', 4), ('gate', 1024)):
    _OFF[_name] = (_o, _n)
    _o += _n
IN_WIDTH = _o


def _cols(name, scale=1.0, repeat=1):
    o, n = _OFF[name]
    idx = np.tile(np.arange(o, o + n), repeat)
    return idx, np.full(idx.shape, scale, np.float32)


def _pad(n):
    return np.zeros((n,), np.int64), np.zeros((n,), np.float32)


def _layout(parts):
    idx = np.concatenate([p[0] for p in parts])
    scale = np.concatenate([p[1] for p in parts])
    return idx, scale


A_IDX, A_SCALE = _layout([_cols('diff_q', Q_SCALE), _cols('diff_k'), _cols('idx_q'),
                          _cols('dsa_q', Q_SCALE), _cols('idx_k', repeat=2), _cols('dsa_k', repeat=2)])
A_WIDTH = A_IDX.shape[0]
B_IDX, B_SCALE = _layout([_cols('gate'), _cols('diff_v'), _cols('fox_q', Q_SCALE), _cols('fox_k'),
                          _cols('fox_v'), _cols('dsa_v', repeat=2), _pad(128)])
B_WIDTH = B_IDX.shape[0]
C_IDX, C_SCALE = _layout([_cols('idx_w'), _cols('fox_f'), _pad(LANES - 12)])
W_LANE0 = 0
F_LANE0 = 8

_c = np.arange(A_WIDTH)
_first_half = (_c % HEAD_DIM) < HEAD_DIM // 2
A_ROT_PARTNER = np.where(_first_half, _c + HEAD_DIM // 2, _c - HEAD_DIM // 2)
A_ROT_SIGN = np.where(_first_half, -1.0, 1.0).astype(np.float32)

_NT = (((1,), (1,)), ((), ()))
INT_MIN = -2 ** 31
KEY_NEG_INF = -2139095041


def _cparams(sem):
    return pltpu.CompilerParams(dimension_semantics=sem, vmem_limit_bytes=VMEM_LIMIT_BYTES)


def _proj_rope_kernel(x_ref, w_ref, wr_ref, cos_ref, sin_ref, o_ref):
    xb = x_ref[...].astype(jnp.bfloat16)
    h = jnp.dot(xb, w_ref[...], preferred_element_type=jnp.float32)
    hr = jnp.dot(xb, wr_ref[...], preferred_element_type=jnp.float32)
    cos = cos_ref[...]
    sin = sin_ref[...]
    for c in range(o_ref.shape[1] // LANES):
        sl = slice(c * LANES, (c + 1) * LANES)
        o_ref[:, sl] = (h[:, sl] * cos + hr[:, sl] * sin).astype(o_ref.dtype)


def _proj_rope(x2d, w, wr, cos, sin, seq, tm=512, tn=512):
    m = x2d.shape[0]
    nseq = seq // tm
    return pl.pallas_call(
        _proj_rope_kernel,
        grid=(m // tm, A_WIDTH // tn),
        in_specs=[
            pl.BlockSpec((tm, D_MODEL), lambda i, j: (i, 0)),
            pl.BlockSpec((D_MODEL, tn), lambda i, j: (0, j)),
            pl.BlockSpec((D_MODEL, tn), lambda i, j: (0, j)),
            pl.BlockSpec((tm, LANES), lambda i, j: (i % nseq, 0)),
            pl.BlockSpec((tm, LANES), lambda i, j: (i % nseq, 0)),
        ],
        out_specs=pl.BlockSpec((tm, tn), lambda i, j: (i, j)),
        out_shape=jax.ShapeDtypeStruct((m, A_WIDTH), jnp.bfloat16),
        compiler_params=_cparams(("parallel", "arbitrary")),
        name="proj_rope",
    )(x2d, w, wr, cos, sin)


def _proj_plain_kernel(x_ref, w_ref, o_ref):
    xb = x_ref[...].astype(jnp.bfloat16)
    o_ref[...] = jnp.dot(xb, w_ref[...], preferred_element_type=jnp.float32).astype(o_ref.dtype)


def _proj_plain(x2d, w, tm=512, tn=512):
    m = x2d.shape[0]
    return pl.pallas_call(
        _proj_plain_kernel,
        grid=(m // tm, B_WIDTH // tn),
        in_specs=[
            pl.BlockSpec((tm, D_MODEL), lambda i, j: (i, 0)),
            pl.BlockSpec((D_MODEL, tn), lambda i, j: (0, j)),
        ],
        out_specs=pl.BlockSpec((tm, tn), lambda i, j: (i, j)),
        out_shape=jax.ShapeDtypeStruct((m, B_WIDTH), jnp.bfloat16),
        compiler_params=_cparams(("parallel", "arbitrary")),
        name="proj_plain",
    )(x2d, w)


def _split3(v):
    hi = v.astype(jnp.bfloat16)
    r = v - hi.astype(jnp.float32)
    mid = r.astype(jnp.bfloat16)
    lo = (r - mid.astype(jnp.float32)).astype(jnp.bfloat16)
    return hi, mid, lo


def _proj_small_kernel(x_ref, w_ref, bf_ref, tri_ref, o_ref, c_ref, carry_ref, *, blocks_per_seq):
    i = pl.program_id(0)
    xb = x_ref[...].astype(jnp.bfloat16)
    h = jnp.dot(xb, w_ref[...], preferred_element_type=jnp.float32)
    lane = lax.broadcasted_iota(jnp.int32, h.shape, 1)
    ff = h + bf_ref[...]
    logf = jnp.minimum(ff, 0.0) - jnp.log(1.0 + jnp.exp(-jnp.abs(ff)))
    o = jnp.where(lane < F_LANE0, h * IDX_WEIGHT_SCALE, logf * LOG2E)
    o_ref[...] = o

    @pl.when(i % blocks_per_seq == 0)
    def _():
        carry_ref[...] = jnp.zeros_like(carry_ref)

    tri = tri_ref[...]
    hi, mid, lo = _split3(o)
    cs = (jnp.dot(tri, hi, preferred_element_type=jnp.float32)
          + jnp.dot(tri, mid, preferred_element_type=jnp.float32)
          + jnp.dot(tri, lo, preferred_element_type=jnp.float32))
    cs = cs + carry_ref[0:1, :]
    c_ref[...] = cs
    tm = cs.shape[0]
    carry_ref[...] = jnp.broadcast_to(cs[tm - 1:tm, :], carry_ref.shape)


def _proj_small(x2d, w, bf_row, seq, tm=512):
    m = x2d.shape[0]
    tri = jnp.asarray(np.tril(np.ones((tm, tm), np.float32)), jnp.bfloat16)
    kern = functools.partial(_proj_small_kernel, blocks_per_seq=seq // tm)
    return pl.pallas_call(
        kern,
        grid=(m // tm,),
        in_specs=[
            pl.BlockSpec((tm, D_MODEL), lambda i: (i, 0)),
            pl.BlockSpec((D_MODEL, LANES), lambda i: (0, 0)),
            pl.BlockSpec((1, LANES), lambda i: (0, 0)),
            pl.BlockSpec((tm, tm), lambda i: (0, 0)),
        ],
        out_specs=[pl.BlockSpec((tm, LANES), lambda i: (i, 0)),
                   pl.BlockSpec((tm, LANES), lambda i: (i, 0))],
        out_shape=[jax.ShapeDtypeStruct((m, LANES), jnp.float32),
                   jax.ShapeDtypeStruct((m, LANES), jnp.float32)],
        scratch_shapes=[pltpu.VMEM((8, LANES), jnp.float32)],
        compiler_params=_cparams(("arbitrary",)),
        name="proj_small",
    )(x2d, w, bf_row, tri)


def _flash_init(m_ref, acc_ref):
    m_ref[...] = jnp.full(m_ref.shape, -1e30, jnp.float32)
    acc_ref[...] = jnp.zeros(acc_ref.shape, jnp.float32)


def _flash_rows(s, v1, m_ref, acc_ref, rows):
    tk = s.shape[1]
    m_prev = m_ref[rows, :]
    m_next = jnp.maximum(m_prev, jnp.max(s, axis=1, keepdims=True))
    p = jnp.exp2((s - pltpu.repeat(m_next, tk // LANES, axis=1)).astype(jnp.bfloat16))
    alpha = jnp.exp2(m_prev - m_next)
    m_ref[rows, :] = m_next
    pv = jnp.dot(p, v1, preferred_element_type=jnp.float32)
    acc_ref[rows, :] = acc_ref[rows, :] * pltpu.repeat(alpha, v1.shape[1] // LANES, axis=1) + pv


def _pipelined_chunks(n_full, produce, consume, buf0, buf1):
    produce(0, buf0)

    def body(p, carry):
        produce(2 * p + 1, buf1)
        consume(2 * p, buf0, False)
        produce(2 * p + 2, buf0)
        consume(2 * p + 1, buf1, False)
        return carry

    lax.fori_loop(0, n_full // 2, body, 0)

    @pl.when(n_full % 2 == 1)
    def _():
        produce(n_full, buf1)
        consume(n_full - 1, buf0, False)
        consume(n_full, buf1, True)

    @pl.when(n_full % 2 == 0)
    def _():
        consume(n_full, buf0, True)


def _stack_pair(q, qs_ref, t):
    lane = lax.broadcasted_iota(jnp.int32, q.shape, 1)
    zero = jnp.zeros_like(q)
    qs_ref[0:t, :] = jnp.where(lane < HEAD_DIM, q, zero)
    qs_ref[t:2 * t, :] = jnp.where(lane >= HEAD_DIM, q, zero)


def _with_ones(v):
    return jnp.concatenate([v, jnp.ones((v.shape[0], LANES), v.dtype)], axis=1)


def _diff_kernel(lam_ref, g_ref, q_ref, k_ref, v_ref, o_ref, qs_ref, m_ref, acc_ref, s0_ref, s1_ref,
                 *, tq, tk):
    i = pl.program_id(2)
    _stack_pair(q_ref[0], qs_ref, tq)
    _flash_init(m_ref, acc_ref)
    groups = (slice(0, tq), slice(tq, 2 * tq))

    def produce(j, s_ref):
        start = pl.multiple_of(j * tk, tk)
        k = k_ref[0, pl.ds(start, tk), :]
        for rows in groups:
            s_ref[rows, :] = lax.dot_general(qs_ref[rows, :], k, _NT, preferred_element_type=jnp.float32)

    def consume(j, s_ref, last):
        start = pl.multiple_of(j * tk, tk)
        v1 = _with_ones(v_ref[0, pl.ds(start, tk), :])
        for rows in groups:
            s = s_ref[rows, :]
            if last:
                r = lax.broadcasted_iota(jnp.int32, (tq, tk), 0)
                c = lax.broadcasted_iota(jnp.int32, (tq, tk), 1)
                s = jnp.where(start + c <= i * tq + r, s, -jnp.inf)
            _flash_rows(s, v1, m_ref, acc_ref, rows)

    _pipelined_chunks((i * tq) // tk, produce, consume, s0_ref, s1_ref)

    lam_rows = lam_ref[...]
    s1 = jnp.sum(lam_rows[0:1, :] * lam_rows[1:2, :], axis=1, keepdims=True)
    s2 = jnp.sum(lam_rows[2:3, :] * lam_rows[3:4, :], axis=1, keepdims=True)
    lam_init = lam_rows[4:5, 0:1]
    lam = jnp.exp(s1) - jnp.exp(s2) + lam_init
    o0 = acc_ref[0:tq, 0:LANES] / acc_ref[0:tq, LANES:2 * LANES]
    o1 = acc_ref[tq:2 * tq, 0:LANES] / acc_ref[tq:2 * tq, LANES:2 * LANES]
    a = o0 - lam * o1
    ms = jnp.mean(a * a, axis=1, keepdims=True)
    a = a * lax.rsqrt(ms + SUBLN_EPS) * g_ref[...] * (1.0 - lam_init)
    o_ref[0] = a.astype(o_ref.dtype)


def _diff_attention(a3, b3, lam_rows, g_row, tq=256, tk=1024):
    bsz, seq, _ = a3.shape
    tk = min(tk, seq)
    kern = functools.partial(_diff_kernel, tq=tq, tk=tk)
    return pl.pallas_call(
        kern,
        grid=(bsz, DIFF_HEADS, seq // tq),
        in_specs=[
            pl.BlockSpec((8, LANES), lambda b, h, i: (0, 0)),
            pl.BlockSpec((1, LANES), lambda b, h, i: (0, 0)),
            pl.BlockSpec((1, tq, LANES), lambda b, h, i: (b, i, h)),
            pl.BlockSpec((1, seq, LANES), lambda b, h, i: (b, 0, 4 + h)),
            pl.BlockSpec((1, seq, LANES), lambda b, h, i: (b, 0, 8 + h)),
        ],
        out_specs=pl.BlockSpec((1, tq, LANES), lambda b, h, i: (b, i, h)),
        out_shape=jax.ShapeDtypeStruct((bsz, seq, DIFF_HEADS * LANES), jnp.bfloat16),
        scratch_shapes=[pltpu.VMEM((2 * tq, LANES), jnp.bfloat16),
                        pltpu.VMEM((2 * tq, LANES), jnp.float32),
                        pltpu.VMEM((2 * tq, 2 * LANES), jnp.float32),
                        pltpu.VMEM((2 * tq, tk), jnp.float32),
                        pltpu.VMEM((2 * tq, tk), jnp.float32)],
        compiler_params=_cparams(("parallel", "parallel", "arbitrary")),
        name="diff_attention",
    )(lam_rows, g_row, a3, a3, b3)


def _fox_kernel(q_ref, k_ref, v_ref, cq_ref, ck_ref, o_ref, qs_ref, cqs_ref, m_ref, acc_ref, s0_ref, s1_ref,
                *, tq, tk):
    g = pl.program_id(1)
    i = pl.program_id(2)
    _stack_pair(q_ref[0], qs_ref, tq)
    lane = lax.broadcasted_iota(jnp.int32, (tq, LANES), 1)
    ctile = cq_ref[0]
    for a in range(2):
        col = jnp.sum(jnp.where(lane == F_LANE0 + 2 * g + a, ctile, 0.0), axis=1, keepdims=True)
        cqs_ref[a * tq:(a + 1) * tq, :] = jnp.broadcast_to(col, (tq, LANES))
    _flash_init(m_ref, acc_ref)
    groups = (slice(0, tq), slice(tq, 2 * tq))

    def produce(j, s_ref):
        start = pl.multiple_of(j * tk, tk)
        k = k_ref[0, pl.ds(start, tk), :]
        for rows in groups:
            s_ref[rows, :] = lax.dot_general(qs_ref[rows, :], k, _NT, preferred_element_type=jnp.float32)

    def consume(j, s_ref, last):
        start = pl.multiple_of(j * tk, tk)
        v1 = _with_ones(v_ref[0, pl.ds(start, tk), :])
        ck = ck_ref[0, 0, :, pl.ds(start, tk)]
        for a, rows in enumerate(groups):
            bias = pltpu.repeat(cqs_ref[rows, :], tk // LANES, axis=1) - jnp.broadcast_to(ck[a:a + 1, :], (tq, tk))
            s = s_ref[rows, :] + bias
            if last:
                r = lax.broadcasted_iota(jnp.int32, (tq, tk), 0)
                c = lax.broadcasted_iota(jnp.int32, (tq, tk), 1)
                s = jnp.where(start + c <= i * tq + r, s, -jnp.inf)
            _flash_rows(s, v1, m_ref, acc_ref, rows)

    _pipelined_chunks((i * tq) // tk, produce, consume, s0_ref, s1_ref)

    o0 = acc_ref[0:tq, 0:LANES] / acc_ref[0:tq, LANES:2 * LANES]
    o1 = acc_ref[tq:2 * tq, 0:LANES] / acc_ref[tq:2 * tq, LANES:2 * LANES]
    o_ref[0] = jnp.where(lane < HEAD_DIM, o0, o1).astype(o_ref.dtype)


def _fox_attention(b3, csum3, ck4, tq=256, tk=1024):
    bsz, seq, _ = b3.shape
    tk = min(tk, seq)
    kern = functools.partial(_fox_kernel, tq=tq, tk=tk)
    return pl.pallas_call(
        kern,
        grid=(bsz, FOX_HEADS // 2, seq // tq),
        in_specs=[
            pl.BlockSpec((1, tq, LANES), lambda b, g, i: (b, i, 12 + g)),
            pl.BlockSpec((1, seq, LANES), lambda b, g, i: (b, 0, 14 + g)),
            pl.BlockSpec((1, seq, LANES), lambda b, g, i: (b, 0, 16 + g)),
            pl.BlockSpec((1, tq, LANES), lambda b, g, i: (b, i, 0)),
            pl.BlockSpec((1, 1, 8, seq), lambda b, g, i: (b, g, 0, 0)),
        ],
        out_specs=pl.BlockSpec((1, tq, LANES), lambda b, g, i: (b, i, g)),
        out_shape=jax.ShapeDtypeStruct((bsz, seq, FOX_HEADS * HEAD_DIM), jnp.bfloat16),
        scratch_shapes=[pltpu.VMEM((2 * tq, LANES), jnp.bfloat16),
                        pltpu.VMEM((2 * tq, LANES), jnp.float32),
                        pltpu.VMEM((2 * tq, LANES), jnp.float32),
                        pltpu.VMEM((2 * tq, 2 * LANES), jnp.float32),
                        pltpu.VMEM((2 * tq, tk), jnp.float32),
                        pltpu.VMEM((2 * tq, tk), jnp.float32)],
        compiler_params=_cparams(("parallel", "parallel", "arbitrary")),
        name="fox_attention",
    )(b3, b3, b3, csum3, ck4)


def _score_key(score):
    bits = pltpu.bitcast(score, jnp.int32)
    return bits ^ ((bits >> 31) & 0x7FFFFFFF)


def _dsa_kernel(iq_ref, sq_ref, w_ref, ik_ref, sk_ref, v_ref, o_ref,
                keys_ref, cand_ref, candk_ref, qi_ref, qd_ref, wb_ref, m_ref, acc_ref, s0_ref, s1_ref,
                a0_ref, a1_ref, thr_ref, cnt_ref, flag_ref, *, tq, tc, ta, topk, idx_bits, slots):
    i = pl.program_id(1)
    lane = lax.broadcasted_iota(jnp.int32, (tq, LANES), 1)
    low = lane < HEAD_DIM

    for h in range(IDX_HEADS):
        blk = iq_ref[0, :, (h // 2) * LANES:(h // 2 + 1) * LANES]
        qi_ref[h * tq:(h + 1) * tq, :] = jnp.where(low if h % 2 == 0 else ~low, blk, jnp.zeros_like(blk))
    dsa_order = (0, 2, 1, 3)
    for slab, h in enumerate(dsa_order):
        blk = sq_ref[0, :, (h // 2) * LANES:(h // 2 + 1) * LANES]
        qd_ref[slab * tq:(slab + 1) * tq, :] = jnp.where(low if h % 2 == 0 else ~low, blk, jnp.zeros_like(blk))
    wt = w_ref[0]
    for h in range(IDX_HEADS):
        col = jnp.sum(jnp.where(lane == W_LANE0 + h, wt, 0.0), axis=1, keepdims=True)
        wb_ref[h] = jnp.broadcast_to(col, (tq, LANES))

    nlast = (i * tq) // tc
    nch = nlast + 1
    cand_ref[...] = jnp.full(cand_ref.shape, -jnp.inf, jnp.float32)
    row_pos = i * tq + lax.broadcasted_iota(jnp.int32, (tq, tc), 0)
    col_iota = lax.broadcasted_iota(jnp.int32, (tq, tc), 1)
    idx_groups = (slice(0, 4 * tq), slice(4 * tq, 8 * tq))
    dsa_groups = (slice(0, 2 * tq), slice(2 * tq, 4 * tq))

    def rep(x):
        return pltpu.repeat(x, tc // LANES, axis=1)

    def idx_produce(j, s_ref):
        start = pl.multiple_of(j * tc, tc)
        ik = ik_ref[0, pl.ds(start, tc), :]
        for rows in idx_groups:
            s_ref[rows, :] = lax.dot_general(qi_ref[rows, :], ik, _NT, preferred_element_type=jnp.float32)

    def idx_consume(j, s_ref, last):
        start = pl.multiple_of(j * tc, tc)
        score = jnp.zeros((tq, tc), jnp.float32)
        for h in range(IDX_HEADS):
            score = score + jnp.maximum(s_ref[h * tq:(h + 1) * tq, :], 0.0) * rep(wb_ref[h])
        if last:
            score = jnp.where(start + col_iota <= row_pos, score, -jnp.inf)
        keys_ref[:, pl.ds(start, tc)] = _score_key(score)
        for rg in range(tq // 8):
            rs = slice(rg * 8, (rg + 1) * 8)
            xs = [score[rs, u * LANES:(u + 1) * LANES] for u in range(tc // LANES)]
            for sl in range(slots):
                cs = slice(sl * LANES, (sl + 1) * LANES)
                a = cand_ref[rs, cs]
                for u in range(len(xs)):
                    hi = jnp.maximum(a, xs[u])
                    xs[u] = jnp.minimum(a, xs[u])
                    a = hi
                cand_ref[rs, cs] = a

    _pipelined_chunks(nlast, idx_produce, idx_consume, s0_ref, s1_ref)

    def count_in(ref, n_chunks, pred):
        def body(c, acc):
            start = c * tc if isinstance(c, int) else pl.multiple_of(c * tc, tc)
            hit = pred(ref[:, pl.ds(start, tc)], start).astype(jnp.int32)
            for u in range(tc // LANES):
                acc = acc + hit[:, u * LANES:(u + 1) * LANES]
            return acc
        part = jnp.zeros((tq, LANES), jnp.int32)
        if isinstance(n_chunks, int):
            for c in range(n_chunks):
                part = body(c, part)
        else:
            part = lax.fori_loop(0, n_chunks, body, part)
        return jnp.broadcast_to(jnp.sum(part, axis=1, keepdims=True), (tq, LANES))

    def count(pred):
        return count_in(keys_ref, nch, pred)

    n_row = i * tq + lax.broadcasted_iota(jnp.int32, (tq, LANES), 0) + 1

    def search(ref, n_chunks):
        def pending(state):
            bi, thr, cnt_thr = state
            return jnp.logical_and(bi < 32, jnp.max(jnp.where(n_row > topk, cnt_thr, topk)) > topk)

        def bit_step(state):
            bi, thr, cnt_thr = state
            cand = thr + lax.shift_left(jnp.int32(1), 31 - bi)
            cnt = count_in(ref, n_chunks, lambda kc, start: kc >= rep(cand))
            ok = cnt >= topk
            return bi + 1, jnp.where(ok, cand, thr), jnp.where(ok, cnt, cnt_thr)

        total = jnp.zeros((tq, LANES), jnp.int32) + n_chunks * tc
        init = (jnp.int32(0), jnp.full((tq, LANES), INT_MIN, jnp.int32), total)
        if isinstance(n_chunks, int):
            _, thr, cnt_thr = lax.fori_loop(0, 32, lambda _, st: bit_step(st), init, unroll=4)
        else:
            _, thr, cnt_thr = lax.while_loop(pending, bit_step, init)
        thr_ref[...] = thr
        cnt_ref[...] = cnt_thr

    cand_chunks = slots * LANES // tc
    flag_ref[0] = 1

    @pl.when(nch > cand_chunks)
    def _():
        candk_ref[...] = _score_key(cand_ref[...])
        search(candk_ref, cand_chunks)
        kept_min = jnp.max(candk_ref[:, (slots - 1) * LANES:slots * LANES], axis=1, keepdims=True)
        unsafe = jnp.where(kept_min >= thr_ref[...], 1, 0)
        flag_ref[0] = jnp.max(unsafe)

    @pl.when(flag_ref[0] > 0)
    def _():
        search(keys_ref, nch)

    thr = thr_ref[...]
    cnt_thr = cnt_ref[...]

    excess = jnp.where(thr > KEY_NEG_INF, cnt_thr - topk, 0)

    @pl.when(jnp.max(excess) > 0)
    def _():
        c_gt = count(lambda kc, start: kc > rep(thr))
        keep = topk - c_gt

        def idx_step(bi, x):
            cand = x + lax.shift_left(jnp.int32(1), idx_bits - 1 - bi)
            below = count(lambda kc, start: (kc == rep(thr)) & (start + col_iota < rep(cand)))
            return jnp.where(below < keep, cand, x)

        last = lax.fori_loop(0, idx_bits, idx_step, jnp.zeros((tq, LANES), jnp.int32))

        def demote(c, carry):
            start = pl.multiple_of(c * tc, tc)
            kc = keys_ref[:, pl.ds(start, tc)]
            drop = (rep(excess) > 0) & (kc == rep(thr)) & (start + col_iota > rep(last))
            keys_ref[:, pl.ds(start, tc)] = jnp.where(drop, kc - 1, kc)
            return carry

        lax.fori_loop(0, nch, demote, 0)

    sel_thr = jnp.maximum(thr, KEY_NEG_INF + 1)

    _flash_init(m_ref, acc_ref)
    vlow = lax.broadcasted_iota(jnp.int32, (ta, LANES), 1) < HEAD_DIM
    alast = (i * tq) // ta

    @pl.when((alast + 1) * ta > nch * tc)
    def _():
        keys_ref[:, pl.ds(pl.multiple_of(nch * tc, tc), tc)] = jnp.full((tq, tc), KEY_NEG_INF, jnp.int32)

    def att_produce(j, s_ref):
        start = pl.multiple_of(j * ta, ta)
        sk = sk_ref[0, pl.ds(start, ta), :]
        for rows in dsa_groups:
            s_ref[rows, :] = lax.dot_general(qd_ref[rows, :], sk, _NT, preferred_element_type=jnp.float32)

    def att_consume(j, s_ref, last):
        start = pl.multiple_of(j * ta, ta)
        v2 = v_ref[0, pl.ds(start, ta), :]
        one = jnp.ones_like(v2)
        v_even = jnp.where(vlow, v2, one)
        v_odd = jnp.where(vlow, one, v2)
        sel = keys_ref[:, pl.ds(start, ta)] >= pltpu.repeat(sel_thr, ta // LANES, axis=1)
        for rows, v1 in zip(dsa_groups, (v_even, v_odd)):
            s = s_ref[rows, :].reshape(2, tq, ta)
            s = jnp.where(sel[None], s, -jnp.inf).reshape(2 * tq, ta)
            _flash_rows(s, v1, m_ref, acc_ref, rows)

    _pipelined_chunks(alast, att_produce, att_consume, a0_ref, a1_ref)

    outs = {}
    for slab, h in enumerate(dsa_order):
        acc = acc_ref[slab * tq:(slab + 1) * tq, :]
        outs[h] = acc / pltpu.roll(acc, HEAD_DIM, axis=1)
    for p in range(DSA_HEADS // 2):
        o_ref[0, :, p * LANES:(p + 1) * LANES] = jnp.where(low, outs[2 * p], outs[2 * p + 1]).astype(o_ref.dtype)


def _dsa_attention(a3, b3, w3, tq=128, tc=512, slots=12):
    bsz, seq, _ = a3.shape
    ta = 2 * tc
    topk = min(DSA_TOPK_MAX, seq // 4)
    idx_bits = int(math.log2(seq))
    assert 2 ** idx_bits == seq and tc % tq == 0 and (slots * LANES) % tc == 0 and seq % ta == 0
    kern = functools.partial(_dsa_kernel, tq=tq, tc=tc, ta=ta, topk=topk, idx_bits=idx_bits, slots=slots)
    return pl.pallas_call(
        kern,
        grid=(bsz, seq // tq),
        in_specs=[
            pl.BlockSpec((1, tq, 4 * LANES), lambda b, i: (b, i, 2)),
            pl.BlockSpec((1, tq, 2 * LANES), lambda b, i: (b, i, 6)),
            pl.BlockSpec((1, tq, LANES), lambda b, i: (b, i, 0)),
            pl.BlockSpec((1, seq, LANES), lambda b, i: (b, 0, 14)),
            pl.BlockSpec((1, seq, LANES), lambda b, i: (b, 0, 15)),
            pl.BlockSpec((1, seq, LANES), lambda b, i: (b, 0, 18)),
        ],
        out_specs=pl.BlockSpec((1, tq, 2 * LANES), lambda b, i: (b, i, 0)),
        out_shape=jax.ShapeDtypeStruct((bsz, seq, DSA_HEADS * HEAD_DIM), jnp.bfloat16),
        scratch_shapes=[pltpu.VMEM((tq, seq), jnp.int32),
                        pltpu.VMEM((tq, slots * LANES), jnp.float32),
                        pltpu.VMEM((tq, slots * LANES), jnp.int32),
                        pltpu.VMEM((IDX_HEADS * tq, LANES), jnp.bfloat16),
                        pltpu.VMEM((DSA_HEADS * tq, LANES), jnp.bfloat16),
                        pltpu.VMEM((IDX_HEADS, tq, LANES), jnp.float32),
                        pltpu.VMEM((DSA_HEADS * tq, LANES), jnp.float32),
                        pltpu.VMEM((DSA_HEADS * tq, LANES), jnp.float32),
                        pltpu.VMEM((IDX_HEADS * tq, tc), jnp.float32),
                        pltpu.VMEM((IDX_HEADS * tq, tc), jnp.float32),
                        pltpu.VMEM((DSA_HEADS * tq, ta), jnp.float32),
                        pltpu.VMEM((DSA_HEADS * tq, ta), jnp.float32),
                        pltpu.VMEM((tq, LANES), jnp.int32),
                        pltpu.VMEM((tq, LANES), jnp.int32),
                        pltpu.SMEM((1,), jnp.int32)],
        compiler_params=_cparams(("parallel", "arbitrary")),
        name="dsa_attention",
    )(a3, a3, w3, a3, a3, b3)


def _out_kernel(x_ref, gate_ref, a_ref, bo_ref, co_ref, wa_ref, wb_ref, wc_ref, g_ref, b_ref, o_ref):
    gate = gate_ref[...].astype(jnp.float32)
    silu = gate / (1.0 + jnp.exp(-gate))

    def gated(ref, lo, hi):
        return (ref[...].astype(jnp.float32) * silu[:, lo:hi]).astype(jnp.bfloat16)

    out = jnp.dot(gated(a_ref, 0, 512), wa_ref[...], preferred_element_type=jnp.float32)
    out = out + jnp.dot(gated(bo_ref, 512, 768), wb_ref[...], preferred_element_type=jnp.float32)
    out = out + jnp.dot(gated(co_ref, 768, 1024), wc_ref[...], preferred_element_type=jnp.float32)
    y = DEEPNORM_ALPHA * x_ref[...] + out
    mu = jnp.mean(y, axis=1, keepdims=True)
    d = y - mu
    var = jnp.mean(d * d, axis=1, keepdims=True)
    o_ref[...] = d * lax.rsqrt(var + LN_EPS) * g_ref[...] + b_ref[...]


def _out_proj(x2d, b2d, a2d, bo2d, co2d, w_out, ln_g, ln_b, tm=256):
    m = x2d.shape[0]
    row = lambda i: (i, 0)
    fixed = lambda i: (0, 0)
    return pl.pallas_call(
        _out_kernel,
        grid=(m // tm,),
        in_specs=[
            pl.BlockSpec((tm, D_MODEL), row),
            pl.BlockSpec((tm, D_MODEL), row),
            pl.BlockSpec((tm, 512), row),
            pl.BlockSpec((tm, 256), row),
            pl.BlockSpec((tm, 256), row),
            pl.BlockSpec((512, D_MODEL), fixed),
            pl.BlockSpec((256, D_MODEL), lambda i: (2, 0)),
            pl.BlockSpec((256, D_MODEL), lambda i: (3, 0)),
            pl.BlockSpec((1, D_MODEL), fixed),
            pl.BlockSpec((1, D_MODEL), fixed),
        ],
        out_specs=pl.BlockSpec((tm, D_MODEL), row),
        out_shape=jax.ShapeDtypeStruct((m, D_MODEL), jnp.float32),
        compiler_params=_cparams(("parallel",)),
        name="out_proj",
    )(x2d, b2d, a2d, bo2d, co2d, w_out, w_out, w_out, ln_g, ln_b)


def _rope_tables(seq):
    inv = ROPE_THETA ** (-jnp.arange(0, HEAD_DIM, 2, dtype=jnp.float32) / HEAD_DIM)
    ang = jnp.arange(seq, dtype=jnp.int32).astype(jnp.float32)[:, None] * inv[None, :]
    reps = LANES // (HEAD_DIM // 2)
    return jnp.tile(jnp.cos(ang), (1, reps)), jnp.tile(jnp.sin(ang), (1, reps))


def kernel(x, w_in, b_f, lam_q1, lam_k1, lam_q2, lam_k2, g_subln, w_out, ln_g, ln_b):
    bsz, seq, _ = x.shape
    m = bsz * seq
    cos, sin = _rope_tables(seq)

    w_a = w_in[:, :, A_IDX] * A_SCALE
    w_ar = (w_a[:, :, A_ROT_PARTNER] * A_ROT_SIGN).astype(jnp.bfloat16)
    w_a = w_a.astype(jnp.bfloat16)
    w_b = (w_in[:, :, B_IDX] * B_SCALE).astype(jnp.bfloat16)
    w_c = (w_in[:, :, C_IDX] * C_SCALE).astype(jnp.bfloat16)
    w_o = w_out.astype(jnp.bfloat16)
    bf_rows = jnp.zeros((DEPTH, 1, LANES), jnp.float32).at[:, 0, F_LANE0:F_LANE0 + FOX_HEADS].set(b_f)

    x2d = x.reshape(m, D_MODEL)
    for l in range(DEPTH):
        lam_init = 0.8 - 0.6 * math.exp(-0.3 * l)
        lam_rows = jnp.zeros((8, LANES), jnp.float32)
        lam_rows = lam_rows.at[0, :HEAD_DIM].set(lam_q1[l]).at[1, :HEAD_DIM].set(lam_k1[l])
        lam_rows = lam_rows.at[2, :HEAD_DIM].set(lam_q2[l]).at[3, :HEAD_DIM].set(lam_k2[l])
        lam_rows = lam_rows.at[4, :].set(lam_init)

        a2d = _proj_rope(x2d, w_a[l], w_ar[l], cos, sin, seq)
        b2d = _proj_plain(x2d, w_b[l])
        wf2d, csum2d = _proj_small(x2d, w_c[l], bf_rows[l], seq)

        a3 = a2d.reshape(bsz, seq, A_WIDTH)
        b3 = b2d.reshape(bsz, seq, B_WIDTH)
        wf3 = wf2d.reshape(bsz, seq, LANES)
        csum3 = csum2d.reshape(bsz, seq, LANES)
        ck = jnp.transpose(csum3[:, :, F_LANE0:F_LANE0 + FOX_HEADS], (0, 2, 1)).reshape(bsz, 2, 2, seq)
        ck4 = jnp.pad(ck, ((0, 0), (0, 0), (0, 6), (0, 0)))

        diff_o = _diff_attention(a3, b3, lam_rows, g_subln[l].reshape(1, LANES))
        fox_o = _fox_attention(b3, csum3, ck4)
        dsa_o = _dsa_attention(a3, b3, wf3)

        x2d = _out_proj(x2d, b2d, diff_o.reshape(m, 512), dsa_o.reshape(m, 256), fox_o.reshape(m, 256),
                        w_o[l], ln_g[l].reshape(1, D_MODEL), ln_b[l].reshape(1, D_MODEL))
    return x2d.reshape(bsz, seq, D_MODEL)
```

```python
import functools
import math

import numpy as np
import jax
import jax.numpy as jnp
from jax import lax
from jax.experimental import pallas as pl
from jax.experimental.pallas import tpu as pltpu

D_MODEL = 1024
DEPTH = 4
HEAD_DIM = 64
DIFF_HEADS = 4
DSA_HEADS = 4
IDX_HEADS = 8
FOX_HEADS = 4
DSA_TOPK_MAX = 256
ROPE_THETA = 10000.0
LN_EPS = 1e-5
SUBLN_EPS = 1e-5
DEEPNORM_ALPHA = (2 * DEPTH) ** 0.25
IDX_WEIGHT_SCALE = (IDX_HEADS * HEAD_DIM) ** -0.5
LOG2E = math.log2(math.e)
Q_SCALE = HEAD_DIM ** -0.5 * LOG2E

LANES = 128
VMEM_LIMIT_BYTES = 56 * 1024 * 1024

_OFF = {}
_o = 0
for _name, _n in (('diff_q', 512), ('diff_k', 512), ('diff_v', 512), ('dsa_q', 256), ('dsa_k', 64),
                  ('dsa_v', 64), ('idx_q', 512), ('idx_k', 64), ('idx_w', 8), ('fox_q', 256),
                  ('fox_k', 256), ('fox_v', 256), ('fox_f', 4), ('gate', 1024)):
    _OFF[_name] = (_o, _n)
    _o += _n
IN_WIDTH = _o


def _cols(name, scale=1.0, repeat=1):
    o, n = _OFF[name]
    idx = np.tile(np.arange(o, o + n), repeat)
    return idx, np.full(idx.shape, scale, np.float32)


def _pad(n):
    return np.zeros((n,), np.int64), np.zeros((n,), np.float32)


def _layout(parts):
    idx = np.concatenate([p[0] for p in parts])
    scale = np.concatenate([p[1] for p in parts])
    return idx, scale


A_IDX, A_SCALE = _layout([_cols('diff_q', Q_SCALE), _cols('diff_k'), _cols('idx_q'),
                          _cols('dsa_q', Q_SCALE), _cols('idx_k', repeat=2), _cols('dsa_k', repeat=2)])
A_WIDTH = A_IDX.shape[0]
B_IDX, B_SCALE = _layout([_cols('gate'), _cols('diff_v'), _cols('fox_q', Q_SCALE), _cols('fox_k'),
                          _cols('fox_v'), _cols('dsa_v', repeat=2), _pad(128)])
B_WIDTH = B_IDX.shape[0]
C_IDX, C_SCALE = _layout([_cols('idx_w'), _cols('fox_f'), _pad(LANES - 12)])
W_LANE0 = 0
F_LANE0 = 8

_c = np.arange(A_WIDTH)
_first_half = (_c % HEAD_DIM) < HEAD_DIM // 2
A_ROT_PARTNER = np.where(_first_half, _c + HEAD_DIM // 2, _c - HEAD_DIM // 2)
A_ROT_SIGN = np.where(_first_half, -1.0, 1.0).astype(np.float32)

_NT = (((1,), (1,)), ((), ()))
INT_MIN = -2 ** 31
KEY_NEG_INF = -2139095041


def _cparams(sem):
    return pltpu.CompilerParams(dimension_semantics=sem, vmem_limit_bytes=VMEM_LIMIT_BYTES)


def _proj_rope_kernel(x_ref, w_ref, wr_ref, cos_ref, sin_ref, o_ref):
    xb = x_ref[...].astype(jnp.bfloat16)
    h = jnp.dot(xb, w_ref[...], preferred_element_type=jnp.float32)
    hr = jnp.dot(xb, wr_ref[...], preferred_element_type=jnp.float32)
    cos = cos_ref[...]
    sin = sin_ref[...]
    for c in range(o_ref.shape[1] // LANES):
        sl = slice(c * LANES, (c + 1) * LANES)
        o_ref[:, sl] = (h[:, sl] * cos + hr[:, sl] * sin).astype(o_ref.dtype)


def _proj_rope(x2d, w, wr, cos, sin, seq, tm=512, tn=512):
    m = x2d.shape[0]
    nseq = seq // tm
    return pl.pallas_call(
        _proj_rope_kernel,
        grid=(m // tm, A_WIDTH // tn),
        in_specs=[
            pl.BlockSpec((tm, D_MODEL), lambda i, j: (i, 0)),
            pl.BlockSpec((D_MODEL, tn), lambda i, j: (0, j)),
            pl.BlockSpec((D_MODEL, tn), lambda i, j: (0, j)),
            pl.BlockSpec((tm, LANES), lambda i, j: (i % nseq, 0)),
            pl.BlockSpec((tm, LANES), lambda i, j: (i % nseq, 0)),
        ],
        out_specs=pl.BlockSpec((tm, tn), lambda i, j: (i, j)),
        out_shape=jax.ShapeDtypeStruct((m, A_WIDTH), jnp.bfloat16),
        compiler_params=_cparams(("parallel", "arbitrary")),
        name="proj_rope",
    )(x2d, w, wr, cos, sin)


def _proj_plain_kernel(x_ref, w_ref, o_ref):
    xb = x_ref[...].astype(jnp.bfloat16)
    o_ref[...] = jnp.dot(xb, w_ref[...], preferred_element_type=jnp.float32).astype(o_ref.dtype)


def _proj_plain(x2d, w, tm=512, tn=512):
    m = x2d.shape[0]
    return pl.pallas_call(
        _proj_plain_kernel,
        grid=(m // tm, B_WIDTH // tn),
        in_specs=[
            pl.BlockSpec((tm, D_MODEL), lambda i, j: (i, 0)),
            pl.BlockSpec((D_MODEL, tn), lambda i, j: (0, j)),
        ],
        out_specs=pl.BlockSpec((tm, tn), lambda i, j: (i, j)),
        out_shape=jax.ShapeDtypeStruct((m, B_WIDTH), jnp.bfloat16),
        compiler_params=_cparams(("parallel", "arbitrary")),
        name="proj_plain",
    )(x2d, w)


def _split3(v):
    hi = v.astype(jnp.bfloat16)
    r = v - hi.astype(jnp.float32)
    mid = r.astype(jnp.bfloat16)
    lo = (r - mid.astype(jnp.float32)).astype(jnp.bfloat16)
    return hi, mid, lo


def _gate_aug_constants():
    place = np.zeros((6, LANES, LANES), np.float32)
    ones = np.zeros((8, LANES), np.float32)
    for h in range(FOX_HEADS):
        for j in range(3):
            place[j, F_LANE0 + h, 8 * h + j] = 1.0
            place[3 + j, F_LANE0 + h, 8 * h + 3 + j] = -1.0
            ones[0, 8 * h + 3 + j] = 1.0
            ones[1, 8 * h + j] = 1.0
    return jnp.asarray(place, jnp.bfloat16), jnp.asarray(ones)


def _proj_small_kernel(x_ref, w_ref, bf_ref, tri_ref, place_ref, ones_ref, o_ref, aq_ref, ak_ref, carry_ref,
                       *, blocks_per_seq):
    i = pl.program_id(0)
    xb = x_ref[...].astype(jnp.bfloat16)
    h = jnp.dot(xb, w_ref[...], preferred_element_type=jnp.float32)
    lane = lax.broadcasted_iota(jnp.int32, h.shape, 1)
    ff = h + bf_ref[...]
    logf = jnp.minimum(ff, 0.0) - jnp.log(1.0 + jnp.exp(-jnp.abs(ff)))
    o = jnp.where(lane < F_LANE0, h * IDX_WEIGHT_SCALE, logf * LOG2E)
    o_ref[...] = o

    @pl.when(i % blocks_per_seq == 0)
    def _():
        carry_ref[...] = jnp.zeros_like(carry_ref)

    tri = tri_ref[...]
    hi, mid, lo = _split3(o)
    cs = (jnp.dot(tri, hi, preferred_element_type=jnp.float32)
          + jnp.dot(tri, mid, preferred_element_type=jnp.float32)
          + jnp.dot(tri, lo, preferred_element_type=jnp.float32))
    cs = cs + carry_ref[0:1, :]
    tm = cs.shape[0]
    carry_ref[...] = jnp.broadcast_to(cs[tm - 1:tm, :], carry_ref.shape)
    parts = _split3(cs)
    aq = ones_ref[0:1, :]
    ak = ones_ref[1:2, :]
    for j in range(3):
        aq = aq + jnp.dot(parts[j], place_ref[j], preferred_element_type=jnp.float32)
        ak = ak + jnp.dot(parts[j], place_ref[3 + j], preferred_element_type=jnp.float32)
    aq_ref[...] = aq.astype(aq_ref.dtype)
    ak_ref[...] = ak.astype(ak_ref.dtype)


def _proj_small(x2d, w, bf_row, seq, tm=512):
    m = x2d.shape[0]
    tri = jnp.asarray(np.tril(np.ones((tm, tm), np.float32)), jnp.bfloat16)
    place, ones = _gate_aug_constants()
    kern = functools.partial(_proj_small_kernel, blocks_per_seq=seq // tm)
    row = lambda i: (i, 0)
    return pl.pallas_call(
        kern,
        grid=(m // tm,),
        in_specs=[
            pl.BlockSpec((tm, D_MODEL), row),
            pl.BlockSpec((D_MODEL, LANES), lambda i: (0, 0)),
            pl.BlockSpec((1, LANES), lambda i: (0, 0)),
            pl.BlockSpec((tm, tm), lambda i: (0, 0)),
            pl.BlockSpec((6, LANES, LANES), lambda i: (0, 0, 0)),
            pl.BlockSpec((8, LANES), lambda i: (0, 0)),
        ],
        out_specs=[pl.BlockSpec((tm, LANES), row), pl.BlockSpec((tm, LANES), row), pl.BlockSpec((tm, LANES), row)],
        out_shape=[jax.ShapeDtypeStruct((m, LANES), jnp.float32),
                   jax.ShapeDtypeStruct((m, LANES), jnp.bfloat16),
                   jax.ShapeDtypeStruct((m, LANES), jnp.bfloat16)],
        scratch_shapes=[pltpu.VMEM((8, LANES), jnp.float32)],
        compiler_params=_cparams(("arbitrary",)),
        name="proj_small",
    )(x2d, w, bf_row, tri, place, ones)


def _flash_init(m_ref, acc_ref):
    m_ref[...] = jnp.full(m_ref.shape, -1e30, jnp.float32)
    acc_ref[...] = jnp.zeros(acc_ref.shape, jnp.float32)


def _flash_rows(s, v1, m_ref, acc_ref, rows):
    tk = s.shape[1]
    m_prev = m_ref[rows, :]
    m_next = jnp.maximum(m_prev, jnp.max(s, axis=1, keepdims=True))
    p = jnp.exp2((s - pltpu.repeat(m_next, tk // LANES, axis=1)).astype(jnp.bfloat16))
    alpha = jnp.exp2(m_prev - m_next)
    m_ref[rows, :] = m_next
    pv = jnp.dot(p, v1, preferred_element_type=jnp.float32)
    acc_ref[rows, :] = acc_ref[rows, :] * pltpu.repeat(alpha, v1.shape[1] // LANES, axis=1) + pv


def _pipelined_chunks(n_full, produce, consume, buf0, buf1):
    produce(0, buf0)

    def body(p, carry):
        produce(2 * p + 1, buf1)
        consume(2 * p, buf0, False)
        produce(2 * p + 2, buf0)
        consume(2 * p + 1, buf1, False)
        return carry

    lax.fori_loop(0, n_full // 2, body, 0)

    @pl.when(n_full % 2 == 1)
    def _():
        produce(n_full, buf1)
        consume(n_full - 1, buf0, False)
        consume(n_full, buf1, True)

    @pl.when(n_full % 2 == 0)
    def _():
        consume(n_full, buf0, True)


def _stack_pair(q, qs_ref, t):
    lane = lax.broadcasted_iota(jnp.int32, q.shape, 1)
    zero = jnp.zeros_like(q)
    qs_ref[0:t, :] = jnp.where(lane < HEAD_DIM, q, zero)
    qs_ref[t:2 * t, :] = jnp.where(lane >= HEAD_DIM, q, zero)


def _with_ones(v):
    return jnp.concatenate([v, jnp.ones((v.shape[0], LANES), v.dtype)], axis=1)


def _diff_kernel(lam_ref, g_ref, q_ref, k_ref, v_ref, o_ref, qs_ref, m_ref, acc_ref, s0_ref, s1_ref,
                 *, tq, tk):
    i = pl.program_id(2)
    _stack_pair(q_ref[0], qs_ref, tq)
    _flash_init(m_ref, acc_ref)
    groups = (slice(0, tq), slice(tq, 2 * tq))

    def produce(j, s_ref):
        start = pl.multiple_of(j * tk, tk)
        k = k_ref[0, pl.ds(start, tk), :]
        for rows in groups:
            s_ref[rows, :] = lax.dot_general(qs_ref[rows, :], k, _NT, preferred_element_type=jnp.float32)

    def consume(j, s_ref, last):
        start = pl.multiple_of(j * tk, tk)
        v1 = _with_ones(v_ref[0, pl.ds(start, tk), :])
        for rows in groups:
            s = s_ref[rows, :]
            if last:
                r = lax.broadcasted_iota(jnp.int32, (tq, tk), 0)
                c = lax.broadcasted_iota(jnp.int32, (tq, tk), 1)
                s = jnp.where(start + c <= i * tq + r, s, -jnp.inf)
            _flash_rows(s, v1, m_ref, acc_ref, rows)

    _pipelined_chunks((i * tq) // tk, produce, consume, s0_ref, s1_ref)

    lam_rows = lam_ref[...]
    s1 = jnp.sum(lam_rows[0:1, :] * lam_rows[1:2, :], axis=1, keepdims=True)
    s2 = jnp.sum(lam_rows[2:3, :] * lam_rows[3:4, :], axis=1, keepdims=True)
    lam_init = lam_rows[4:5, 0:1]
    lam = jnp.exp(s1) - jnp.exp(s2) + lam_init
    o0 = acc_ref[0:tq, 0:LANES] / acc_ref[0:tq, LANES:2 * LANES]
    o1 = acc_ref[tq:2 * tq, 0:LANES] / acc_ref[tq:2 * tq, LANES:2 * LANES]
    a = o0 - lam * o1
    ms = jnp.mean(a * a, axis=1, keepdims=True)
    a = a * lax.rsqrt(ms + SUBLN_EPS) * g_ref[...] * (1.0 - lam_init)
    o_ref[0] = a.astype(o_ref.dtype)


def _diff_attention(a3, b3, lam_rows, g_row, tq=256, tk=1024):
    bsz, seq, _ = a3.shape
    tk = min(tk, seq)
    kern = functools.partial(_diff_kernel, tq=tq, tk=tk)
    return pl.pallas_call(
        kern,
        grid=(bsz, DIFF_HEADS, seq // tq),
        in_specs=[
            pl.BlockSpec((8, LANES), lambda b, h, i: (0, 0)),
            pl.BlockSpec((1, LANES), lambda b, h, i: (0, 0)),
            pl.BlockSpec((1, tq, LANES), lambda b, h, i: (b, i, h)),
            pl.BlockSpec((1, seq, LANES), lambda b, h, i: (b, 0, 4 + h)),
            pl.BlockSpec((1, seq, LANES), lambda b, h, i: (b, 0, 8 + h)),
        ],
        out_specs=pl.BlockSpec((1, tq, LANES), lambda b, h, i: (b, i, h)),
        out_shape=jax.ShapeDtypeStruct((bsz, seq, DIFF_HEADS * LANES), jnp.bfloat16),
        scratch_shapes=[pltpu.VMEM((2 * tq, LANES), jnp.bfloat16),
                        pltpu.VMEM((2 * tq, LANES), jnp.float32),
                        pltpu.VMEM((2 * tq, 2 * LANES), jnp.float32),
                        pltpu.VMEM((2 * tq, tk), jnp.float32),
                        pltpu.VMEM((2 * tq, tk), jnp.float32)],
        compiler_params=_cparams(("parallel", "parallel", "arbitrary")),
        name="diff_attention",
    )(lam_rows, g_row, a3, a3, b3)


def _fox_kernel(q_ref, k_ref, v_ref, aq_ref, ak_ref, o_ref, qs_ref, m_ref, acc_ref, s0_ref, s1_ref, *, tq, tk):
    g = pl.program_id(1)
    i = pl.program_id(2)
    q = q_ref[0]
    aq = aq_ref[0]
    lane = lax.broadcasted_iota(jnp.int32, (tq, LANES), 1)
    zero = jnp.zeros_like(q)
    for a in range(2):
        rows = slice(a * tq, (a + 1) * tq)
        head_lanes = (lane < HEAD_DIM) if a == 0 else (lane >= HEAD_DIM)
        first = 8 * (2 * g + a)
        qs_ref[rows, 0:LANES] = jnp.where(head_lanes, q, zero)
        qs_ref[rows, LANES:2 * LANES] = jnp.where((lane >= first) & (lane < first + 6), aq, zero)
    _flash_init(m_ref, acc_ref)
    groups = (slice(0, tq), slice(tq, 2 * tq))

    def produce(j, s_ref):
        start = pl.multiple_of(j * tk, tk)
        k2 = jnp.concatenate([k_ref[0, pl.ds(start, tk), :], ak_ref[0, pl.ds(start, tk), :]], axis=1)
        for rows in groups:
            s_ref[rows, :] = lax.dot_general(qs_ref[rows, :], k2, _NT, preferred_element_type=jnp.float32)

    def consume(j, s_ref, last):
        start = pl.multiple_of(j * tk, tk)
        v1 = _with_ones(v_ref[0, pl.ds(start, tk), :])
        for rows in groups:
            s = s_ref[rows, :]
            if last:
                r = lax.broadcasted_iota(jnp.int32, (tq, tk), 0)
                c = lax.broadcasted_iota(jnp.int32, (tq, tk), 1)
                s = jnp.where(start + c <= i * tq + r, s, -jnp.inf)
            _flash_rows(s, v1, m_ref, acc_ref, rows)

    _pipelined_chunks((i * tq) // tk, produce, consume, s0_ref, s1_ref)

    o0 = acc_ref[0:tq, 0:LANES] / acc_ref[0:tq, LANES:2 * LANES]
    o1 = acc_ref[tq:2 * tq, 0:LANES] / acc_ref[tq:2 * tq, LANES:2 * LANES]
    o_ref[0] = jnp.where(lane < HEAD_DIM, o0, o1).astype(o_ref.dtype)


def _fox_attention(b3, aq3, ak3, tq=256, tk=1024):
    bsz, seq, _ = b3.shape
    tk = min(tk, seq)
    kern = functools.partial(_fox_kernel, tq=tq, tk=tk)
    return pl.pallas_call(
        kern,
        grid=(bsz, FOX_HEADS // 2, seq // tq),
        in_specs=[
            pl.BlockSpec((1, tq, LANES), lambda b, g, i: (b, i, 12 + g)),
            pl.BlockSpec((1, seq, LANES), lambda b, g, i: (b, 0, 14 + g)),
            pl.BlockSpec((1, seq, LANES), lambda b, g, i: (b, 0, 16 + g)),
            pl.BlockSpec((1, tq, LANES), lambda b, g, i: (b, i, 0)),
            pl.BlockSpec((1, seq, LANES), lambda b, g, i: (b, 0, 0)),
        ],
        out_specs=pl.BlockSpec((1, tq, LANES), lambda b, g, i: (b, i, g)),
        out_shape=jax.ShapeDtypeStruct((bsz, seq, FOX_HEADS * HEAD_DIM), jnp.bfloat16),
        scratch_shapes=[pltpu.VMEM((2 * tq, 2 * LANES), jnp.bfloat16),
                        pltpu.VMEM((2 * tq, LANES), jnp.float32),
                        pltpu.VMEM((2 * tq, 2 * LANES), jnp.float32),
                        pltpu.VMEM((2 * tq, tk), jnp.float32),
                        pltpu.VMEM((2 * tq, tk), jnp.float32)],
        compiler_params=_cparams(("parallel", "parallel", "arbitrary")),
        name="fox_attention",
    )(b3, b3, b3, aq3, ak3)


def _score_key(score):
    bits = pltpu.bitcast(score, jnp.int32)
    return bits ^ ((bits >> 31) & 0x7FFFFFFF)


def _dsa_kernel(iq_ref, sq_ref, w_ref, ik_ref, sk_ref, v_ref, o_ref,
                keys_ref, cand_ref, candk_ref, qi_ref, qd_ref, wb_ref, m_ref, acc_ref, s0_ref, s1_ref,
                a0_ref, a1_ref, thr_ref, cnt_ref, flag_ref, *, tq, tc, ta, topk, idx_bits, slots):
    i = pl.program_id(1)
    lane = lax.broadcasted_iota(jnp.int32, (tq, LANES), 1)
    low = lane < HEAD_DIM

    for h in range(IDX_HEADS):
        blk = iq_ref[0, :, (h // 2) * LANES:(h // 2 + 1) * LANES]
        qi_ref[h * tq:(h + 1) * tq, :] = jnp.where(low if h % 2 == 0 else ~low, blk, jnp.zeros_like(blk))
    dsa_order = (0, 2, 1, 3)
    for slab, h in enumerate(dsa_order):
        blk = sq_ref[0, :, (h // 2) * LANES:(h // 2 + 1) * LANES]
        qd_ref[slab * tq:(slab + 1) * tq, :] = jnp.where(low if h % 2 == 0 else ~low, blk, jnp.zeros_like(blk))
    wt = w_ref[0]
    for h in range(IDX_HEADS):
        col = jnp.sum(jnp.where(lane == W_LANE0 + h, wt, 0.0), axis=1, keepdims=True)
        wb_ref[h] = jnp.broadcast_to(col, (tq, LANES))

    nlast = (i * tq) // tc
    nch = nlast + 1
    cand_ref[...] = jnp.full(cand_ref.shape, -jnp.inf, jnp.float32)
    row_pos = i * tq + lax.broadcasted_iota(jnp.int32, (tq, tc), 0)
    col_iota = lax.broadcasted_iota(jnp.int32, (tq, tc), 1)
    idx_groups = (slice(0, 4 * tq), slice(4 * tq, 8 * tq))
    dsa_groups = (slice(0, 2 * tq), slice(2 * tq, 4 * tq))

    def rep(x):
        return pltpu.repeat(x, tc // LANES, axis=1)

    def idx_produce(j, s_ref):
        start = pl.multiple_of(j * tc, tc)
        ik = ik_ref[0, pl.ds(start, tc), :]
        for rows in idx_groups:
            s_ref[rows, :] = lax.dot_general(qi_ref[rows, :], ik, _NT, preferred_element_type=jnp.float32)

    def idx_consume(j, s_ref, last):
        start = pl.multiple_of(j * tc, tc)
        score = jnp.zeros((tq, tc), jnp.float32)
        for h in range(IDX_HEADS):
            score = score + jnp.maximum(s_ref[h * tq:(h + 1) * tq, :], 0.0) * rep(wb_ref[h])
        if last:
            score = jnp.where(start + col_iota <= row_pos, score, -jnp.inf)
        keys_ref[:, pl.ds(start, tc)] = _score_key(score)
        for rg in range(tq // 8):
            rs = slice(rg * 8, (rg + 1) * 8)
            xs = [score[rs, u * LANES:(u + 1) * LANES] for u in range(tc // LANES)]
            for sl in range(slots):
                cs = slice(sl * LANES, (sl + 1) * LANES)
                a = cand_ref[rs, cs]
                for u in range(len(xs)):
                    hi = jnp.maximum(a, xs[u])
                    xs[u] = jnp.minimum(a, xs[u])
                    a = hi
                cand_ref[rs, cs] = a

    _pipelined_chunks(nlast, idx_produce, idx_consume, s0_ref, s1_ref)

    def count(pred):
        def body(c, acc):
            start = pl.multiple_of(c * tc, tc)
            hit = pred(keys_ref[:, pl.ds(start, tc)], start).astype(jnp.int32)
            for u in range(tc // LANES):
                acc = acc + hit[:, u * LANES:(u + 1) * LANES]
            return acc
        part = lax.fori_loop(0, nch, body, jnp.zeros((tq, LANES), jnp.int32))
        return jnp.broadcast_to(jnp.sum(part, axis=1, keepdims=True), (tq, LANES))

    n_row = i * tq + lax.broadcasted_iota(jnp.int32, (tq, LANES), 0) + 1

    def search_all_keys():
        def pending(state):
            bi, thr, cnt_thr = state
            return jnp.logical_and(bi < 32, jnp.max(jnp.where(n_row > topk, cnt_thr, topk)) > topk)

        def bit_step(state):
            bi, thr, cnt_thr = state
            cand = thr + lax.shift_left(jnp.int32(1), 31 - bi)
            cnt = count(lambda kc, start: kc >= rep(cand))
            ok = cnt >= topk
            return bi + 1, jnp.where(ok, cand, thr), jnp.where(ok, cnt, cnt_thr)

        total = jnp.zeros((tq, LANES), jnp.int32) + nch * tc
        init = (jnp.int32(0), jnp.full((tq, LANES), INT_MIN, jnp.int32), total)
        _, thr, cnt_thr = lax.while_loop(pending, bit_step, init)
        thr_ref[...] = thr
        cnt_ref[...] = cnt_thr

    def search_candidates():
        def count_ge(cands):
            parts = [jnp.zeros((tq, LANES), jnp.int32) for _ in cands]
            for sl in range(slots):
                piece = candk_ref[:, sl * LANES:(sl + 1) * LANES]
                parts = [p + (piece >= c).astype(jnp.int32) for p, c in zip(parts, cands)]
            return [jnp.broadcast_to(jnp.sum(p, axis=1, keepdims=True), (tq, LANES)) for p in parts]

        lo_slot = (topk - 1) // LANES
        kmax = jnp.broadcast_to(jnp.max(candk_ref[:, 0:LANES], axis=1, keepdims=True), (tq, LANES))
        kmin = jnp.broadcast_to(
            jnp.min(candk_ref[:, lo_slot * LANES:(lo_slot + 1) * LANES], axis=1, keepdims=True), (tq, LANES))
        nbits = 32 - lax.clz(kmax ^ kmin)
        low_mask = jnp.where(nbits >= 32, -1, lax.shift_left(jnp.int32(1), jnp.minimum(nbits, 31)) - 1)
        thr = (((kmax ^ INT_MIN) & ~low_mask) ^ INT_MIN)
        (cnt_thr,) = count_ge([thr])
        npairs = (jnp.max(nbits) + 1) // 2

        def pair_step(p, state):
            thr, cnt_thr = state
            b_lo = 2 * (npairs - 1 - p)
            one_lo = lax.shift_left(jnp.int32(1), b_lo)
            c1, c2, c3 = thr + one_lo, thr + 2 * one_lo, thr + 3 * one_lo
            n1, n2, n3 = count_ge([c1, c2, c3])
            ok1 = (b_lo < nbits) & (n1 >= topk)
            ok2 = (b_lo + 1 < nbits) & (n2 >= topk)
            ok3 = (b_lo + 1 < nbits) & (n3 >= topk)
            thr = jnp.where(ok3, c3, jnp.where(ok2, c2, jnp.where(ok1, c1, thr)))
            cnt_thr = jnp.where(ok3, n3, jnp.where(ok2, n2, jnp.where(ok1, n1, cnt_thr)))
            return thr, cnt_thr

        thr, cnt_thr = lax.fori_loop(0, npairs, pair_step, (thr, cnt_thr))
        thr_ref[...] = thr
        cnt_ref[...] = cnt_thr

    cand_chunks = slots * LANES // tc
    flag_ref[0] = 1

    @pl.when(nch > cand_chunks)
    def _():
        candk_ref[...] = _score_key(cand_ref[...])
        search_candidates()
        kept_min = jnp.max(candk_ref[:, (slots - 1) * LANES:slots * LANES], axis=1, keepdims=True)
        unsafe = jnp.where(kept_min >= thr_ref[...], 1, 0)
        flag_ref[0] = jnp.max(unsafe)

    @pl.when(flag_ref[0] > 0)
    def _():
        search_all_keys()

    thr = thr_ref[...]
    cnt_thr = cnt_ref[...]

    excess = jnp.where(thr > KEY_NEG_INF, cnt_thr - topk, 0)

    @pl.when(jnp.max(excess) > 0)
    def _():
        c_gt = count(lambda kc, start: kc > rep(thr))
        keep = topk - c_gt

        def idx_step(bi, x):
            cand = x + lax.shift_left(jnp.int32(1), idx_bits - 1 - bi)
            below = count(lambda kc, start: (kc == rep(thr)) & (start + col_iota < rep(cand)))
            return jnp.where(below < keep, cand, x)

        last = lax.fori_loop(0, idx_bits, idx_step, jnp.zeros((tq, LANES), jnp.int32))

        def demote(c, carry):
            start = pl.multiple_of(c * tc, tc)
            kc = keys_ref[:, pl.ds(start, tc)]
            drop = (rep(excess) > 0) & (kc == rep(thr)) & (start + col_iota > rep(last))
            keys_ref[:, pl.ds(start, tc)] = jnp.where(drop, kc - 1, kc)
            return carry

        lax.fori_loop(0, nch, demote, 0)

    sel_thr = jnp.maximum(thr, KEY_NEG_INF + 1)

    _flash_init(m_ref, acc_ref)
    vlow = lax.broadcasted_iota(jnp.int32, (ta, LANES), 1) < HEAD_DIM
    alast = (i * tq) // ta

    @pl.when((alast + 1) * ta > nch * tc)
    def _():
        keys_ref[:, pl.ds(pl.multiple_of(nch * tc, tc), tc)] = jnp.full((tq, tc), KEY_NEG_INF, jnp.int32)

    def att_produce(j, s_ref):
        start = pl.multiple_of(j * ta, ta)
        sk = sk_ref[0, pl.ds(start, ta), :]
        for rows in dsa_groups:
            s_ref[rows, :] = lax.dot_general(qd_ref[rows, :], sk, _NT, preferred_element_type=jnp.float32)

    def att_consume(j, s_ref, last):
        start = pl.multiple_of(j * ta, ta)
        v2 = v_ref[0, pl.ds(start, ta), :]
        one = jnp.ones_like(v2)
        v_even = jnp.where(vlow, v2, one)
        v_odd = jnp.where(vlow, one, v2)
        sel = keys_ref[:, pl.ds(start, ta)] >= pltpu.repeat(sel_thr, ta // LANES, axis=1)
        for rows, v1 in zip(dsa_groups, (v_even, v_odd)):
            s = s_ref[rows, :].reshape(2, tq, ta)
            s = jnp.where(sel[None], s, -jnp.inf).reshape(2 * tq, ta)
            _flash_rows(s, v1, m_ref, acc_ref, rows)

    _pipelined_chunks(alast, att_produce, att_consume, a0_ref, a1_ref)

    outs = {}
    for slab, h in enumerate(dsa_order):
        acc = acc_ref[slab * tq:(slab + 1) * tq, :]
        outs[h] = acc / pltpu.roll(acc, HEAD_DIM, axis=1)
    for p in range(DSA_HEADS // 2):
        o_ref[0, :, p * LANES:(p + 1) * LANES] = jnp.where(low, outs[2 * p], outs[2 * p + 1]).astype(o_ref.dtype)


def _dsa_attention(a3, b3, w3, tq=128, tc=512, slots=12):
    bsz, seq, _ = a3.shape
    ta = 2 * tc
    topk = min(DSA_TOPK_MAX, seq // 4)
    idx_bits = int(math.log2(seq))
    assert 2 ** idx_bits == seq and tc % tq == 0 and (slots * LANES) % tc == 0 and seq % ta == 0
    kern = functools.partial(_dsa_kernel, tq=tq, tc=tc, ta=ta, topk=topk, idx_bits=idx_bits, slots=slots)
    return pl.pallas_call(
        kern,
        grid=(bsz, seq // tq),
        in_specs=[
            pl.BlockSpec((1, tq, 4 * LANES), lambda b, i: (b, i, 2)),
            pl.BlockSpec((1, tq, 2 * LANES), lambda b, i: (b, i, 6)),
            pl.BlockSpec((1, tq, LANES), lambda b, i: (b, i, 0)),
            pl.BlockSpec((1, seq, LANES), lambda b, i: (b, 0, 14)),
            pl.BlockSpec((1, seq, LANES), lambda b, i: (b, 0, 15)),
            pl.BlockSpec((1, seq, LANES), lambda b, i: (b, 0, 18)),
        ],
        out_specs=pl.BlockSpec((1, tq, 2 * LANES), lambda b, i: (b, i, 0)),
        out_shape=jax.ShapeDtypeStruct((bsz, seq, DSA_HEADS * HEAD_DIM), jnp.bfloat16),
        scratch_shapes=[pltpu.VMEM((tq, seq), jnp.int32),
                        pltpu.VMEM((tq, slots * LANES), jnp.float32),
                        pltpu.VMEM((tq, slots * LANES), jnp.int32),
                        pltpu.VMEM((IDX_HEADS * tq, LANES), jnp.bfloat16),
                        pltpu.VMEM((DSA_HEADS * tq, LANES), jnp.bfloat16),
                        pltpu.VMEM((IDX_HEADS, tq, LANES), jnp.float32),
                        pltpu.VMEM((DSA_HEADS * tq, LANES), jnp.float32),
                        pltpu.VMEM((DSA_HEADS * tq, LANES), jnp.float32),
                        pltpu.VMEM((IDX_HEADS * tq, tc), jnp.float32),
                        pltpu.VMEM((IDX_HEADS * tq, tc), jnp.float32),
                        pltpu.VMEM((DSA_HEADS * tq, ta), jnp.float32),
                        pltpu.VMEM((DSA_HEADS * tq, ta), jnp.float32),
                        pltpu.VMEM((tq, LANES), jnp.int32),
                        pltpu.VMEM((tq, LANES), jnp.int32),
                        pltpu.SMEM((1,), jnp.int32)],
        compiler_params=_cparams(("parallel", "arbitrary")),
        name="dsa_attention",
    )(a3, a3, w3, a3, a3, b3)


def _out_kernel(x_ref, gate_ref, a_ref, bo_ref, co_ref, wa_ref, wb_ref, wc_ref, g_ref, b_ref, o_ref):
    gate = gate_ref[...].astype(jnp.float32)
    silu = gate / (1.0 + jnp.exp(-gate))

    def gated(ref, lo, hi):
        return (ref[...].astype(jnp.float32) * silu[:, lo:hi]).astype(jnp.bfloat16)

    out = jnp.dot(gated(a_ref, 0, 512), wa_ref[...], preferred_element_type=jnp.float32)
    out = out + jnp.dot(gated(bo_ref, 512, 768), wb_ref[...], preferred_element_type=jnp.float32)
    out = out + jnp.dot(gated(co_ref, 768, 1024), wc_ref[...], preferred_element_type=jnp.float32)
    y = DEEPNORM_ALPHA * x_ref[...] + out
    mu = jnp.mean(y, axis=1, keepdims=True)
    d = y - mu
    var = jnp.mean(d * d, axis=1, keepdims=True)
    o_ref[...] = d * lax.rsqrt(var + LN_EPS) * g_ref[...] + b_ref[...]


def _out_proj(x2d, b2d, a2d, bo2d, co2d, w_out, ln_g, ln_b, tm=256):
    m = x2d.shape[0]
    row = lambda i: (i, 0)
    fixed = lambda i: (0, 0)
    return pl.pallas_call(
        _out_kernel,
        grid=(m // tm,),
        in_specs=[
            pl.BlockSpec((tm, D_MODEL), row),
            pl.BlockSpec((tm, D_MODEL), row),
            pl.BlockSpec((tm, 512), row),
            pl.BlockSpec((tm, 256), row),
            pl.BlockSpec((tm, 256), row),
            pl.BlockSpec((512, D_MODEL), fixed),
            pl.BlockSpec((256, D_MODEL), lambda i: (2, 0)),
            pl.BlockSpec((256, D_MODEL), lambda i: (3, 0)),
            pl.BlockSpec((1, D_MODEL), fixed),
            pl.BlockSpec((1, D_MODEL), fixed),
        ],
        out_specs=pl.BlockSpec((tm, D_MODEL), row),
        out_shape=jax.ShapeDtypeStruct((m, D_MODEL), jnp.float32),
        compiler_params=_cparams(("parallel",)),
        name="out_proj",
    )(x2d, b2d, a2d, bo2d, co2d, w_out, w_out, w_out, ln_g, ln_b)


def _rope_tables(seq):
    inv = ROPE_THETA ** (-jnp.arange(0, HEAD_DIM, 2, dtype=jnp.float32) / HEAD_DIM)
    ang = jnp.arange(seq, dtype=jnp.int32).astype(jnp.float32)[:, None] * inv[None, :]
    reps = LANES // (HEAD_DIM // 2)
    return jnp.tile(jnp.cos(ang), (1, reps)), jnp.tile(jnp.sin(ang), (1, reps))


def kernel(x, w_in, b_f, lam_q1, lam_k1, lam_q2, lam_k2, g_subln, w_out, ln_g, ln_b):
    bsz, seq, _ = x.shape
    m = bsz * seq
    cos, sin = _rope_tables(seq)

    w_a = w_in[:, :, A_IDX] * A_SCALE
    w_ar = (w_a[:, :, A_ROT_PARTNER] * A_ROT_SIGN).astype(jnp.bfloat16)
    w_a = w_a.astype(jnp.bfloat16)
    w_b = (w_in[:, :, B_IDX] * B_SCALE).astype(jnp.bfloat16)
    w_c = (w_in[:, :, C_IDX] * C_SCALE).astype(jnp.bfloat16)
    w_o = w_out.astype(jnp.bfloat16)
    bf_rows = jnp.zeros((DEPTH, 1, LANES), jnp.float32).at[:, 0, F_LANE0:F_LANE0 + FOX_HEADS].set(b_f)

    x2d = x.reshape(m, D_MODEL)
    for l in range(DEPTH):
        lam_init = 0.8 - 0.6 * math.exp(-0.3 * l)
        lam_rows = jnp.zeros((8, LANES), jnp.float32)
        lam_rows = lam_rows.at[0, :HEAD_DIM].set(lam_q1[l]).at[1, :HEAD_DIM].set(lam_k1[l])
        lam_rows = lam_rows.at[2, :HEAD_DIM].set(lam_q2[l]).at[3, :HEAD_DIM].set(lam_k2[l])
        lam_rows = lam_rows.at[4, :].set(lam_init)

        a2d = _proj_rope(x2d, w_a[l], w_ar[l], cos, sin, seq)
        b2d = _proj_plain(x2d, w_b[l])
        wf2d, aq2d, ak2d = _proj_small(x2d, w_c[l], bf_rows[l], seq)

        a3 = a2d.reshape(bsz, seq, A_WIDTH)
        b3 = b2d.reshape(bsz, seq, B_WIDTH)
        wf3 = wf2d.reshape(bsz, seq, LANES)

        diff_o = _diff_attention(a3, b3, lam_rows, g_subln[l].reshape(1, LANES))
        fox_o = _fox_attention(b3, aq2d.reshape(bsz, seq, LANES), ak2d.reshape(bsz, seq, LANES))
        dsa_o = _dsa_attention(a3, b3, wf3)

        x2d = _out_proj(x2d, b2d, diff_o.reshape(m, 512), dsa_o.reshape(m, 256), fox_o.reshape(m, 256),
                        w_o[l], ln_g[l].reshape(1, D_MODEL), ln_b[l].reshape(1, D_MODEL))
    return x2d.reshape(bsz, seq, D_MODEL)
```

```python
import functools
import math

import numpy as np
import jax
import jax.numpy as jnp
from jax import lax
from jax.experimental import pallas as pl
from jax.experimental.pallas import tpu as pltpu

D_MODEL = 1024
DEPTH = 4
HEAD_DIM = 64
DIFF_HEADS = 4
DSA_HEADS = 4
IDX_HEADS = 8
FOX_HEADS = 4
DSA_TOPK_MAX = 256
ROPE_THETA = 10000.0
LN_EPS = 1e-5
SUBLN_EPS = 1e-5
DEEPNORM_ALPHA = (2 * DEPTH) ** 0.25
IDX_WEIGHT_SCALE = (IDX_HEADS * HEAD_DIM) ** -0.5
LOG2E = math.log2(math.e)
Q_SCALE = HEAD_DIM ** -0.5 * LOG2E

LANES = 128
VMEM_LIMIT_BYTES = 56 * 1024 * 1024

_OFF = {}
_o = 0
for _name, _n in (('diff_q', 512), ('diff_k', 512), ('diff_v', 512), ('dsa_q', 256), ('dsa_k', 64),
                  ('dsa_v', 64), ('idx_q', 512), ('idx_k', 64), ('idx_w', 8), ('fox_q', 256),
                  ('fox_k', 256), ('fox_v', 256), ('fox_f', 4), ('gate', 1024)):
    _OFF[_name] = (_o, _n)
    _o += _n
IN_WIDTH = _o


def _cols(name, scale=1.0, repeat=1):
    o, n = _OFF[name]
    idx = np.tile(np.arange(o, o + n), repeat)
    return idx, np.full(idx.shape, scale, np.float32)


def _pad(n):
    return np.zeros((n,), np.int64), np.zeros((n,), np.float32)


def _layout(parts):
    idx = np.concatenate([p[0] for p in parts])
    scale = np.concatenate([p[1] for p in parts])
    return idx, scale


A_IDX, A_SCALE = _layout([_cols('diff_q', Q_SCALE), _cols('diff_k'), _cols('idx_q'),
                          _cols('dsa_q', Q_SCALE), _cols('idx_k', repeat=2), _cols('dsa_k', repeat=2)])
A_WIDTH = A_IDX.shape[0]
B_IDX, B_SCALE = _layout([_cols('gate'), _cols('diff_v'), _cols('fox_q', Q_SCALE), _cols('fox_k'),
                          _cols('fox_v'), _cols('dsa_v', repeat=2), _pad(128)])
B_WIDTH = B_IDX.shape[0]
C_IDX, C_SCALE = _layout([_cols('idx_w'), _cols('fox_f'), _pad(LANES - 12)])
W_LANE0 = 0
F_LANE0 = 8

_c = np.arange(A_WIDTH)
_first_half = (_c % HEAD_DIM) < HEAD_DIM // 2
A_ROT_PARTNER = np.where(_first_half, _c + HEAD_DIM // 2, _c - HEAD_DIM // 2)
A_ROT_SIGN = np.where(_first_half, -1.0, 1.0).astype(np.float32)

_NT = (((1,), (1,)), ((), ()))
INT_MIN = -2 ** 31
KEY_NEG_INF = -2139095041

_SORT4 = ((0, 1), (2, 3), (0, 2), (1, 3), (1, 2))
_BITONIC12 = ((0, 8), (1, 9), (2, 10), (3, 11), (4, 8), (5, 9), (6, 10), (7, 11),
              (0, 2), (1, 3), (4, 6), (5, 7), (8, 10), (9, 11),
              (0, 1), (2, 3), (4, 5), (6, 7), (8, 9), (10, 11))
DSA_SLOTS = 12
DSA_SCORE_CHUNK = 512


def _cparams(sem):
    return pltpu.CompilerParams(dimension_semantics=sem, vmem_limit_bytes=VMEM_LIMIT_BYTES)


def _proj_rope_kernel(x_ref, w_ref, wr_ref, cos_ref, sin_ref, o_ref):
    xb = x_ref[...].astype(jnp.bfloat16)
    h = jnp.dot(xb, w_ref[...], preferred_element_type=jnp.float32)
    hr = jnp.dot(xb, wr_ref[...], preferred_element_type=jnp.float32)
    cos = cos_ref[...]
    sin = sin_ref[...]
    for c in range(o_ref.shape[1] // LANES):
        sl = slice(c * LANES, (c + 1) * LANES)
        o_ref[:, sl] = (h[:, sl] * cos + hr[:, sl] * sin).astype(o_ref.dtype)


def _proj_rope(x2d, w, wr, cos, sin, seq, tm=512, tn=512):
    m = x2d.shape[0]
    nseq = seq // tm
    return pl.pallas_call(
        _proj_rope_kernel,
        grid=(m // tm, A_WIDTH // tn),
        in_specs=[
            pl.BlockSpec((tm, D_MODEL), lambda i, j: (i, 0)),
            pl.BlockSpec((D_MODEL, tn), lambda i, j: (0, j)),
            pl.BlockSpec((D_MODEL, tn), lambda i, j: (0, j)),
            pl.BlockSpec((tm, LANES), lambda i, j: (i % nseq, 0)),
            pl.BlockSpec((tm, LANES), lambda i, j: (i % nseq, 0)),
        ],
        out_specs=pl.BlockSpec((tm, tn), lambda i, j: (i, j)),
        out_shape=jax.ShapeDtypeStruct((m, A_WIDTH), jnp.bfloat16),
        compiler_params=_cparams(("parallel", "arbitrary")),
        name="proj_rope",
    )(x2d, w, wr, cos, sin)


def _proj_plain_kernel(x_ref, w_ref, o_ref):
    xb = x_ref[...].astype(jnp.bfloat16)
    o_ref[...] = jnp.dot(xb, w_ref[...], preferred_element_type=jnp.float32).astype(o_ref.dtype)


def _proj_plain(x2d, w, tm=512, tn=512):
    m = x2d.shape[0]
    return pl.pallas_call(
        _proj_plain_kernel,
        grid=(m // tm, B_WIDTH // tn),
        in_specs=[
            pl.BlockSpec((tm, D_MODEL), lambda i, j: (i, 0)),
            pl.BlockSpec((D_MODEL, tn), lambda i, j: (0, j)),
        ],
        out_specs=pl.BlockSpec((tm, tn), lambda i, j: (i, j)),
        out_shape=jax.ShapeDtypeStruct((m, B_WIDTH), jnp.bfloat16),
        compiler_params=_cparams(("parallel", "arbitrary")),
        name="proj_plain",
    )(x2d, w)


def _split3(v):
    hi = v.astype(jnp.bfloat16)
    r = v - hi.astype(jnp.float32)
    mid = r.astype(jnp.bfloat16)
    lo = (r - mid.astype(jnp.float32)).astype(jnp.bfloat16)
    return hi, mid, lo


def _gate_aug_constants():
    place = np.zeros((6, LANES, LANES), np.float32)
    ones = np.zeros((8, LANES), np.float32)
    for h in range(FOX_HEADS):
        for j in range(3):
            place[j, F_LANE0 + h, 8 * h + j] = 1.0
            place[3 + j, F_LANE0 + h, 8 * h + 3 + j] = -1.0
            ones[0, 8 * h + 3 + j] = 1.0
            ones[1, 8 * h + j] = 1.0
    return jnp.asarray(place, jnp.bfloat16), jnp.asarray(ones)


def _proj_small_kernel(x_ref, w_ref, bf_ref, tri_ref, place_ref, ones_ref, o_ref, aq_ref, ak_ref, carry_ref,
                       *, blocks_per_seq):
    i = pl.program_id(0)
    xb = x_ref[...].astype(jnp.bfloat16)
    h = jnp.dot(xb, w_ref[...], preferred_element_type=jnp.float32)
    lane = lax.broadcasted_iota(jnp.int32, h.shape, 1)
    ff = h + bf_ref[...]
    logf = jnp.minimum(ff, 0.0) - jnp.log(1.0 + jnp.exp(-jnp.abs(ff)))
    o = jnp.where(lane < F_LANE0, h * IDX_WEIGHT_SCALE, logf * LOG2E)
    o_ref[...] = o

    @pl.when(i % blocks_per_seq == 0)
    def _():
        carry_ref[...] = jnp.zeros_like(carry_ref)

    tri = tri_ref[...]
    hi, mid, lo = _split3(o)
    cs = (jnp.dot(tri, hi, preferred_element_type=jnp.float32)
          + jnp.dot(tri, mid, preferred_element_type=jnp.float32)
          + jnp.dot(tri, lo, preferred_element_type=jnp.float32))
    cs = cs + carry_ref[0:1, :]
    tm = cs.shape[0]
    carry_ref[...] = jnp.broadcast_to(cs[tm - 1:tm, :], carry_ref.shape)
    parts = _split3(cs)
    aq = ones_ref[0:1, :]
    ak = ones_ref[1:2, :]
    for j in range(3):
        aq = aq + jnp.dot(parts[j], place_ref[j], preferred_element_type=jnp.float32)
        ak = ak + jnp.dot(parts[j], place_ref[3 + j], preferred_element_type=jnp.float32)
    aq_ref[...] = aq.astype(aq_ref.dtype)
    ak_ref[...] = ak.astype(ak_ref.dtype)


def _proj_small(x2d, w, bf_row, seq, tm=512):
    m = x2d.shape[0]
    tri = jnp.asarray(np.tril(np.ones((tm, tm), np.float32)), jnp.bfloat16)
    place, ones = _gate_aug_constants()
    kern = functools.partial(_proj_small_kernel, blocks_per_seq=seq // tm)
    row = lambda i: (i, 0)
    return pl.pallas_call(
        kern,
        grid=(m // tm,),
        in_specs=[
            pl.BlockSpec((tm, D_MODEL), row),
            pl.BlockSpec((D_MODEL, LANES), lambda i: (0, 0)),
            pl.BlockSpec((1, LANES), lambda i: (0, 0)),
            pl.BlockSpec((tm, tm), lambda i: (0, 0)),
            pl.BlockSpec((6, LANES, LANES), lambda i: (0, 0, 0)),
            pl.BlockSpec((8, LANES), lambda i: (0, 0)),
        ],
        out_specs=[pl.BlockSpec((tm, LANES), row), pl.BlockSpec((tm, LANES), row), pl.BlockSpec((tm, LANES), row)],
        out_shape=[jax.ShapeDtypeStruct((m, LANES), jnp.float32),
                   jax.ShapeDtypeStruct((m, LANES), jnp.bfloat16),
                   jax.ShapeDtypeStruct((m, LANES), jnp.bfloat16)],
        scratch_shapes=[pltpu.VMEM((8, LANES), jnp.float32)],
        compiler_params=_cparams(("arbitrary",)),
        name="proj_small",
    )(x2d, w, bf_row, tri, place, ones)


def _flash_init(m_ref, acc_ref):
    m_ref[...] = jnp.full(m_ref.shape, -1e30, jnp.float32)
    acc_ref[...] = jnp.zeros(acc_ref.shape, jnp.float32)


def _flash_rows(s, v1, m_ref, acc_ref, rows):
    tk = s.shape[1]
    m_prev = m_ref[rows, :]
    m_next = jnp.maximum(m_prev, jnp.max(s, axis=1, keepdims=True))
    p = jnp.exp2((s - pltpu.repeat(m_next, tk // LANES, axis=1)).astype(jnp.bfloat16))
    alpha = jnp.exp2(m_prev - m_next)
    m_ref[rows, :] = m_next
    pv = jnp.dot(p, v1, preferred_element_type=jnp.float32)
    acc_ref[rows, :] = acc_ref[rows, :] * pltpu.repeat(alpha, v1.shape[1] // LANES, axis=1) + pv


def _pipelined_chunks(n_full, produce, consume, buf0, buf1):
    produce(0, buf0)

    def body(p, carry):
        produce(2 * p + 1, buf1)
        consume(2 * p, buf0, False)
        produce(2 * p + 2, buf0)
        consume(2 * p + 1, buf1, False)
        return carry

    lax.fori_loop(0, n_full // 2, body, 0)

    @pl.when(n_full % 2 == 1)
    def _():
        produce(n_full, buf1)
        consume(n_full - 1, buf0, False)
        consume(n_full, buf1, True)

    @pl.when(n_full % 2 == 0)
    def _():
        consume(n_full, buf0, True)


def _stack_pair(q, qs_ref, t):
    lane = lax.broadcasted_iota(jnp.int32, q.shape, 1)
    zero = jnp.zeros_like(q)
    qs_ref[0:t, :] = jnp.where(lane < HEAD_DIM, q, zero)
    qs_ref[t:2 * t, :] = jnp.where(lane >= HEAD_DIM, q, zero)


def _with_ones(v):
    return jnp.concatenate([v, jnp.ones((v.shape[0], LANES), v.dtype)], axis=1)


def _diff_kernel(lam_ref, g_ref, q_ref, k_ref, v_ref, o_ref, qs_ref, m_ref, acc_ref, s0_ref, s1_ref,
                 *, tq, tk):
    i = pl.program_id(2)
    _stack_pair(q_ref[0], qs_ref, tq)
    _flash_init(m_ref, acc_ref)
    groups = (slice(0, tq), slice(tq, 2 * tq))

    def produce(j, s_ref):
        start = pl.multiple_of(j * tk, tk)
        k = k_ref[0, pl.ds(start, tk), :]
        for rows in groups:
            s_ref[rows, :] = lax.dot_general(qs_ref[rows, :], k, _NT, preferred_element_type=jnp.float32)

    def consume(j, s_ref, last):
        start = pl.multiple_of(j * tk, tk)
        v1 = _with_ones(v_ref[0, pl.ds(start, tk), :])
        for rows in groups:
            s = s_ref[rows, :]
            if last:
                r = lax.broadcasted_iota(jnp.int32, (tq, tk), 0)
                c = lax.broadcasted_iota(jnp.int32, (tq, tk), 1)
                s = jnp.where(start + c <= i * tq + r, s, -jnp.inf)
            _flash_rows(s, v1, m_ref, acc_ref, rows)

    _pipelined_chunks((i * tq) // tk, produce, consume, s0_ref, s1_ref)

    lam_rows = lam_ref[...]
    s1 = jnp.sum(lam_rows[0:1, :] * lam_rows[1:2, :], axis=1, keepdims=True)
    s2 = jnp.sum(lam_rows[2:3, :] * lam_rows[3:4, :], axis=1, keepdims=True)
    lam_init = lam_rows[4:5, 0:1]
    lam = jnp.exp(s1) - jnp.exp(s2) + lam_init
    o0 = acc_ref[0:tq, 0:LANES] / acc_ref[0:tq, LANES:2 * LANES]
    o1 = acc_ref[tq:2 * tq, 0:LANES] / acc_ref[tq:2 * tq, LANES:2 * LANES]
    a = o0 - lam * o1
    ms = jnp.mean(a * a, axis=1, keepdims=True)
    a = a * lax.rsqrt(ms + SUBLN_EPS) * g_ref[...] * (1.0 - lam_init)
    o_ref[0] = a.astype(o_ref.dtype)


def _diff_attention(a3, b3, lam_rows, g_row, tq=512, tk=1024):
    bsz, seq, _ = a3.shape
    tk = min(tk, seq)
    kern = functools.partial(_diff_kernel, tq=tq, tk=tk)
    return pl.pallas_call(
        kern,
        grid=(bsz, DIFF_HEADS, seq // tq),
        in_specs=[
            pl.BlockSpec((8, LANES), lambda b, h, i: (0, 0)),
            pl.BlockSpec((1, LANES), lambda b, h, i: (0, 0)),
            pl.BlockSpec((1, tq, LANES), lambda b, h, i: (b, i, h)),
            pl.BlockSpec((1, seq, LANES), lambda b, h, i: (b, 0, 4 + h)),
            pl.BlockSpec((1, seq, LANES), lambda b, h, i: (b, 0, 8 + h)),
        ],
        out_specs=pl.BlockSpec((1, tq, LANES), lambda b, h, i: (b, i, h)),
        out_shape=jax.ShapeDtypeStruct((bsz, seq, DIFF_HEADS * LANES), jnp.bfloat16),
        scratch_shapes=[pltpu.VMEM((2 * tq, LANES), jnp.bfloat16),
                        pltpu.VMEM((2 * tq, LANES), jnp.float32),
                        pltpu.VMEM((2 * tq, 2 * LANES), jnp.float32),
                        pltpu.VMEM((2 * tq, tk), jnp.float32),
                        pltpu.VMEM((2 * tq, tk), jnp.float32)],
        compiler_params=_cparams(("parallel", "parallel", "arbitrary")),
        name="diff_attention",
    )(lam_rows, g_row, a3, a3, b3)


def _fox_kernel(q_ref, k_ref, v_ref, aq_ref, ak_ref, o_ref, qs_ref, m_ref, acc_ref, s0_ref, s1_ref, *, tq, tk):
    g = pl.program_id(1)
    i = pl.program_id(2)
    q = q_ref[0]
    aq = aq_ref[0]
    lane = lax.broadcasted_iota(jnp.int32, (tq, LANES), 1)
    zero = jnp.zeros_like(q)
    for a in range(2):
        rows = slice(a * tq, (a + 1) * tq)
        head_lanes = (lane < HEAD_DIM) if a == 0 else (lane >= HEAD_DIM)
        first = 8 * (2 * g + a)
        qs_ref[rows, 0:LANES] = jnp.where(head_lanes, q, zero)
        qs_ref[rows, LANES:2 * LANES] = jnp.where((lane >= first) & (lane < first + 6), aq, zero)
    _flash_init(m_ref, acc_ref)
    groups = (slice(0, tq), slice(tq, 2 * tq))

    def produce(j, s_ref):
        start = pl.multiple_of(j * tk, tk)
        k2 = jnp.concatenate([k_ref[0, pl.ds(start, tk), :], ak_ref[0, pl.ds(start, tk), :]], axis=1)
        for rows in groups:
            s_ref[rows, :] = lax.dot_general(qs_ref[rows, :], k2, _NT, preferred_element_type=jnp.float32)

    def consume(j, s_ref, last):
        start = pl.multiple_of(j * tk, tk)
        v1 = _with_ones(v_ref[0, pl.ds(start, tk), :])
        for rows in groups:
            s = s_ref[rows, :]
            if last:
                r = lax.broadcasted_iota(jnp.int32, (tq, tk), 0)
                c = lax.broadcasted_iota(jnp.int32, (tq, tk), 1)
                s = jnp.where(start + c <= i * tq + r, s, -jnp.inf)
            _flash_rows(s, v1, m_ref, acc_ref, rows)

    _pipelined_chunks((i * tq) // tk, produce, consume, s0_ref, s1_ref)

    o0 = acc_ref[0:tq, 0:LANES] / acc_ref[0:tq, LANES:2 * LANES]
    o1 = acc_ref[tq:2 * tq, 0:LANES] / acc_ref[tq:2 * tq, LANES:2 * LANES]
    o_ref[0] = jnp.where(lane < HEAD_DIM, o0, o1).astype(o_ref.dtype)


def _fox_attention(b3, aq3, ak3, tq=512, tk=1024):
    bsz, seq, _ = b3.shape
    tk = min(tk, seq)
    kern = functools.partial(_fox_kernel, tq=tq, tk=tk)
    return pl.pallas_call(
        kern,
        grid=(bsz, FOX_HEADS // 2, seq // tq),
        in_specs=[
            pl.BlockSpec((1, tq, LANES), lambda b, g, i: (b, i, 12 + g)),
            pl.BlockSpec((1, seq, LANES), lambda b, g, i: (b, 0, 14 + g)),
            pl.BlockSpec((1, seq, LANES), lambda b, g, i: (b, 0, 16 + g)),
            pl.BlockSpec((1, tq, LANES), lambda b, g, i: (b, i, 0)),
            pl.BlockSpec((1, seq, LANES), lambda b, g, i: (b, 0, 0)),
        ],
        out_specs=pl.BlockSpec((1, tq, LANES), lambda b, g, i: (b, i, g)),
        out_shape=jax.ShapeDtypeStruct((bsz, seq, FOX_HEADS * HEAD_DIM), jnp.bfloat16),
        scratch_shapes=[pltpu.VMEM((2 * tq, 2 * LANES), jnp.bfloat16),
                        pltpu.VMEM((2 * tq, LANES), jnp.float32),
                        pltpu.VMEM((2 * tq, 2 * LANES), jnp.float32),
                        pltpu.VMEM((2 * tq, tk), jnp.float32),
                        pltpu.VMEM((2 * tq, tk), jnp.float32)],
        compiler_params=_cparams(("parallel", "parallel", "arbitrary")),
        name="fox_attention",
    )(b3, b3, b3, aq3, ak3)


def _score_key(score):
    bits = pltpu.bitcast(score, jnp.int32)
    return bits ^ ((bits >> 31) & 0x7FFFFFFF)


def _dsa_kernel(iq_ref, sq_ref, w_ref, ik_ref, sk_ref, v_ref, o_ref,
                keys_ref, cand_ref, candk_ref, qi_ref, qd_ref, wb_ref, m_ref, acc_ref, s0_ref, s1_ref,
                a0_ref, a1_ref, thr_ref, cnt_ref, flag_ref, *, tq, tc, ta, topk, idx_bits, slots):
    i = pl.program_id(1)
    lane = lax.broadcasted_iota(jnp.int32, (tq, LANES), 1)
    low = lane < HEAD_DIM

    for h in range(IDX_HEADS):
        blk = iq_ref[0, :, (h // 2) * LANES:(h // 2 + 1) * LANES]
        qi_ref[h * tq:(h + 1) * tq, :] = jnp.where(low if h % 2 == 0 else ~low, blk, jnp.zeros_like(blk))
    dsa_order = (0, 2, 1, 3)
    for slab, h in enumerate(dsa_order):
        blk = sq_ref[0, :, (h // 2) * LANES:(h // 2 + 1) * LANES]
        qd_ref[slab * tq:(slab + 1) * tq, :] = jnp.where(low if h % 2 == 0 else ~low, blk, jnp.zeros_like(blk))
    wt = w_ref[0]
    for h in range(IDX_HEADS):
        col = jnp.sum(jnp.where(lane == W_LANE0 + h, wt, 0.0), axis=1, keepdims=True)
        wb_ref[h] = jnp.broadcast_to(col, (tq, LANES))

    nlast = (i * tq) // tc
    nch = nlast + 1
    cand_ref[...] = jnp.full(cand_ref.shape, -jnp.inf, jnp.float32)
    row_pos = i * tq + lax.broadcasted_iota(jnp.int32, (tq, tc), 0)
    col_iota = lax.broadcasted_iota(jnp.int32, (tq, tc), 1)
    idx_groups = (slice(0, 4 * tq), slice(4 * tq, 8 * tq))
    dsa_groups = (slice(0, 2 * tq), slice(2 * tq, 4 * tq))

    def rep(x):
        return pltpu.repeat(x, tc // LANES, axis=1)

    def idx_produce(j, s_ref):
        start = pl.multiple_of(j * tc, tc)
        ik = ik_ref[0, pl.ds(start, tc), :]
        for rows in idx_groups:
            s_ref[rows, :] = lax.dot_general(qi_ref[rows, :], ik, _NT, preferred_element_type=jnp.float32)

    def idx_consume(j, s_ref, last):
        start = pl.multiple_of(j * tc, tc)
        score = jnp.zeros((tq, tc), jnp.float32)
        for h in range(IDX_HEADS):
            score = score + jnp.maximum(s_ref[h * tq:(h + 1) * tq, :], 0.0) * rep(wb_ref[h])
        if last:
            score = jnp.where(start + col_iota <= row_pos, score, -jnp.inf)
        keys_ref[:, pl.ds(start, tc)] = _score_key(score)
        for rg in range(tq // 8):
            rs = slice(rg * 8, (rg + 1) * 8)
            xs = [score[rs, u * LANES:(u + 1) * LANES] for u in range(4)]
            for p, q in _SORT4:
                xs[p], xs[q] = jnp.maximum(xs[p], xs[q]), jnp.minimum(xs[p], xs[q])
            c = [cand_ref[rs, sl * LANES:(sl + 1) * LANES] for sl in range(12)]
            for t in range(4):
                c[8 + t] = jnp.maximum(c[8 + t], xs[3 - t])
            for p, q in _BITONIC12:
                c[p], c[q] = jnp.maximum(c[p], c[q]), jnp.minimum(c[p], c[q])
            for sl in range(12):
                cand_ref[rs, sl * LANES:(sl + 1) * LANES] = c[sl]

    _pipelined_chunks(nlast, idx_produce, idx_consume, s0_ref, s1_ref)

    def count(pred):
        def body(c, acc):
            start = pl.multiple_of(c * tc, tc)
            hit = pred(keys_ref[:, pl.ds(start, tc)], start).astype(jnp.int32)
            for u in range(tc // LANES):
                acc = acc + hit[:, u * LANES:(u + 1) * LANES]
            return acc
        part = lax.fori_loop(0, nch, body, jnp.zeros((tq, LANES), jnp.int32))
        return jnp.broadcast_to(jnp.sum(part, axis=1, keepdims=True), (tq, LANES))

    n_row = i * tq + lax.broadcasted_iota(jnp.int32, (tq, LANES), 0) + 1

    def search_all_keys():
        def pending(state):
            bi, thr, cnt_thr = state
            return jnp.logical_and(bi < 32, jnp.max(jnp.where(n_row > topk, cnt_thr, topk)) > topk)

        def bit_step(state):
            bi, thr, cnt_thr = state
            cand = thr + lax.shift_left(jnp.int32(1), 31 - bi)
            cnt = count(lambda kc, start: kc >= rep(cand))
            ok = cnt >= topk
            return bi + 1, jnp.where(ok, cand, thr), jnp.where(ok, cnt, cnt_thr)

        total = jnp.zeros((tq, LANES), jnp.int32) + nch * tc
        init = (jnp.int32(0), jnp.full((tq, LANES), INT_MIN, jnp.int32), total)
        _, thr, cnt_thr = lax.while_loop(pending, bit_step, init)
        thr_ref[...] = thr
        cnt_ref[...] = cnt_thr

    def search_candidates():
        def bit_step(bi, state):
            thr, cnt_thr = state
            cand = thr + lax.shift_left(jnp.int32(1), 31 - bi)
            part = jnp.zeros((tq, LANES), jnp.int32)
            for sl in range(slots):
                part = part + (candk_ref[:, sl * LANES:(sl + 1) * LANES] >= cand).astype(jnp.int32)
            cnt = jnp.broadcast_to(jnp.sum(part, axis=1, keepdims=True), (tq, LANES))
            ok = cnt >= topk
            return jnp.where(ok, cand, thr), jnp.where(ok, cnt, cnt_thr)

        init = (jnp.full((tq, LANES), INT_MIN, jnp.int32), jnp.full((tq, LANES), slots * LANES, jnp.int32))
        thr, cnt_thr = lax.fori_loop(0, 32, bit_step, init, unroll=4)
        thr_ref[...] = thr
        cnt_ref[...] = cnt_thr

    cand_chunks = slots * LANES // tc
    flag_ref[0] = 1

    @pl.when(nch > cand_chunks)
    def _():
        candk_ref[...] = _score_key(cand_ref[...])
        search_candidates()
        kept_min = jnp.max(candk_ref[:, (slots - 1) * LANES:slots * LANES], axis=1, keepdims=True)
        unsafe = jnp.where(kept_min >= thr_ref[...], 1, 0)
        flag_ref[0] = jnp.max(unsafe)

    @pl.when(flag_ref[0] > 0)
    def _():
        search_all_keys()

    thr = thr_ref[...]
    cnt_thr = cnt_ref[...]

    excess = jnp.where(thr > KEY_NEG_INF, cnt_thr - topk, 0)

    @pl.when(jnp.max(excess) > 0)
    def _():
        c_gt = count(lambda kc, start: kc > rep(thr))
        keep = topk - c_gt

        def idx_step(bi, x):
            cand = x + lax.shift_left(jnp.int32(1), idx_bits - 1 - bi)
            below = count(lambda kc, start: (kc == rep(thr)) & (start + col_iota < rep(cand)))
            return jnp.where(below < keep, cand, x)

        last = lax.fori_loop(0, idx_bits, idx_step, jnp.zeros((tq, LANES), jnp.int32))

        def demote(c, carry):
            start = pl.multiple_of(c * tc, tc)
            kc = keys_ref[:, pl.ds(start, tc)]
            drop = (rep(excess) > 0) & (kc == rep(thr)) & (start + col_iota > rep(last))
            keys_ref[:, pl.ds(start, tc)] = jnp.where(drop, kc - 1, kc)
            return carry

        lax.fori_loop(0, nch, demote, 0)

    sel_thr = jnp.maximum(thr, KEY_NEG_INF + 1)

    _flash_init(m_ref, acc_ref)
    vlow = lax.broadcasted_iota(jnp.int32, (ta, LANES), 1) < HEAD_DIM
    alast = (i * tq) // ta

    @pl.when((alast + 1) * ta > nch * tc)
    def _():
        keys_ref[:, pl.ds(pl.multiple_of(nch * tc, tc), tc)] = jnp.full((tq, tc), KEY_NEG_INF, jnp.int32)

    def att_produce(j, s_ref):
        start = pl.multiple_of(j * ta, ta)
        sk = sk_ref[0, pl.ds(start, ta), :]
        for rows in dsa_groups:
            s_ref[rows, :] = lax.dot_general(qd_ref[rows, :], sk, _NT, preferred_element_type=jnp.float32)

    def att_consume(j, s_ref, last):
        start = pl.multiple_of(j * ta, ta)
        v2 = v_ref[0, pl.ds(start, ta), :]
        one = jnp.ones_like(v2)
        v_even = jnp.where(vlow, v2, one)
        v_odd = jnp.where(vlow, one, v2)
        sel = keys_ref[:, pl.ds(start, ta)] >= pltpu.repeat(sel_thr, ta // LANES, axis=1)
        for rows, v1 in zip(dsa_groups, (v_even, v_odd)):
            s = s_ref[rows, :].reshape(2, tq, ta)
            s = jnp.where(sel[None], s, -jnp.inf).reshape(2 * tq, ta)
            _flash_rows(s, v1, m_ref, acc_ref, rows)

    _pipelined_chunks(alast, att_produce, att_consume, a0_ref, a1_ref)

    outs = {}
    for slab, h in enumerate(dsa_order):
        acc = acc_ref[slab * tq:(slab + 1) * tq, :]
        outs[h] = acc / pltpu.roll(acc, HEAD_DIM, axis=1)
    for p in range(DSA_HEADS // 2):
        o_ref[0, :, p * LANES:(p + 1) * LANES] = jnp.where(low, outs[2 * p], outs[2 * p + 1]).astype(o_ref.dtype)


def _dsa_attention(a3, b3, w3, tq=128):
    bsz, seq, _ = a3.shape
    tc, slots = DSA_SCORE_CHUNK, DSA_SLOTS
    ta = 2 * tc
    topk = min(DSA_TOPK_MAX, seq // 4)
    idx_bits = int(math.log2(seq))
    assert 2 ** idx_bits == seq and tc % tq == 0 and (slots * LANES) % tc == 0 and seq % ta == 0
    kern = functools.partial(_dsa_kernel, tq=tq, tc=tc, ta=ta, topk=topk, idx_bits=idx_bits, slots=slots)
    return pl.pallas_call(
        kern,
        grid=(bsz, seq // tq),
        in_specs=[
            pl.BlockSpec((1, tq, 4 * LANES), lambda b, i: (b, i, 2)),
            pl.BlockSpec((1, tq, 2 * LANES), lambda b, i: (b, i, 6)),
            pl.BlockSpec((1, tq, LANES), lambda b, i: (b, i, 0)),
            pl.BlockSpec((1, seq, LANES), lambda b, i: (b, 0, 14)),
            pl.BlockSpec((1, seq, LANES), lambda b, i: (b, 0, 15)),
            pl.BlockSpec((1, seq, LANES), lambda b, i: (b, 0, 18)),
        ],
        out_specs=pl.BlockSpec((1, tq, 2 * LANES), lambda b, i: (b, i, 0)),
        out_shape=jax.ShapeDtypeStruct((bsz, seq, DSA_HEADS * HEAD_DIM), jnp.bfloat16),
        scratch_shapes=[pltpu.VMEM((tq, seq), jnp.int32),
                        pltpu.VMEM((tq, slots * LANES), jnp.float32),
                        pltpu.VMEM((tq, slots * LANES), jnp.int32),
                        pltpu.VMEM((IDX_HEADS * tq, LANES), jnp.bfloat16),
                        pltpu.VMEM((DSA_HEADS * tq, LANES), jnp.bfloat16),
                        pltpu.VMEM((IDX_HEADS, tq, LANES), jnp.float32),
                        pltpu.VMEM((DSA_HEADS * tq, LANES), jnp.float32),
                        pltpu.VMEM((DSA_HEADS * tq, LANES), jnp.float32),
                        pltpu.VMEM((IDX_HEADS * tq, tc), jnp.float32),
                        pltpu.VMEM((IDX_HEADS * tq, tc), jnp.float32),
                        pltpu.VMEM((DSA_HEADS * tq, ta), jnp.float32),
                        pltpu.VMEM((DSA_HEADS * tq, ta), jnp.float32),
                        pltpu.VMEM((tq, LANES), jnp.int32),
                        pltpu.VMEM((tq, LANES), jnp.int32),
                        pltpu.SMEM((1,), jnp.int32)],
        compiler_params=_cparams(("parallel", "arbitrary")),
        name="dsa_attention",
    )(a3, a3, w3, a3, a3, b3)


def _out_kernel(x_ref, gate_ref, a_ref, bo_ref, co_ref, wa_ref, wb_ref, wc_ref, g_ref, b_ref, o_ref):
    gate = gate_ref[...].astype(jnp.float32)
    silu = gate / (1.0 + jnp.exp(-gate))

    def gated(ref, lo, hi):
        return (ref[...].astype(jnp.float32) * silu[:, lo:hi]).astype(jnp.bfloat16)

    out = jnp.dot(gated(a_ref, 0, 512), wa_ref[...], preferred_element_type=jnp.float32)
    out = out + jnp.dot(gated(bo_ref, 512, 768), wb_ref[...], preferred_element_type=jnp.float32)
    out = out + jnp.dot(gated(co_ref, 768, 1024), wc_ref[...], preferred_element_type=jnp.float32)
    y = DEEPNORM_ALPHA * x_ref[...] + out
    mu = jnp.mean(y, axis=1, keepdims=True)
    d = y - mu
    var = jnp.mean(d * d, axis=1, keepdims=True)
    o_ref[...] = d * lax.rsqrt(var + LN_EPS) * g_ref[...] + b_ref[...]


def _out_proj(x2d, b2d, a2d, bo2d, co2d, w_out, ln_g, ln_b, tm=256):
    m = x2d.shape[0]
    row = lambda i: (i, 0)
    fixed = lambda i: (0, 0)
    return pl.pallas_call(
        _out_kernel,
        grid=(m // tm,),
        in_specs=[
            pl.BlockSpec((tm, D_MODEL), row),
            pl.BlockSpec((tm, D_MODEL), row),
            pl.BlockSpec((tm, 512), row),
            pl.BlockSpec((tm, 256), row),
            pl.BlockSpec((tm, 256), row),
            pl.BlockSpec((512, D_MODEL), fixed),
            pl.BlockSpec((256, D_MODEL), lambda i: (2, 0)),
            pl.BlockSpec((256, D_MODEL), lambda i: (3, 0)),
            pl.BlockSpec((1, D_MODEL), fixed),
            pl.BlockSpec((1, D_MODEL), fixed),
        ],
        out_specs=pl.BlockSpec((tm, D_MODEL), row),
        out_shape=jax.ShapeDtypeStruct((m, D_MODEL), jnp.float32),
        compiler_params=_cparams(("parallel",)),
        name="out_proj",
    )(x2d, b2d, a2d, bo2d, co2d, w_out, w_out, w_out, ln_g, ln_b)


def _rope_tables(seq):
    inv = ROPE_THETA ** (-jnp.arange(0, HEAD_DIM, 2, dtype=jnp.float32) / HEAD_DIM)
    ang = jnp.arange(seq, dtype=jnp.int32).astype(jnp.float32)[:, None] * inv[None, :]
    reps = LANES // (HEAD_DIM // 2)
    return jnp.tile(jnp.cos(ang), (1, reps)), jnp.tile(jnp.sin(ang), (1, reps))


def kernel(x, w_in, b_f, lam_q1, lam_k1, lam_q2, lam_k2, g_subln, w_out, ln_g, ln_b):
    bsz, seq, _ = x.shape
    m = bsz * seq
    cos, sin = _rope_tables(seq)

    w_a = w_in[:, :, A_IDX] * A_SCALE
    w_ar = (w_a[:, :, A_ROT_PARTNER] * A_ROT_SIGN).astype(jnp.bfloat16)
    w_a = w_a.astype(jnp.bfloat16)
    w_b = (w_in[:, :, B_IDX] * B_SCALE).astype(jnp.bfloat16)
    w_c = (w_in[:, :, C_IDX] * C_SCALE).astype(jnp.bfloat16)
    w_o = w_out.astype(jnp.bfloat16)
    bf_rows = jnp.zeros((DEPTH, 1, LANES), jnp.float32).at[:, 0, F_LANE0:F_LANE0 + FOX_HEADS].set(b_f)

    x2d = x.reshape(m, D_MODEL)
    for l in range(DEPTH):
        lam_init = 0.8 - 0.6 * math.exp(-0.3 * l)
        lam_rows = jnp.zeros((8, LANES), jnp.float32)
        lam_rows = lam_rows.at[0, :HEAD_DIM].set(lam_q1[l]).at[1, :HEAD_DIM].set(lam_k1[l])
        lam_rows = lam_rows.at[2, :HEAD_DIM].set(lam_q2[l]).at[3, :HEAD_DIM].set(lam_k2[l])
        lam_rows = lam_rows.at[4, :].set(lam_init)

        a2d = _proj_rope(x2d, w_a[l], w_ar[l], cos, sin, seq)
        b2d = _proj_plain(x2d, w_b[l])
        wf2d, aq2d, ak2d = _proj_small(x2d, w_c[l], bf_rows[l], seq)

        a3 = a2d.reshape(bsz, seq, A_WIDTH)
        b3 = b2d.reshape(bsz, seq, B_WIDTH)
        wf3 = wf2d.reshape(bsz, seq, LANES)

        diff_o = _diff_attention(a3, b3, lam_rows, g_subln[l].reshape(1, LANES))
        fox_o = _fox_attention(b3, aq2d.reshape(bsz, seq, LANES), ak2d.reshape(bsz, seq, LANES))
        dsa_o = _dsa_attention(a3, b3, wf3)

        x2d = _out_proj(x2d, b2d, diff_o.reshape(m, 512), dsa_o.reshape(m, 256), fox_o.reshape(m, 256),
                        w_o[l], ln_g[l].reshape(1, D_MODEL), ln_b[l].reshape(1, D_MODEL))
    return x2d.reshape(bsz, seq, D_MODEL)
```

```python
import functools
import math

import numpy as np
import jax
import jax.numpy as jnp
from jax import lax
from jax.experimental import pallas as pl
from jax.experimental.pallas import tpu as pltpu

D_MODEL = 1024
DEPTH = 4
HEAD_DIM = 64
DIFF_HEADS = 4
DSA_HEADS = 4
IDX_HEADS = 8
FOX_HEADS = 4
DSA_TOPK_MAX = 256
ROPE_THETA = 10000.0
LN_EPS = 1e-5
SUBLN_EPS = 1e-5
DEEPNORM_ALPHA = (2 * DEPTH) ** 0.25
IDX_WEIGHT_SCALE = (IDX_HEADS * HEAD_DIM) ** -0.5
LOG2E = math.log2(math.e)
Q_SCALE = HEAD_DIM ** -0.5 * LOG2E

LANES = 128
VMEM_LIMIT_BYTES = 56 * 1024 * 1024

_OFF = {}
_o = 0
for _name, _n in (('diff_q', 512), ('diff_k', 512), ('diff_v', 512), ('dsa_q', 256), ('dsa_k', 64),
                  ('dsa_v', 64), ('idx_q', 512), ('idx_k', 64), ('idx_w', 8), ('fox_q', 256),
                  ('fox_k', 256), ('fox_v', 256), ('fox_f', 4), ('gate', 1024)):
    _OFF[_name] = (_o, _n)
    _o += _n
IN_WIDTH = _o


def _cols(name, scale=1.0, repeat=1):
    o, n = _OFF[name]
    idx = np.tile(np.arange(o, o + n), repeat)
    return idx, np.full(idx.shape, scale, np.float32)


def _pad(n):
    return np.zeros((n,), np.int64), np.zeros((n,), np.float32)


def _layout(parts):
    idx = np.concatenate([p[0] for p in parts])
    scale = np.concatenate([p[1] for p in parts])
    return idx, scale


A_IDX, A_SCALE = _layout([_cols('diff_q', Q_SCALE), _cols('diff_k'), _cols('idx_q'),
                          _cols('dsa_q', Q_SCALE), _cols('idx_k', repeat=2), _cols('dsa_k', repeat=2)])
A_WIDTH = A_IDX.shape[0]
B_IDX, B_SCALE = _layout([_cols('gate'), _cols('diff_v'), _cols('fox_q', Q_SCALE), _cols('fox_k'),
                          _cols('fox_v'), _cols('dsa_v', repeat=2), _pad(128)])
B_WIDTH = B_IDX.shape[0]
C_IDX, C_SCALE = _layout([_cols('idx_w'), _cols('fox_f'), _pad(LANES - 12)])
W_LANE0 = 0
F_LANE0 = 8

_c = np.arange(A_WIDTH)
_first_half = (_c % HEAD_DIM) < HEAD_DIM // 2
A_ROT_PARTNER = np.where(_first_half, _c + HEAD_DIM // 2, _c - HEAD_DIM // 2)
A_ROT_SIGN = np.where(_first_half, -1.0, 1.0).astype(np.float32)

_NT = (((1,), (1,)), ((), ()))
INT_MIN = -2 ** 31
KEY_NEG_INF = -2139095041

_SORT4 = ((0, 1), (2, 3), (0, 2), (1, 3), (1, 2))
_BITONIC12 = ((0, 8), (1, 9), (2, 10), (3, 11), (4, 8), (5, 9), (6, 10), (7, 11),
              (0, 2), (1, 3), (4, 6), (5, 7), (8, 10), (9, 11),
              (0, 1), (2, 3), (4, 5), (6, 7), (8, 9), (10, 11))
DSA_SLOTS = 12
DSA_SCORE_CHUNK = 512


def _cparams(sem):
    return pltpu.CompilerParams(dimension_semantics=sem, vmem_limit_bytes=VMEM_LIMIT_BYTES)


PROJ_COLS = 512


def _proj_rope_kernel(x_ref, w_ref, wr_ref, cos_ref, sin_ref, o_ref):
    xb = x_ref[...].astype(jnp.bfloat16)
    cos = cos_ref[...]
    sin = sin_ref[...]
    for n in range(o_ref.shape[1] // PROJ_COLS):
        cols = slice(n * PROJ_COLS, (n + 1) * PROJ_COLS)
        h = jnp.dot(xb, w_ref[:, cols], preferred_element_type=jnp.float32)
        hr = jnp.dot(xb, wr_ref[:, cols], preferred_element_type=jnp.float32)
        for c in range(PROJ_COLS // LANES):
            sl = slice(c * LANES, (c + 1) * LANES)
            o_ref[:, n * PROJ_COLS + c * LANES:n * PROJ_COLS + (c + 1) * LANES] = (
                h[:, sl] * cos + hr[:, sl] * sin).astype(o_ref.dtype)


def _proj_rope(x2d, w, wr, cos, sin, seq, tm=512):
    m = x2d.shape[0]
    nseq = seq // tm
    return pl.pallas_call(
        _proj_rope_kernel,
        grid=(m // tm,),
        in_specs=[
            pl.BlockSpec((tm, D_MODEL), lambda i: (i, 0)),
            pl.BlockSpec((D_MODEL, A_WIDTH), lambda i: (0, 0)),
            pl.BlockSpec((D_MODEL, A_WIDTH), lambda i: (0, 0)),
            pl.BlockSpec((tm, LANES), lambda i: (i % nseq, 0)),
            pl.BlockSpec((tm, LANES), lambda i: (i % nseq, 0)),
        ],
        out_specs=pl.BlockSpec((tm, A_WIDTH), lambda i: (i, 0)),
        out_shape=jax.ShapeDtypeStruct((m, A_WIDTH), jnp.bfloat16),
        compiler_params=_cparams(("parallel",)),
        name="proj_rope",
    )(x2d, w, wr, cos, sin)


def _proj_plain_kernel(x_ref, w_ref, o_ref):
    xb = x_ref[...].astype(jnp.bfloat16)
    for n in range(o_ref.shape[1] // PROJ_COLS):
        cols = slice(n * PROJ_COLS, (n + 1) * PROJ_COLS)
        o_ref[:, cols] = jnp.dot(xb, w_ref[:, cols], preferred_element_type=jnp.float32).astype(o_ref.dtype)


def _proj_plain(x2d, w, tm=512):
    m = x2d.shape[0]
    return pl.pallas_call(
        _proj_plain_kernel,
        grid=(m // tm,),
        in_specs=[
            pl.BlockSpec((tm, D_MODEL), lambda i: (i, 0)),
            pl.BlockSpec((D_MODEL, B_WIDTH), lambda i: (0, 0)),
        ],
        out_specs=pl.BlockSpec((tm, B_WIDTH), lambda i: (i, 0)),
        out_shape=jax.ShapeDtypeStruct((m, B_WIDTH), jnp.bfloat16),
        compiler_params=_cparams(("parallel",)),
        name="proj_plain",
    )(x2d, w)


def _split3(v):
    hi = v.astype(jnp.bfloat16)
    r = v - hi.astype(jnp.float32)
    mid = r.astype(jnp.bfloat16)
    lo = (r - mid.astype(jnp.float32)).astype(jnp.bfloat16)
    return hi, mid, lo


def _gate_aug_constants():
    place = np.zeros((6, LANES, LANES), np.float32)
    ones = np.zeros((8, LANES), np.float32)
    for h in range(FOX_HEADS):
        for j in range(3):
            place[j, F_LANE0 + h, 8 * h + j] = 1.0
            place[3 + j, F_LANE0 + h, 8 * h + 3 + j] = -1.0
            ones[0, 8 * h + 3 + j] = 1.0
            ones[1, 8 * h + j] = 1.0
    return jnp.asarray(place, jnp.bfloat16), jnp.asarray(ones)


def _proj_small_kernel(x_ref, w_ref, bf_ref, tri_ref, place_ref, ones_ref, o_ref, aq_ref, ak_ref, carry_ref,
                       *, blocks_per_seq):
    i = pl.program_id(0)
    xb = x_ref[...].astype(jnp.bfloat16)
    h = jnp.dot(xb, w_ref[...], preferred_element_type=jnp.float32)
    lane = lax.broadcasted_iota(jnp.int32, h.shape, 1)
    ff = h + bf_ref[...]
    logf = jnp.minimum(ff, 0.0) - jnp.log(1.0 + jnp.exp(-jnp.abs(ff)))
    o = jnp.where(lane < F_LANE0, h * IDX_WEIGHT_SCALE, logf * LOG2E)
    o_ref[...] = o

    @pl.when(i % blocks_per_seq == 0)
    def _():
        carry_ref[...] = jnp.zeros_like(carry_ref)

    tri = tri_ref[...]
    hi, mid, lo = _split3(o)
    cs = (jnp.dot(tri, hi, preferred_element_type=jnp.float32)
          + jnp.dot(tri, mid, preferred_element_type=jnp.float32)
          + jnp.dot(tri, lo, preferred_element_type=jnp.float32))
    cs = cs + carry_ref[0:1, :]
    tm = cs.shape[0]
    carry_ref[...] = jnp.broadcast_to(cs[tm - 1:tm, :], carry_ref.shape)
    parts = _split3(cs)
    aq = ones_ref[0:1, :]
    ak = ones_ref[1:2, :]
    for j in range(3):
        aq = aq + jnp.dot(parts[j], place_ref[j], preferred_element_type=jnp.float32)
        ak = ak + jnp.dot(parts[j], place_ref[3 + j], preferred_element_type=jnp.float32)
    aq_ref[...] = aq.astype(aq_ref.dtype)
    ak_ref[...] = ak.astype(ak_ref.dtype)


def _proj_small(x2d, w, bf_row, seq, tm=512):
    m = x2d.shape[0]
    tri = jnp.asarray(np.tril(np.ones((tm, tm), np.float32)), jnp.bfloat16)
    place, ones = _gate_aug_constants()
    kern = functools.partial(_proj_small_kernel, blocks_per_seq=seq // tm)
    row = lambda i: (i, 0)
    return pl.pallas_call(
        kern,
        grid=(m // tm,),
        in_specs=[
            pl.BlockSpec((tm, D_MODEL), row),
            pl.BlockSpec((D_MODEL, LANES), lambda i: (0, 0)),
            pl.BlockSpec((1, LANES), lambda i: (0, 0)),
            pl.BlockSpec((tm, tm), lambda i: (0, 0)),
            pl.BlockSpec((6, LANES, LANES), lambda i: (0, 0, 0)),
            pl.BlockSpec((8, LANES), lambda i: (0, 0)),
        ],
        out_specs=[pl.BlockSpec((tm, LANES), row), pl.BlockSpec((tm, LANES), row), pl.BlockSpec((tm, LANES), row)],
        out_shape=[jax.ShapeDtypeStruct((m, LANES), jnp.float32),
                   jax.ShapeDtypeStruct((m, LANES), jnp.bfloat16),
                   jax.ShapeDtypeStruct((m, LANES), jnp.bfloat16)],
        scratch_shapes=[pltpu.VMEM((8, LANES), jnp.float32)],
        compiler_params=_cparams(("arbitrary",)),
        name="proj_small",
    )(x2d, w, bf_row, tri, place, ones)


def _flash_init(m_ref, acc_ref):
    m_ref[...] = jnp.full(m_ref.shape, -1e30, jnp.float32)
    acc_ref[...] = jnp.zeros(acc_ref.shape, jnp.float32)


def _flash_rows(s, v1, m_ref, acc_ref, rows):
    tk = s.shape[1]
    m_prev = m_ref[rows, :]
    m_next = jnp.maximum(m_prev, jnp.max(s, axis=1, keepdims=True))
    p = jnp.exp2((s - pltpu.repeat(m_next, tk // LANES, axis=1)).astype(jnp.bfloat16))
    alpha = jnp.exp2(m_prev - m_next)
    m_ref[rows, :] = m_next
    pv = jnp.dot(p, v1, preferred_element_type=jnp.float32)
    acc_ref[rows, :] = acc_ref[rows, :] * pltpu.repeat(alpha, v1.shape[1] // LANES, axis=1) + pv


def _pipelined_chunks(n_full, produce, consume, buf0, buf1):
    produce(0, buf0)

    def body(p, carry):
        produce(2 * p + 1, buf1)
        consume(2 * p, buf0, False)
        produce(2 * p + 2, buf0)
        consume(2 * p + 1, buf1, False)
        return carry

    lax.fori_loop(0, n_full // 2, body, 0)

    @pl.when(n_full % 2 == 1)
    def _():
        produce(n_full, buf1)
        consume(n_full - 1, buf0, False)
        consume(n_full, buf1, True)

    @pl.when(n_full % 2 == 0)
    def _():
        consume(n_full, buf0, True)


def _stack_pair(q, qs_ref, t):
    lane = lax.broadcasted_iota(jnp.int32, q.shape, 1)
    zero = jnp.zeros_like(q)
    qs_ref[0:t, :] = jnp.where(lane < HEAD_DIM, q, zero)
    qs_ref[t:2 * t, :] = jnp.where(lane >= HEAD_DIM, q, zero)


def _with_ones(v):
    return jnp.concatenate([v, jnp.ones((v.shape[0], LANES), v.dtype)], axis=1)


def _diff_kernel(lam_ref, g_ref, q_ref, k_ref, v_ref, o_ref, qs_ref, m_ref, acc_ref, s0_ref, s1_ref,
                 *, tq, tk):
    i = pl.program_id(2)
    _stack_pair(q_ref[0], qs_ref, tq)
    _flash_init(m_ref, acc_ref)
    groups = (slice(0, tq), slice(tq, 2 * tq))

    def produce(j, s_ref):
        start = pl.multiple_of(j * tk, tk)
        k = k_ref[0, pl.ds(start, tk), :]
        for rows in groups:
            s_ref[rows, :] = lax.dot_general(qs_ref[rows, :], k, _NT, preferred_element_type=jnp.float32)

    def consume(j, s_ref, last):
        start = pl.multiple_of(j * tk, tk)
        v1 = _with_ones(v_ref[0, pl.ds(start, tk), :])
        for rows in groups:
            s = s_ref[rows, :]
            if last:
                r = lax.broadcasted_iota(jnp.int32, (tq, tk), 0)
                c = lax.broadcasted_iota(jnp.int32, (tq, tk), 1)
                s = jnp.where(start + c <= i * tq + r, s, -jnp.inf)
            _flash_rows(s, v1, m_ref, acc_ref, rows)

    _pipelined_chunks((i * tq) // tk, produce, consume, s0_ref, s1_ref)

    lam_rows = lam_ref[...]
    s1 = jnp.sum(lam_rows[0:1, :] * lam_rows[1:2, :], axis=1, keepdims=True)
    s2 = jnp.sum(lam_rows[2:3, :] * lam_rows[3:4, :], axis=1, keepdims=True)
    lam_init = lam_rows[4:5, 0:1]
    lam = jnp.exp(s1) - jnp.exp(s2) + lam_init
    o0 = acc_ref[0:tq, 0:LANES] / acc_ref[0:tq, LANES:2 * LANES]
    o1 = acc_ref[tq:2 * tq, 0:LANES] / acc_ref[tq:2 * tq, LANES:2 * LANES]
    a = o0 - lam * o1
    ms = jnp.mean(a * a, axis=1, keepdims=True)
    a = a * lax.rsqrt(ms + SUBLN_EPS) * g_ref[...] * (1.0 - lam_init)
    o_ref[0] = a.astype(o_ref.dtype)


def _diff_attention(a3, b3, lam_rows, g_row, tq=512, tk=1024):
    bsz, seq, _ = a3.shape
    tk = min(tk, seq)
    kern = functools.partial(_diff_kernel, tq=tq, tk=tk)
    return pl.pallas_call(
        kern,
        grid=(bsz, DIFF_HEADS, seq // tq),
        in_specs=[
            pl.BlockSpec((8, LANES), lambda b, h, i: (0, 0)),
            pl.BlockSpec((1, LANES), lambda b, h, i: (0, 0)),
            pl.BlockSpec((1, tq, LANES), lambda b, h, i: (b, i, h)),
            pl.BlockSpec((1, seq, LANES), lambda b, h, i: (b, 0, 4 + h)),
            pl.BlockSpec((1, seq, LANES), lambda b, h, i: (b, 0, 8 + h)),
        ],
        out_specs=pl.BlockSpec((1, tq, LANES), lambda b, h, i: (b, i, h)),
        out_shape=jax.ShapeDtypeStruct((bsz, seq, DIFF_HEADS * LANES), jnp.bfloat16),
        scratch_shapes=[pltpu.VMEM((2 * tq, LANES), jnp.bfloat16),
                        pltpu.VMEM((2 * tq, LANES), jnp.float32),
                        pltpu.VMEM((2 * tq, 2 * LANES), jnp.float32),
                        pltpu.VMEM((2 * tq, tk), jnp.float32),
                        pltpu.VMEM((2 * tq, tk), jnp.float32)],
        compiler_params=_cparams(("parallel", "parallel", "arbitrary")),
        name="diff_attention",
    )(lam_rows, g_row, a3, a3, b3)


def _fox_kernel(q_ref, k_ref, v_ref, aq_ref, ak_ref, o_ref, qs_ref, m_ref, acc_ref, s0_ref, s1_ref, *, tq, tk):
    g = pl.program_id(1)
    i = pl.program_id(2)
    q = q_ref[0]
    aq = aq_ref[0]
    lane = lax.broadcasted_iota(jnp.int32, (tq, LANES), 1)
    zero = jnp.zeros_like(q)
    for a in range(2):
        rows = slice(a * tq, (a + 1) * tq)
        head_lanes = (lane < HEAD_DIM) if a == 0 else (lane >= HEAD_DIM)
        first = 8 * (2 * g + a)
        qs_ref[rows, 0:LANES] = jnp.where(head_lanes, q, zero)
        qs_ref[rows, LANES:2 * LANES] = jnp.where((lane >= first) & (lane < first + 6), aq, zero)
    _flash_init(m_ref, acc_ref)
    groups = (slice(0, tq), slice(tq, 2 * tq))

    def produce(j, s_ref):
        start = pl.multiple_of(j * tk, tk)
        k2 = jnp.concatenate([k_ref[0, pl.ds(start, tk), :], ak_ref[0, pl.ds(start, tk), :]], axis=1)
        for rows in groups:
            s_ref[rows, :] = lax.dot_general(qs_ref[rows, :], k2, _NT, preferred_element_type=jnp.float32)

    def consume(j, s_ref, last):
        start = pl.multiple_of(j * tk, tk)
        v1 = _with_ones(v_ref[0, pl.ds(start, tk), :])
        for rows in groups:
            s = s_ref[rows, :]
            if last:
                r = lax.broadcasted_iota(jnp.int32, (tq, tk), 0)
                c = lax.broadcasted_iota(jnp.int32, (tq, tk), 1)
                s = jnp.where(start + c <= i * tq + r, s, -jnp.inf)
            _flash_rows(s, v1, m_ref, acc_ref, rows)

    _pipelined_chunks((i * tq) // tk, produce, consume, s0_ref, s1_ref)

    o0 = acc_ref[0:tq, 0:LANES] / acc_ref[0:tq, LANES:2 * LANES]
    o1 = acc_ref[tq:2 * tq, 0:LANES] / acc_ref[tq:2 * tq, LANES:2 * LANES]
    o_ref[0] = jnp.where(lane < HEAD_DIM, o0, o1).astype(o_ref.dtype)


def _fox_attention(b3, aq3, ak3, tq=512, tk=1024):
    bsz, seq, _ = b3.shape
    tk = min(tk, seq)
    kern = functools.partial(_fox_kernel, tq=tq, tk=tk)
    return pl.pallas_call(
        kern,
        grid=(bsz, FOX_HEADS // 2, seq // tq),
        in_specs=[
            pl.BlockSpec((1, tq, LANES), lambda b, g, i: (b, i, 12 + g)),
            pl.BlockSpec((1, seq, LANES), lambda b, g, i: (b, 0, 14 + g)),
            pl.BlockSpec((1, seq, LANES), lambda b, g, i: (b, 0, 16 + g)),
            pl.BlockSpec((1, tq, LANES), lambda b, g, i: (b, i, 0)),
            pl.BlockSpec((1, seq, LANES), lambda b, g, i: (b, 0, 0)),
        ],
        out_specs=pl.BlockSpec((1, tq, LANES), lambda b, g, i: (b, i, g)),
        out_shape=jax.ShapeDtypeStruct((bsz, seq, FOX_HEADS * HEAD_DIM), jnp.bfloat16),
        scratch_shapes=[pltpu.VMEM((2 * tq, 2 * LANES), jnp.bfloat16),
                        pltpu.VMEM((2 * tq, LANES), jnp.float32),
                        pltpu.VMEM((2 * tq, 2 * LANES), jnp.float32),
                        pltpu.VMEM((2 * tq, tk), jnp.float32),
                        pltpu.VMEM((2 * tq, tk), jnp.float32)],
        compiler_params=_cparams(("parallel", "parallel", "arbitrary")),
        name="fox_attention",
    )(b3, b3, b3, aq3, ak3)


def _score_key(score):
    bits = pltpu.bitcast(score, jnp.int32)
    return bits ^ ((bits >> 31) & 0x7FFFFFFF)


def _dsa_kernel(iq_ref, sq_ref, w_ref, ik_ref, sk_ref, v_ref, o_ref,
                keys_ref, cand_ref, candk_ref, qi_ref, qd_ref, wb_ref, m_ref, acc_ref, s0_ref, s1_ref,
                a0_ref, a1_ref, thr_ref, cnt_ref, flag_ref, *, tq, tc, ta, topk, idx_bits, slots):
    i = pl.program_id(1)
    lane = lax.broadcasted_iota(jnp.int32, (tq, LANES), 1)
    low = lane < HEAD_DIM

    for h in range(IDX_HEADS):
        blk = iq_ref[0, :, (h // 2) * LANES:(h // 2 + 1) * LANES]
        qi_ref[h * tq:(h + 1) * tq, :] = jnp.where(low if h % 2 == 0 else ~low, blk, jnp.zeros_like(blk))
    dsa_order = (0, 2, 1, 3)
    for slab, h in enumerate(dsa_order):
        blk = sq_ref[0, :, (h // 2) * LANES:(h // 2 + 1) * LANES]
        qd_ref[slab * tq:(slab + 1) * tq, :] = jnp.where(low if h % 2 == 0 else ~low, blk, jnp.zeros_like(blk))
    wt = w_ref[0]
    for h in range(IDX_HEADS):
        col = jnp.sum(jnp.where(lane == W_LANE0 + h, wt, 0.0), axis=1, keepdims=True)
        wb_ref[h] = jnp.broadcast_to(col, (tq, LANES))

    nlast = (i * tq) // tc
    nch = nlast + 1
    cand_ref[...] = jnp.full(cand_ref.shape, -jnp.inf, jnp.float32)
    row_pos = i * tq + lax.broadcasted_iota(jnp.int32, (tq, tc), 0)
    col_iota = lax.broadcasted_iota(jnp.int32, (tq, tc), 1)
    idx_groups = (slice(0, 4 * tq), slice(4 * tq, 8 * tq))
    dsa_groups = (slice(0, 2 * tq), slice(2 * tq, 4 * tq))

    def rep(x):
        return pltpu.repeat(x, tc // LANES, axis=1)

    def idx_produce(j, s_ref):
        start = pl.multiple_of(j * tc, tc)
        ik = ik_ref[0, pl.ds(start, tc), :]
        for rows in idx_groups:
            s_ref[rows, :] = lax.dot_general(qi_ref[rows, :], ik, _NT, preferred_element_type=jnp.float32)

    def idx_consume(j, s_ref, last):
        start = pl.multiple_of(j * tc, tc)
        score = jnp.zeros((tq, tc), jnp.float32)
        for h in range(IDX_HEADS):
            score = score + jnp.maximum(s_ref[h * tq:(h + 1) * tq, :], 0.0) * rep(wb_ref[h])
        if last:
            score = jnp.where(start + col_iota <= row_pos, score, -jnp.inf)
        keys_ref[:, pl.ds(start, tc)] = _score_key(score)
        for rg in range(tq // 8):
            rs = slice(rg * 8, (rg + 1) * 8)
            xs = [score[rs, u * LANES:(u + 1) * LANES] for u in range(4)]
            for p, q in _SORT4:
                xs[p], xs[q] = jnp.maximum(xs[p], xs[q]), jnp.minimum(xs[p], xs[q])
            c = [cand_ref[rs, sl * LANES:(sl + 1) * LANES] for sl in range(12)]
            for t in range(4):
                c[8 + t] = jnp.maximum(c[8 + t], xs[3 - t])
            for p, q in _BITONIC12:
                c[p], c[q] = jnp.maximum(c[p], c[q]), jnp.minimum(c[p], c[q])
            for sl in range(12):
                cand_ref[rs, sl * LANES:(sl + 1) * LANES] = c[sl]

    _pipelined_chunks(nlast, idx_produce, idx_consume, s0_ref, s1_ref)

    def count(pred):
        def body(c, acc):
            start = pl.multiple_of(c * tc, tc)
            hit = pred(keys_ref[:, pl.ds(start, tc)], start).astype(jnp.int32)
            for u in range(tc // LANES):
                acc = acc + hit[:, u * LANES:(u + 1) * LANES]
            return acc
        part = lax.fori_loop(0, nch, body, jnp.zeros((tq, LANES), jnp.int32))
        return jnp.broadcast_to(jnp.sum(part, axis=1, keepdims=True), (tq, LANES))

    n_row = i * tq + lax.broadcasted_iota(jnp.int32, (tq, LANES), 0) + 1

    def search_all_keys():
        def pending(state):
            bi, thr, cnt_thr = state
            return jnp.logical_and(bi < 32, jnp.max(jnp.where(n_row > topk, cnt_thr, topk)) > topk)

        def bit_step(state):
            bi, thr, cnt_thr = state
            cand = thr + lax.shift_left(jnp.int32(1), 31 - bi)
            cnt = count(lambda kc, start: kc >= rep(cand))
            ok = cnt >= topk
            return bi + 1, jnp.where(ok, cand, thr), jnp.where(ok, cnt, cnt_thr)

        total = jnp.zeros((tq, LANES), jnp.int32) + nch * tc
        init = (jnp.int32(0), jnp.full((tq, LANES), INT_MIN, jnp.int32), total)
        _, thr, cnt_thr = lax.while_loop(pending, bit_step, init)
        thr_ref[...] = thr
        cnt_ref[...] = cnt_thr

    def search_candidates():
        def bit_step(bi, state):
            thr, cnt_thr = state
            cand = thr + lax.shift_left(jnp.int32(1), 31 - bi)
            part = jnp.zeros((tq, LANES), jnp.int32)
            for sl in range(slots):
                part = jnp.where(candk_ref[:, sl * LANES:(sl + 1) * LANES] >= cand, sl + 1, part)
            cnt = jnp.broadcast_to(jnp.sum(part, axis=1, keepdims=True), (tq, LANES))
            ok = cnt >= topk
            return jnp.where(ok, cand, thr), jnp.where(ok, cnt, cnt_thr)

        init = (jnp.full((tq, LANES), INT_MIN, jnp.int32), jnp.full((tq, LANES), slots * LANES, jnp.int32))
        thr, cnt_thr = lax.fori_loop(0, 32, bit_step, init, unroll=4)
        thr_ref[...] = thr
        cnt_ref[...] = cnt_thr

    cand_chunks = slots * LANES // tc
    flag_ref[0] = 1

    @pl.when(nch > cand_chunks)
    def _():
        candk_ref[...] = _score_key(cand_ref[...])
        search_candidates()
        kept_min = jnp.max(candk_ref[:, (slots - 1) * LANES:slots * LANES], axis=1, keepdims=True)
        unsafe = jnp.where(kept_min >= thr_ref[...], 1, 0)
        flag_ref[0] = jnp.max(unsafe)

    @pl.when(flag_ref[0] > 0)
    def _():
        search_all_keys()

    thr = thr_ref[...]
    cnt_thr = cnt_ref[...]

    excess = jnp.where(thr > KEY_NEG_INF, cnt_thr - topk, 0)

    @pl.when(jnp.max(excess) > 0)
    def _():
        c_gt = count(lambda kc, start: kc > rep(thr))
        keep = topk - c_gt

        def idx_step(bi, x):
            cand = x + lax.shift_left(jnp.int32(1), idx_bits - 1 - bi)
            below = count(lambda kc, start: (kc == rep(thr)) & (start + col_iota < rep(cand)))
            return jnp.where(below < keep, cand, x)

        last = lax.fori_loop(0, idx_bits, idx_step, jnp.zeros((tq, LANES), jnp.int32))

        def demote(c, carry):
            start = pl.multiple_of(c * tc, tc)
            kc = keys_ref[:, pl.ds(start, tc)]
            drop = (rep(excess) > 0) & (kc == rep(thr)) & (start + col_iota > rep(last))
            keys_ref[:, pl.ds(start, tc)] = jnp.where(drop, kc - 1, kc)
            return carry

        lax.fori_loop(0, nch, demote, 0)

    sel_thr = jnp.maximum(thr, KEY_NEG_INF + 1)

    _flash_init(m_ref, acc_ref)
    vlow = lax.broadcasted_iota(jnp.int32, (ta, LANES), 1) < HEAD_DIM
    alast = (i * tq) // ta

    @pl.when((alast + 1) * ta > nch * tc)
    def _():
        keys_ref[:, pl.ds(pl.multiple_of(nch * tc, tc), tc)] = jnp.full((tq, tc), KEY_NEG_INF, jnp.int32)

    def att_produce(j, s_ref):
        start = pl.multiple_of(j * ta, ta)
        sk = sk_ref[0, pl.ds(start, ta), :]
        for rows in dsa_groups:
            s_ref[rows, :] = lax.dot_general(qd_ref[rows, :], sk, _NT, preferred_element_type=jnp.float32)

    def att_consume(j, s_ref, last):
        start = pl.multiple_of(j * ta, ta)
        v2 = v_ref[0, pl.ds(start, ta), :]
        one = jnp.ones_like(v2)
        v_even = jnp.where(vlow, v2, one)
        v_odd = jnp.where(vlow, one, v2)
        sel = keys_ref[:, pl.ds(start, ta)] >= pltpu.repeat(sel_thr, ta // LANES, axis=1)
        for rows, v1 in zip(dsa_groups, (v_even, v_odd)):
            s = s_ref[rows, :].reshape(2, tq, ta)
            s = jnp.where(sel[None], s, -jnp.inf).reshape(2 * tq, ta)
            _flash_rows(s, v1, m_ref, acc_ref, rows)

    _pipelined_chunks(alast, att_produce, att_consume, a0_ref, a1_ref)

    outs = {}
    for slab, h in enumerate(dsa_order):
        acc = acc_ref[slab * tq:(slab + 1) * tq, :]
        outs[h] = acc / pltpu.roll(acc, HEAD_DIM, axis=1)
    for p in range(DSA_HEADS // 2):
        o_ref[0, :, p * LANES:(p + 1) * LANES] = jnp.where(low, outs[2 * p], outs[2 * p + 1]).astype(o_ref.dtype)


def _dsa_attention(a3, b3, w3, tq=128):
    bsz, seq, _ = a3.shape
    tc, slots = DSA_SCORE_CHUNK, DSA_SLOTS
    ta = 2 * tc
    topk = min(DSA_TOPK_MAX, seq // 4)
    idx_bits = int(math.log2(seq))
    assert 2 ** idx_bits == seq and tc % tq == 0 and (slots * LANES) % tc == 0 and seq % ta == 0
    kern = functools.partial(_dsa_kernel, tq=tq, tc=tc, ta=ta, topk=topk, idx_bits=idx_bits, slots=slots)
    return pl.pallas_call(
        kern,
        grid=(bsz, seq // tq),
        in_specs=[
            pl.BlockSpec((1, tq, 4 * LANES), lambda b, i: (b, i, 2)),
            pl.BlockSpec((1, tq, 2 * LANES), lambda b, i: (b, i, 6)),
            pl.BlockSpec((1, tq, LANES), lambda b, i: (b, i, 0)),
            pl.BlockSpec((1, seq, LANES), lambda b, i: (b, 0, 14)),
            pl.BlockSpec((1, seq, LANES), lambda b, i: (b, 0, 15)),
            pl.BlockSpec((1, seq, LANES), lambda b, i: (b, 0, 18)),
        ],
        out_specs=pl.BlockSpec((1, tq, 2 * LANES), lambda b, i: (b, i, 0)),
        out_shape=jax.ShapeDtypeStruct((bsz, seq, DSA_HEADS * HEAD_DIM), jnp.bfloat16),
        scratch_shapes=[pltpu.VMEM((tq, seq), jnp.int32),
                        pltpu.VMEM((tq, slots * LANES), jnp.float32),
                        pltpu.VMEM((tq, slots * LANES), jnp.int32),
                        pltpu.VMEM((IDX_HEADS * tq, LANES), jnp.bfloat16),
                        pltpu.VMEM((DSA_HEADS * tq, LANES), jnp.bfloat16),
                        pltpu.VMEM((IDX_HEADS, tq, LANES), jnp.float32),
                        pltpu.VMEM((DSA_HEADS * tq, LANES), jnp.float32),
                        pltpu.VMEM((DSA_HEADS * tq, LANES), jnp.float32),
                        pltpu.VMEM((IDX_HEADS * tq, tc), jnp.float32),
                        pltpu.VMEM((IDX_HEADS * tq, tc), jnp.float32),
                        pltpu.VMEM((DSA_HEADS * tq, ta), jnp.float32),
                        pltpu.VMEM((DSA_HEADS * tq, ta), jnp.float32),
                        pltpu.VMEM((tq, LANES), jnp.int32),
                        pltpu.VMEM((tq, LANES), jnp.int32),
                        pltpu.SMEM((1,), jnp.int32)],
        compiler_params=_cparams(("parallel", "arbitrary")),
        name="dsa_attention",
    )(a3, a3, w3, a3, a3, b3)


def _out_kernel(x_ref, gate_ref, a_ref, bo_ref, co_ref, wa_ref, wb_ref, wc_ref, g_ref, b_ref, o_ref):
    gate = gate_ref[...].astype(jnp.float32)
    silu = gate / (1.0 + jnp.exp(-gate))

    def gated(ref, lo, hi):
        return (ref[...].astype(jnp.float32) * silu[:, lo:hi]).astype(jnp.bfloat16)

    out = jnp.dot(gated(a_ref, 0, 512), wa_ref[...], preferred_element_type=jnp.float32)
    out = out + jnp.dot(gated(bo_ref, 512, 768), wb_ref[...], preferred_element_type=jnp.float32)
    out = out + jnp.dot(gated(co_ref, 768, 1024), wc_ref[...], preferred_element_type=jnp.float32)
    y = DEEPNORM_ALPHA * x_ref[...] + out
    mu = jnp.mean(y, axis=1, keepdims=True)
    d = y - mu
    var = jnp.mean(d * d, axis=1, keepdims=True)
    o_ref[...] = d * lax.rsqrt(var + LN_EPS) * g_ref[...] + b_ref[...]


def _out_proj(x2d, b2d, a2d, bo2d, co2d, w_out, ln_g, ln_b, tm=256):
    m = x2d.shape[0]
    row = lambda i: (i, 0)
    fixed = lambda i: (0, 0)
    return pl.pallas_call(
        _out_kernel,
        grid=(m // tm,),
        in_specs=[
            pl.BlockSpec((tm, D_MODEL), row),
            pl.BlockSpec((tm, D_MODEL), row),
            pl.BlockSpec((tm, 512), row),
            pl.BlockSpec((tm, 256), row),
            pl.BlockSpec((tm, 256), row),
            pl.BlockSpec((512, D_MODEL), fixed),
            pl.BlockSpec((256, D_MODEL), lambda i: (2, 0)),
            pl.BlockSpec((256, D_MODEL), lambda i: (3, 0)),
            pl.BlockSpec((1, D_MODEL), fixed),
            pl.BlockSpec((1, D_MODEL), fixed),
        ],
        out_specs=pl.BlockSpec((tm, D_MODEL), row),
        out_shape=jax.ShapeDtypeStruct((m, D_MODEL), jnp.float32),
        compiler_params=_cparams(("parallel",)),
        name="out_proj",
    )(x2d, b2d, a2d, bo2d, co2d, w_out, w_out, w_out, ln_g, ln_b)


def _rope_tables(seq):
    inv = ROPE_THETA ** (-jnp.arange(0, HEAD_DIM, 2, dtype=jnp.float32) / HEAD_DIM)
    ang = jnp.arange(seq, dtype=jnp.int32).astype(jnp.float32)[:, None] * inv[None, :]
    reps = LANES // (HEAD_DIM // 2)
    return jnp.tile(jnp.cos(ang), (1, reps)), jnp.tile(jnp.sin(ang), (1, reps))


def kernel(x, w_in, b_f, lam_q1, lam_k1, lam_q2, lam_k2, g_subln, w_out, ln_g, ln_b):
    bsz, seq, _ = x.shape
    m = bsz * seq
    cos, sin = _rope_tables(seq)

    w_a = w_in[:, :, A_IDX] * A_SCALE
    w_ar = (w_a[:, :, A_ROT_PARTNER] * A_ROT_SIGN).astype(jnp.bfloat16)
    w_a = w_a.astype(jnp.bfloat16)
    w_b = (w_in[:, :, B_IDX] * B_SCALE).astype(jnp.bfloat16)
    w_c = (w_in[:, :, C_IDX] * C_SCALE).astype(jnp.bfloat16)
    w_o = w_out.astype(jnp.bfloat16)
    bf_rows = jnp.zeros((DEPTH, 1, LANES), jnp.float32).at[:, 0, F_LANE0:F_LANE0 + FOX_HEADS].set(b_f)

    x2d = x.reshape(m, D_MODEL)
    for l in range(DEPTH):
        lam_init = 0.8 - 0.6 * math.exp(-0.3 * l)
        lam_rows = jnp.zeros((8, LANES), jnp.float32)
        lam_rows = lam_rows.at[0, :HEAD_DIM].set(lam_q1[l]).at[1, :HEAD_DIM].set(lam_k1[l])
        lam_rows = lam_rows.at[2, :HEAD_DIM].set(lam_q2[l]).at[3, :HEAD_DIM].set(lam_k2[l])
        lam_rows = lam_rows.at[4, :].set(lam_init)

        a2d = _proj_rope(x2d, w_a[l], w_ar[l], cos, sin, seq)
        b2d = _proj_plain(x2d, w_b[l])
        wf2d, aq2d, ak2d = _proj_small(x2d, w_c[l], bf_rows[l], seq)

        a3 = a2d.reshape(bsz, seq, A_WIDTH)
        b3 = b2d.reshape(bsz, seq, B_WIDTH)
        wf3 = wf2d.reshape(bsz, seq, LANES)

        diff_o = _diff_attention(a3, b3, lam_rows, g_subln[l].reshape(1, LANES))
        fox_o = _fox_attention(b3, aq2d.reshape(bsz, seq, LANES), ak2d.reshape(bsz, seq, LANES))
        dsa_o = _dsa_attention(a3, b3, wf3)

        x2d = _out_proj(x2d, b2d, diff_o.reshape(m, 512), dsa_o.reshape(m, 256), fox_o.reshape(m, 256),
                        w_o[l], ln_g[l].reshape(1, D_MODEL), ln_b[l].reshape(1, D_MODEL))
    return x2d.reshape(bsz, seq, D_MODEL)
```

```python
import functools
import math

import numpy as np
import jax
import jax.numpy as jnp
from jax import lax
from jax.experimental import pallas as pl
from jax.experimental.pallas import tpu as pltpu

D_MODEL = 1024
DEPTH = 4
HEAD_DIM = 64
DIFF_HEADS = 4
DSA_HEADS = 4
IDX_HEADS = 8
FOX_HEADS = 4
DSA_TOPK_MAX = 256
ROPE_THETA = 10000.0
LN_EPS = 1e-5
SUBLN_EPS = 1e-5
DEEPNORM_ALPHA = (2 * DEPTH) ** 0.25
IDX_WEIGHT_SCALE = (IDX_HEADS * HEAD_DIM) ** -0.5
LOG2E = math.log2(math.e)
Q_SCALE = HEAD_DIM ** -0.5 * LOG2E

LANES = 128
VMEM_LIMIT_BYTES = 56 * 1024 * 1024

_OFF = {}
_o = 0
for _name, _n in (('diff_q', 512), ('diff_k', 512), ('diff_v', 512), ('dsa_q', 256), ('dsa_k', 64),
                  ('dsa_v', 64), ('idx_q', 512), ('idx_k', 64), ('idx_w', 8), ('fox_q', 256),
                  ('fox_k', 256), ('fox_v', 256), ('fox_f', 4), ('gate', 1024)):
    _OFF[_name] = (_o, _n)
    _o += _n
IN_WIDTH = _o


def _cols(name, scale=1.0, repeat=1):
    o, n = _OFF[name]
    idx = np.tile(np.arange(o, o + n), repeat)
    return idx, np.full(idx.shape, scale, np.float32)


def _pad(n):
    return np.zeros((n,), np.int64), np.zeros((n,), np.float32)


def _layout(parts):
    idx = np.concatenate([p[0] for p in parts])
    scale = np.concatenate([p[1] for p in parts])
    return idx, scale


A_IDX, A_SCALE = _layout([_cols('diff_q', Q_SCALE), _cols('diff_k'), _cols('idx_q'),
                          _cols('dsa_q', Q_SCALE), _cols('idx_k', repeat=2), _cols('dsa_k', repeat=2)])
A_WIDTH = A_IDX.shape[0]
B_IDX, B_SCALE = _layout([_cols('gate'), _cols('diff_v'), _cols('fox_q', Q_SCALE), _cols('fox_k'),
                          _cols('fox_v'), _cols('dsa_v', repeat=2), _pad(128)])
B_WIDTH = B_IDX.shape[0]
C_IDX, C_SCALE = _layout([_cols('idx_w'), _cols('fox_f'), _pad(LANES - 12)])
W_LANE0 = 0
F_LANE0 = 8

_c = np.arange(A_WIDTH)
_first_half = (_c % HEAD_DIM) < HEAD_DIM // 2
A_ROT_PARTNER = np.where(_first_half, _c + HEAD_DIM // 2, _c - HEAD_DIM // 2)
A_ROT_SIGN = np.where(_first_half, -1.0, 1.0).astype(np.float32)

_NT = (((1,), (1,)), ((), ()))
INT_MIN = -2 ** 31
KEY_NEG_INF = -2139095041

_SORT4 = ((0, 1), (2, 3), (0, 2), (1, 3), (1, 2))
_BITONIC12 = ((0, 8), (1, 9), (2, 10), (3, 11), (4, 8), (5, 9), (6, 10), (7, 11),
              (0, 2), (1, 3), (4, 6), (5, 7), (8, 10), (9, 11),
              (0, 1), (2, 3), (4, 5), (6, 7), (8, 9), (10, 11))
DSA_SLOTS = 12
DSA_SCORE_CHUNK = 512


def _cparams(sem):
    return pltpu.CompilerParams(dimension_semantics=sem, vmem_limit_bytes=VMEM_LIMIT_BYTES)


PROJ_COLS = 512


def _proj_rope_kernel(x_ref, w_ref, wr_ref, cos_ref, sin_ref, o_ref):
    xb = x_ref[...].astype(jnp.bfloat16)
    cos = cos_ref[...]
    sin = sin_ref[...]
    for n in range(o_ref.shape[1] // PROJ_COLS):
        cols = slice(n * PROJ_COLS, (n + 1) * PROJ_COLS)
        h = jnp.dot(xb, w_ref[:, cols], preferred_element_type=jnp.float32)
        hr = jnp.dot(xb, wr_ref[:, cols], preferred_element_type=jnp.float32)
        for c in range(PROJ_COLS // LANES):
            sl = slice(c * LANES, (c + 1) * LANES)
            o_ref[:, n * PROJ_COLS + c * LANES:n * PROJ_COLS + (c + 1) * LANES] = (
                h[:, sl] * cos + hr[:, sl] * sin).astype(o_ref.dtype)


def _proj_rope(x2d, w, wr, cos, sin, seq, tm=512):
    m = x2d.shape[0]
    nseq = seq // tm
    return pl.pallas_call(
        _proj_rope_kernel,
        grid=(m // tm,),
        in_specs=[
            pl.BlockSpec((tm, D_MODEL), lambda i: (i, 0)),
            pl.BlockSpec((D_MODEL, A_WIDTH), lambda i: (0, 0)),
            pl.BlockSpec((D_MODEL, A_WIDTH), lambda i: (0, 0)),
            pl.BlockSpec((tm, LANES), lambda i: (i % nseq, 0)),
            pl.BlockSpec((tm, LANES), lambda i: (i % nseq, 0)),
        ],
        out_specs=pl.BlockSpec((tm, A_WIDTH), lambda i: (i, 0)),
        out_shape=jax.ShapeDtypeStruct((m, A_WIDTH), jnp.bfloat16),
        compiler_params=_cparams(("parallel",)),
        name="proj_rope",
    )(x2d, w, wr, cos, sin)


def _proj_plain_kernel(x_ref, w_ref, o_ref):
    xb = x_ref[...].astype(jnp.bfloat16)
    for n in range(o_ref.shape[1] // PROJ_COLS):
        cols = slice(n * PROJ_COLS, (n + 1) * PROJ_COLS)
        o_ref[:, cols] = jnp.dot(xb, w_ref[:, cols], preferred_element_type=jnp.float32).astype(o_ref.dtype)


def _proj_plain(x2d, w, tm=512):
    m = x2d.shape[0]
    return pl.pallas_call(
        _proj_plain_kernel,
        grid=(m // tm,),
        in_specs=[
            pl.BlockSpec((tm, D_MODEL), lambda i: (i, 0)),
            pl.BlockSpec((D_MODEL, B_WIDTH), lambda i: (0, 0)),
        ],
        out_specs=pl.BlockSpec((tm, B_WIDTH), lambda i: (i, 0)),
        out_shape=jax.ShapeDtypeStruct((m, B_WIDTH), jnp.bfloat16),
        compiler_params=_cparams(("parallel",)),
        name="proj_plain",
    )(x2d, w)


def _split3(v):
    hi = v.astype(jnp.bfloat16)
    r = v - hi.astype(jnp.float32)
    mid = r.astype(jnp.bfloat16)
    lo = (r - mid.astype(jnp.float32)).astype(jnp.bfloat16)
    return hi, mid, lo


def _gate_aug_constants():
    place = np.zeros((6, LANES, LANES), np.float32)
    ones = np.zeros((8, LANES), np.float32)
    for h in range(FOX_HEADS):
        for j in range(3):
            place[j, F_LANE0 + h, 8 * h + j] = 1.0
            place[3 + j, F_LANE0 + h, 8 * h + 3 + j] = -1.0
            ones[0, 8 * h + 3 + j] = 1.0
            ones[1, 8 * h + j] = 1.0
    return jnp.asarray(place, jnp.bfloat16), jnp.asarray(ones)


def _proj_small_kernel(x_ref, w_ref, bf_ref, tri_ref, place_ref, ones_ref, o_ref, aq_ref, ak_ref, carry_ref,
                       *, blocks_per_seq):
    i = pl.program_id(0)
    xb = x_ref[...].astype(jnp.bfloat16)
    h = jnp.dot(xb, w_ref[...], preferred_element_type=jnp.float32)
    lane = lax.broadcasted_iota(jnp.int32, h.shape, 1)
    ff = h + bf_ref[...]
    logf = jnp.minimum(ff, 0.0) - jnp.log(1.0 + jnp.exp(-jnp.abs(ff)))
    o = jnp.where(lane < F_LANE0, h * IDX_WEIGHT_SCALE, logf * LOG2E)
    o_ref[...] = o

    @pl.when(i % blocks_per_seq == 0)
    def _():
        carry_ref[...] = jnp.zeros_like(carry_ref)

    tri = tri_ref[...]
    hi, mid, lo = _split3(o)
    cs = (jnp.dot(tri, hi, preferred_element_type=jnp.float32)
          + jnp.dot(tri, mid, preferred_element_type=jnp.float32)
          + jnp.dot(tri, lo, preferred_element_type=jnp.float32))
    cs = cs + carry_ref[0:1, :]
    tm = cs.shape[0]
    carry_ref[...] = jnp.broadcast_to(cs[tm - 1:tm, :], carry_ref.shape)
    parts = _split3(cs)
    aq = ones_ref[0:1, :]
    ak = ones_ref[1:2, :]
    for j in range(3):
        aq = aq + jnp.dot(parts[j], place_ref[j], preferred_element_type=jnp.float32)
        ak = ak + jnp.dot(parts[j], place_ref[3 + j], preferred_element_type=jnp.float32)
    aq_ref[...] = aq.astype(aq_ref.dtype)
    ak_ref[...] = ak.astype(ak_ref.dtype)


def _proj_small(x2d, w, bf_row, seq, tm=512):
    m = x2d.shape[0]
    tri = jnp.asarray(np.tril(np.ones((tm, tm), np.float32)), jnp.bfloat16)
    place, ones = _gate_aug_constants()
    kern = functools.partial(_proj_small_kernel, blocks_per_seq=seq // tm)
    row = lambda i: (i, 0)
    return pl.pallas_call(
        kern,
        grid=(m // tm,),
        in_specs=[
            pl.BlockSpec((tm, D_MODEL), row),
            pl.BlockSpec((D_MODEL, LANES), lambda i: (0, 0)),
            pl.BlockSpec((1, LANES), lambda i: (0, 0)),
            pl.BlockSpec((tm, tm), lambda i: (0, 0)),
            pl.BlockSpec((6, LANES, LANES), lambda i: (0, 0, 0)),
            pl.BlockSpec((8, LANES), lambda i: (0, 0)),
        ],
        out_specs=[pl.BlockSpec((tm, LANES), row), pl.BlockSpec((tm, LANES), row), pl.BlockSpec((tm, LANES), row)],
        out_shape=[jax.ShapeDtypeStruct((m, LANES), jnp.float32),
                   jax.ShapeDtypeStruct((m, LANES), jnp.bfloat16),
                   jax.ShapeDtypeStruct((m, LANES), jnp.bfloat16)],
        scratch_shapes=[pltpu.VMEM((8, LANES), jnp.float32)],
        compiler_params=_cparams(("arbitrary",)),
        name="proj_small",
    )(x2d, w, bf_row, tri, place, ones)


def _flash_init(m_ref, acc_ref):
    m_ref[...] = jnp.full(m_ref.shape, -1e30, jnp.float32)
    acc_ref[...] = jnp.zeros(acc_ref.shape, jnp.float32)


def _flash_rows(s, v1, m_ref, acc_ref, rows):
    tk = s.shape[1]
    m_prev = m_ref[rows, :]
    m_next = jnp.maximum(m_prev, jnp.max(s, axis=1, keepdims=True))
    p = jnp.exp2((s - pltpu.repeat(m_next, tk // LANES, axis=1)).astype(jnp.bfloat16))
    alpha = jnp.exp2(m_prev - m_next)
    m_ref[rows, :] = m_next
    pv = jnp.dot(p, v1, preferred_element_type=jnp.float32)
    acc_ref[rows, :] = acc_ref[rows, :] * pltpu.repeat(alpha, v1.shape[1] // LANES, axis=1) + pv


def _pipelined_chunks(n_full, produce, consume, buf0, buf1):
    produce(0, buf0)

    def body(p, carry):
        produce(2 * p + 1, buf1)
        consume(2 * p, buf0, False)
        produce(2 * p + 2, buf0)
        consume(2 * p + 1, buf1, False)
        return carry

    lax.fori_loop(0, n_full // 2, body, 0)

    @pl.when(n_full % 2 == 1)
    def _():
        produce(n_full, buf1)
        consume(n_full - 1, buf0, False)
        consume(n_full, buf1, True)

    @pl.when(n_full % 2 == 0)
    def _():
        consume(n_full, buf0, True)


def _stack_pair(q, qs_ref, t):
    lane = lax.broadcasted_iota(jnp.int32, q.shape, 1)
    zero = jnp.zeros_like(q)
    qs_ref[0:t, :] = jnp.where(lane < HEAD_DIM, q, zero)
    qs_ref[t:2 * t, :] = jnp.where(lane >= HEAD_DIM, q, zero)


def _with_ones(v):
    return jnp.concatenate([v, jnp.ones((v.shape[0], LANES), v.dtype)], axis=1)


def _diff_kernel(lam_ref, g_ref, q_ref, k_ref, v_ref, o_ref, qs_ref, m_ref, acc_ref, s0_ref, s1_ref,
                 *, tq, tk):
    i = pl.program_id(2)
    _stack_pair(q_ref[0], qs_ref, tq)
    _flash_init(m_ref, acc_ref)
    groups = (slice(0, tq), slice(tq, 2 * tq))

    def produce(j, s_ref):
        start = pl.multiple_of(j * tk, tk)
        k = k_ref[0, pl.ds(start, tk), :]
        for rows in groups:
            s_ref[rows, :] = lax.dot_general(qs_ref[rows, :], k, _NT, preferred_element_type=jnp.float32)

    def consume(j, s_ref, last):
        start = pl.multiple_of(j * tk, tk)
        v1 = _with_ones(v_ref[0, pl.ds(start, tk), :])
        for rows in groups:
            s = s_ref[rows, :]
            if last:
                r = lax.broadcasted_iota(jnp.int32, (tq, tk), 0)
                c = lax.broadcasted_iota(jnp.int32, (tq, tk), 1)
                s = jnp.where(start + c <= i * tq + r, s, -jnp.inf)
            _flash_rows(s, v1, m_ref, acc_ref, rows)

    _pipelined_chunks((i * tq) // tk, produce, consume, s0_ref, s1_ref)

    lam_rows = lam_ref[...]
    s1 = jnp.sum(lam_rows[0:1, :] * lam_rows[1:2, :], axis=1, keepdims=True)
    s2 = jnp.sum(lam_rows[2:3, :] * lam_rows[3:4, :], axis=1, keepdims=True)
    lam_init = lam_rows[4:5, 0:1]
    lam = jnp.exp(s1) - jnp.exp(s2) + lam_init
    o0 = acc_ref[0:tq, 0:LANES] / acc_ref[0:tq, LANES:2 * LANES]
    o1 = acc_ref[tq:2 * tq, 0:LANES] / acc_ref[tq:2 * tq, LANES:2 * LANES]
    a = o0 - lam * o1
    ms = jnp.mean(a * a, axis=1, keepdims=True)
    a = a * lax.rsqrt(ms + SUBLN_EPS) * g_ref[...] * (1.0 - lam_init)
    o_ref[0] = a.astype(o_ref.dtype)


def _diff_attention(a3, b3, lam_rows, g_row, tq=512, tk=1024):
    bsz, seq, _ = a3.shape
    tk = min(tk, seq)
    kern = functools.partial(_diff_kernel, tq=tq, tk=tk)
    return pl.pallas_call(
        kern,
        grid=(bsz, DIFF_HEADS, seq // tq),
        in_specs=[
            pl.BlockSpec((8, LANES), lambda b, h, i: (0, 0)),
            pl.BlockSpec((1, LANES), lambda b, h, i: (0, 0)),
            pl.BlockSpec((1, tq, LANES), lambda b, h, i: (b, i, h)),
            pl.BlockSpec((1, seq, LANES), lambda b, h, i: (b, 0, 4 + h)),
            pl.BlockSpec((1, seq, LANES), lambda b, h, i: (b, 0, 8 + h)),
        ],
        out_specs=pl.BlockSpec((1, tq, LANES), lambda b, h, i: (b, i, h)),
        out_shape=jax.ShapeDtypeStruct((bsz, seq, DIFF_HEADS * LANES), jnp.bfloat16),
        scratch_shapes=[pltpu.VMEM((2 * tq, LANES), jnp.bfloat16),
                        pltpu.VMEM((2 * tq, LANES), jnp.float32),
                        pltpu.VMEM((2 * tq, 2 * LANES), jnp.float32),
                        pltpu.VMEM((2 * tq, tk), jnp.float32),
                        pltpu.VMEM((2 * tq, tk), jnp.float32)],
        compiler_params=_cparams(("parallel", "parallel", "arbitrary")),
        name="diff_attention",
    )(lam_rows, g_row, a3, a3, b3)


def _fox_kernel(q_ref, k_ref, v_ref, aq_ref, ak_ref, o_ref, qs_ref, m_ref, acc_ref, s0_ref, s1_ref, *, tq, tk):
    g = pl.program_id(1)
    i = pl.program_id(2)
    q = q_ref[0]
    aq = aq_ref[0]
    lane = lax.broadcasted_iota(jnp.int32, (tq, LANES), 1)
    zero = jnp.zeros_like(q)
    for a in range(2):
        rows = slice(a * tq, (a + 1) * tq)
        head_lanes = (lane < HEAD_DIM) if a == 0 else (lane >= HEAD_DIM)
        first = 8 * (2 * g + a)
        qs_ref[rows, 0:LANES] = jnp.where(head_lanes, q, zero)
        qs_ref[rows, LANES:2 * LANES] = jnp.where((lane >= first) & (lane < first + 6), aq, zero)
    _flash_init(m_ref, acc_ref)
    groups = (slice(0, tq), slice(tq, 2 * tq))

    def produce(j, s_ref):
        start = pl.multiple_of(j * tk, tk)
        k2 = jnp.concatenate([k_ref[0, pl.ds(start, tk), :], ak_ref[0, pl.ds(start, tk), :]], axis=1)
        for rows in groups:
            s_ref[rows, :] = lax.dot_general(qs_ref[rows, :], k2, _NT, preferred_element_type=jnp.float32)

    def consume(j, s_ref, last):
        start = pl.multiple_of(j * tk, tk)
        v1 = _with_ones(v_ref[0, pl.ds(start, tk), :])
        for rows in groups:
            s = s_ref[rows, :]
            if last:
                r = lax.broadcasted_iota(jnp.int32, (tq, tk), 0)
                c = lax.broadcasted_iota(jnp.int32, (tq, tk), 1)
                s = jnp.where(start + c <= i * tq + r, s, -jnp.inf)
            _flash_rows(s, v1, m_ref, acc_ref, rows)

    _pipelined_chunks((i * tq) // tk, produce, consume, s0_ref, s1_ref)

    o0 = acc_ref[0:tq, 0:LANES] / acc_ref[0:tq, LANES:2 * LANES]
    o1 = acc_ref[tq:2 * tq, 0:LANES] / acc_ref[tq:2 * tq, LANES:2 * LANES]
    o_ref[0] = jnp.where(lane < HEAD_DIM, o0, o1).astype(o_ref.dtype)


def _fox_attention(b3, aq3, ak3, tq=512, tk=1024):
    bsz, seq, _ = b3.shape
    tk = min(tk, seq)
    kern = functools.partial(_fox_kernel, tq=tq, tk=tk)
    return pl.pallas_call(
        kern,
        grid=(bsz, FOX_HEADS // 2, seq // tq),
        in_specs=[
            pl.BlockSpec((1, tq, LANES), lambda b, g, i: (b, i, 12 + g)),
            pl.BlockSpec((1, seq, LANES), lambda b, g, i: (b, 0, 14 + g)),
            pl.BlockSpec((1, seq, LANES), lambda b, g, i: (b, 0, 16 + g)),
            pl.BlockSpec((1, tq, LANES), lambda b, g, i: (b, i, 0)),
            pl.BlockSpec((1, seq, LANES), lambda b, g, i: (b, 0, 0)),
        ],
        out_specs=pl.BlockSpec((1, tq, LANES), lambda b, g, i: (b, i, g)),
        out_shape=jax.ShapeDtypeStruct((bsz, seq, FOX_HEADS * HEAD_DIM), jnp.bfloat16),
        scratch_shapes=[pltpu.VMEM((2 * tq, 2 * LANES), jnp.bfloat16),
                        pltpu.VMEM((2 * tq, LANES), jnp.float32),
                        pltpu.VMEM((2 * tq, 2 * LANES), jnp.float32),
                        pltpu.VMEM((2 * tq, tk), jnp.float32),
                        pltpu.VMEM((2 * tq, tk), jnp.float32)],
        compiler_params=_cparams(("parallel", "parallel", "arbitrary")),
        name="fox_attention",
    )(b3, b3, b3, aq3, ak3)


def _score_key(score):
    bits = pltpu.bitcast(score, jnp.int32)
    return bits ^ ((bits >> 31) & 0x7FFFFFFF)


def _dsa_kernel(iq_ref, sq_ref, w_ref, ik_ref, sk_ref, v_ref, o_ref,
                keys_ref, cand_ref, candk_ref, qi_ref, qd_ref, wb_ref, m_ref, acc_ref, s0_ref, s1_ref,
                thr_ref, cnt_ref, flag_ref, *, tq, tc, topk, idx_bits, slots):
    i = pl.program_id(1)
    lane = lax.broadcasted_iota(jnp.int32, (tq, LANES), 1)
    low = lane < HEAD_DIM

    for h in range(IDX_HEADS):
        blk = iq_ref[0, :, (h // 2) * LANES:(h // 2 + 1) * LANES]
        qi_ref[h * tq:(h + 1) * tq, :] = jnp.where(low if h % 2 == 0 else ~low, blk, jnp.zeros_like(blk))
    dsa_order = (0, 2, 1, 3)
    for slab, h in enumerate(dsa_order):
        blk = sq_ref[0, :, (h // 2) * LANES:(h // 2 + 1) * LANES]
        qd_ref[slab * tq:(slab + 1) * tq, :] = jnp.where(low if h % 2 == 0 else ~low, blk, jnp.zeros_like(blk))
    wt = w_ref[0]
    for h in range(IDX_HEADS):
        col = jnp.sum(jnp.where(lane == W_LANE0 + h, wt, 0.0), axis=1, keepdims=True)
        wb_ref[h] = jnp.broadcast_to(col, (tq, LANES))

    nlast = (i * tq) // tc
    nch = nlast + 1
    cand_ref[...] = jnp.full(cand_ref.shape, -jnp.inf, jnp.float32)
    row_pos = i * tq + lax.broadcasted_iota(jnp.int32, (tq, tc), 0)
    col_iota = lax.broadcasted_iota(jnp.int32, (tq, tc), 1)
    idx_groups = (slice(0, 4 * tq), slice(4 * tq, 8 * tq))
    dsa_groups = (slice(0, 2 * tq), slice(2 * tq, 4 * tq))

    def rep(x):
        return pltpu.repeat(x, tc // LANES, axis=1)

    def idx_produce(j, s_ref):
        start = pl.multiple_of(j * tc, tc)
        ik = ik_ref[0, pl.ds(start, tc), :]
        for rows in idx_groups:
            s_ref[rows, :] = lax.dot_general(qi_ref[rows, :], ik, _NT, preferred_element_type=jnp.float32)

    def idx_consume(j, s_ref, last):
        start = pl.multiple_of(j * tc, tc)
        score = jnp.zeros((tq, tc), jnp.float32)
        for h in range(IDX_HEADS):
            score = score + jnp.maximum(s_ref[h * tq:(h + 1) * tq, :], 0.0) * rep(wb_ref[h])
        if last:
            score = jnp.where(start + col_iota <= row_pos, score, -jnp.inf)
        keys_ref[:, pl.ds(start, tc)] = _score_key(score)
        for rg in range(tq // 8):
            rs = slice(rg * 8, (rg + 1) * 8)
            xs = [score[rs, u * LANES:(u + 1) * LANES] for u in range(4)]
            for p, q in _SORT4:
                xs[p], xs[q] = jnp.maximum(xs[p], xs[q]), jnp.minimum(xs[p], xs[q])
            c = [cand_ref[rs, sl * LANES:(sl + 1) * LANES] for sl in range(12)]
            for t in range(4):
                c[8 + t] = jnp.maximum(c[8 + t], xs[3 - t])
            for p, q in _BITONIC12:
                c[p], c[q] = jnp.maximum(c[p], c[q]), jnp.minimum(c[p], c[q])
            for sl in range(12):
                cand_ref[rs, sl * LANES:(sl + 1) * LANES] = c[sl]

    _pipelined_chunks(nlast, idx_produce, idx_consume, s0_ref, s1_ref)

    def count(pred):
        def body(c, acc):
            start = pl.multiple_of(c * tc, tc)
            hit = pred(keys_ref[:, pl.ds(start, tc)], start).astype(jnp.int32)
            for u in range(tc // LANES):
                acc = acc + hit[:, u * LANES:(u + 1) * LANES]
            return acc
        part = lax.fori_loop(0, nch, body, jnp.zeros((tq, LANES), jnp.int32))
        return jnp.broadcast_to(jnp.sum(part, axis=1, keepdims=True), (tq, LANES))

    n_row = i * tq + lax.broadcasted_iota(jnp.int32, (tq, LANES), 0) + 1

    def search_all_keys():
        def pending(state):
            bi, thr, cnt_thr = state
            return jnp.logical_and(bi < 32, jnp.max(jnp.where(n_row > topk, cnt_thr, topk)) > topk)

        def bit_step(state):
            bi, thr, cnt_thr = state
            cand = thr + lax.shift_left(jnp.int32(1), 31 - bi)
            cnt = count(lambda kc, start: kc >= rep(cand))
            ok = cnt >= topk
            return bi + 1, jnp.where(ok, cand, thr), jnp.where(ok, cnt, cnt_thr)

        total = jnp.zeros((tq, LANES), jnp.int32) + nch * tc
        init = (jnp.int32(0), jnp.full((tq, LANES), INT_MIN, jnp.int32), total)
        _, thr, cnt_thr = lax.while_loop(pending, bit_step, init)
        thr_ref[...] = thr
        cnt_ref[...] = cnt_thr

    def search_candidates():
        def pair_step(p, state):
            thr, cnt_thr = state
            one_lo = lax.shift_left(jnp.int32(1), 30 - 2 * p)
            cands = [thr + one_lo, thr + 2 * one_lo, thr + 3 * one_lo]
            parts = [jnp.zeros((tq, LANES), jnp.int32) for _ in cands]
            for sl in range(slots):
                piece = candk_ref[:, sl * LANES:(sl + 1) * LANES]
                parts = [jnp.where(piece >= c, sl + 1, part) for c, part in zip(cands, parts)]
            for c, part in zip(cands, parts):
                cnt = jnp.broadcast_to(jnp.sum(part, axis=1, keepdims=True), (tq, LANES))
                ok = cnt >= topk
                thr = jnp.where(ok, c, thr)
                cnt_thr = jnp.where(ok, cnt, cnt_thr)
            return thr, cnt_thr

        init = (jnp.full((tq, LANES), INT_MIN, jnp.int32), jnp.full((tq, LANES), slots * LANES, jnp.int32))
        thr, cnt_thr = lax.fori_loop(0, 16, pair_step, init, unroll=2)
        thr_ref[...] = thr
        cnt_ref[...] = cnt_thr

    cand_chunks = slots * LANES // tc
    flag_ref[0] = 1

    @pl.when(nch > cand_chunks)
    def _():
        candk_ref[...] = _score_key(cand_ref[...])
        search_candidates()
        kept_min = jnp.max(candk_ref[:, (slots - 1) * LANES:slots * LANES], axis=1, keepdims=True)
        unsafe = jnp.where(kept_min >= thr_ref[...], 1, 0)
        flag_ref[0] = jnp.max(unsafe)

    @pl.when(flag_ref[0] > 0)
    def _():
        search_all_keys()

    thr = thr_ref[...]
    cnt_thr = cnt_ref[...]

    excess = jnp.where(thr > KEY_NEG_INF, cnt_thr - topk, 0)

    @pl.when(jnp.max(excess) > 0)
    def _():
        c_gt = count(lambda kc, start: kc > rep(thr))
        keep = topk - c_gt

        def idx_step(bi, x):
            cand = x + lax.shift_left(jnp.int32(1), idx_bits - 1 - bi)
            below = count(lambda kc, start: (kc == rep(thr)) & (start + col_iota < rep(cand)))
            return jnp.where(below < keep, cand, x)

        last = lax.fori_loop(0, idx_bits, idx_step, jnp.zeros((tq, LANES), jnp.int32))

        def demote(c, carry):
            start = pl.multiple_of(c * tc, tc)
            kc = keys_ref[:, pl.ds(start, tc)]
            drop = (rep(excess) > 0) & (kc == rep(thr)) & (start + col_iota > rep(last))
            keys_ref[:, pl.ds(start, tc)] = jnp.where(drop, kc - 1, kc)
            return carry

        lax.fori_loop(0, nch, demote, 0)

    sel_thr = jnp.maximum(thr, KEY_NEG_INF + 1)

    _flash_init(m_ref, acc_ref)
    vlow = lax.broadcasted_iota(jnp.int32, (tc, LANES), 1) < HEAD_DIM

    def att_produce(j, s_ref):
        start = pl.multiple_of(j * tc, tc)
        sk = sk_ref[0, pl.ds(start, tc), :]
        for rows in dsa_groups:
            s_ref[rows, :] = lax.dot_general(qd_ref[rows, :], sk, _NT, preferred_element_type=jnp.float32)

    def att_consume(j, s_ref, last):
        start = pl.multiple_of(j * tc, tc)
        v2 = v_ref[0, pl.ds(start, tc), :]
        one = jnp.ones_like(v2)
        v_even = jnp.where(vlow, v2, one)
        v_odd = jnp.where(vlow, one, v2)
        sel = keys_ref[:, pl.ds(start, tc)] >= rep(sel_thr)
        for rows, v1 in zip(dsa_groups, (v_even, v_odd)):
            s = s_ref[rows, :].reshape(2, tq, tc)
            s = jnp.where(sel[None], s, -jnp.inf).reshape(2 * tq, tc)
            _flash_rows(s, v1, m_ref, acc_ref, rows)

    _pipelined_chunks(nlast, att_produce, att_consume, s0_ref, s1_ref)

    outs = {}
    for slab, h in enumerate(dsa_order):
        acc = acc_ref[slab * tq:(slab + 1) * tq, :]
        outs[h] = acc / pltpu.roll(acc, HEAD_DIM, axis=1)
    for p in range(DSA_HEADS // 2):
        o_ref[0, :, p * LANES:(p + 1) * LANES] = jnp.where(low, outs[2 * p], outs[2 * p + 1]).astype(o_ref.dtype)


def _dsa_attention(a3, b3, w3, tq=128):
    bsz, seq, _ = a3.shape
    tc, slots = DSA_SCORE_CHUNK, DSA_SLOTS
    topk = min(DSA_TOPK_MAX, seq // 4)
    idx_bits = int(math.log2(seq))
    assert 2 ** idx_bits == seq and tc % tq == 0 and (slots * LANES) % tc == 0 and seq % tc == 0
    kern = functools.partial(_dsa_kernel, tq=tq, tc=tc, topk=topk, idx_bits=idx_bits, slots=slots)
    return pl.pallas_call(
        kern,
        grid=(bsz, seq // tq),
        in_specs=[
            pl.BlockSpec((1, tq, 4 * LANES), lambda b, i: (b, i, 2)),
            pl.BlockSpec((1, tq, 2 * LANES), lambda b, i: (b, i, 6)),
            pl.BlockSpec((1, tq, LANES), lambda b, i: (b, i, 0)),
            pl.BlockSpec((1, seq, LANES), lambda b, i: (b, 0, 14)),
            pl.BlockSpec((1, seq, LANES), lambda b, i: (b, 0, 15)),
            pl.BlockSpec((1, seq, LANES), lambda b, i: (b, 0, 18)),
        ],
        out_specs=pl.BlockSpec((1, tq, 2 * LANES), lambda b, i: (b, i, 0)),
        out_shape=jax.ShapeDtypeStruct((bsz, seq, DSA_HEADS * HEAD_DIM), jnp.bfloat16),
        scratch_shapes=[pltpu.VMEM((tq, seq), jnp.int32),
                        pltpu.VMEM((tq, slots * LANES), jnp.float32),
                        pltpu.VMEM((tq, slots * LANES), jnp.int32),
                        pltpu.VMEM((IDX_HEADS * tq, LANES), jnp.bfloat16),
                        pltpu.VMEM((DSA_HEADS * tq, LANES), jnp.bfloat16),
                        pltpu.VMEM((IDX_HEADS, tq, LANES), jnp.float32),
                        pltpu.VMEM((DSA_HEADS * tq, LANES), jnp.float32),
                        pltpu.VMEM((DSA_HEADS * tq, LANES), jnp.float32),
                        pltpu.VMEM((IDX_HEADS * tq, tc), jnp.float32),
                        pltpu.VMEM((IDX_HEADS * tq, tc), jnp.float32),
                        pltpu.VMEM((tq, LANES), jnp.int32),
                        pltpu.VMEM((tq, LANES), jnp.int32),
                        pltpu.SMEM((1,), jnp.int32)],
        compiler_params=_cparams(("parallel", "arbitrary")),
        name="dsa_attention",
    )(a3, a3, w3, a3, a3, b3)


def _out_kernel(x_ref, gate_ref, a_ref, bo_ref, co_ref, wa_ref, wb_ref, wc_ref, g_ref, b_ref, o_ref):
    gate = gate_ref[...].astype(jnp.float32)
    silu = gate / (1.0 + jnp.exp(-gate))

    def gated(ref, lo, hi):
        return (ref[...].astype(jnp.float32) * silu[:, lo:hi]).astype(jnp.bfloat16)

    out = jnp.dot(gated(a_ref, 0, 512), wa_ref[...], preferred_element_type=jnp.float32)
    out = out + jnp.dot(gated(bo_ref, 512, 768), wb_ref[...], preferred_element_type=jnp.float32)
    out = out + jnp.dot(gated(co_ref, 768, 1024), wc_ref[...], preferred_element_type=jnp.float32)
    y = DEEPNORM_ALPHA * x_ref[...] + out
    mu = jnp.mean(y, axis=1, keepdims=True)
    d = y - mu
    var = jnp.mean(d * d, axis=1, keepdims=True)
    o_ref[...] = d * lax.rsqrt(var + LN_EPS) * g_ref[...] + b_ref[...]


def _out_proj(x2d, b2d, a2d, bo2d, co2d, w_out, ln_g, ln_b, tm=256):
    m = x2d.shape[0]
    row = lambda i: (i, 0)
    fixed = lambda i: (0, 0)
    return pl.pallas_call(
        _out_kernel,
        grid=(m // tm,),
        in_specs=[
            pl.BlockSpec((tm, D_MODEL), row),
            pl.BlockSpec((tm, D_MODEL), row),
            pl.BlockSpec((tm, 512), row),
            pl.BlockSpec((tm, 256), row),
            pl.BlockSpec((tm, 256), row),
            pl.BlockSpec((512, D_MODEL), fixed),
            pl.BlockSpec((256, D_MODEL), lambda i: (2, 0)),
            pl.BlockSpec((256, D_MODEL), lambda i: (3, 0)),
            pl.BlockSpec((1, D_MODEL), fixed),
            pl.BlockSpec((1, D_MODEL), fixed),
        ],
        out_specs=pl.BlockSpec((tm, D_MODEL), row),
        out_shape=jax.ShapeDtypeStruct((m, D_MODEL), jnp.float32),
        compiler_params=_cparams(("parallel",)),
        name="out_proj",
    )(x2d, b2d, a2d, bo2d, co2d, w_out, w_out, w_out, ln_g, ln_b)


def _rope_tables(seq):
    inv = ROPE_THETA ** (-jnp.arange(0, HEAD_DIM, 2, dtype=jnp.float32) / HEAD_DIM)
    ang = jnp.arange(seq, dtype=jnp.int32).astype(jnp.float32)[:, None] * inv[None, :]
    reps = LANES // (HEAD_DIM // 2)
    return jnp.tile(jnp.cos(ang), (1, reps)), jnp.tile(jnp.sin(ang), (1, reps))


def kernel(x, w_in, b_f, lam_q1, lam_k1, lam_q2, lam_k2, g_subln, w_out, ln_g, ln_b):
    bsz, seq, _ = x.shape
    m = bsz * seq
    cos, sin = _rope_tables(seq)

    w_a = w_in[:, :, A_IDX] * A_SCALE
    w_ar = (w_a[:, :, A_ROT_PARTNER] * A_ROT_SIGN).astype(jnp.bfloat16)
    w_a = w_a.astype(jnp.bfloat16)
    w_b = (w_in[:, :, B_IDX] * B_SCALE).astype(jnp.bfloat16)
    w_c = (w_in[:, :, C_IDX] * C_SCALE).astype(jnp.bfloat16)
    w_o = w_out.astype(jnp.bfloat16)
    bf_rows = jnp.zeros((DEPTH, 1, LANES), jnp.float32).at[:, 0, F_LANE0:F_LANE0 + FOX_HEADS].set(b_f)

    x2d = x.reshape(m, D_MODEL)
    for l in range(DEPTH):
        lam_init = 0.8 - 0.6 * math.exp(-0.3 * l)
        lam_rows = jnp.zeros((8, LANES), jnp.float32)
        lam_rows = lam_rows.at[0, :HEAD_DIM].set(lam_q1[l]).at[1, :HEAD_DIM].set(lam_k1[l])
        lam_rows = lam_rows.at[2, :HEAD_DIM].set(lam_q2[l]).at[3, :HEAD_DIM].set(lam_k2[l])
        lam_rows = lam_rows.at[4, :].set(lam_init)

        a2d = _proj_rope(x2d, w_a[l], w_ar[l], cos, sin, seq)
        b2d = _proj_plain(x2d, w_b[l])
        wf2d, aq2d, ak2d = _proj_small(x2d, w_c[l], bf_rows[l], seq)

        a3 = a2d.reshape(bsz, seq, A_WIDTH)
        b3 = b2d.reshape(bsz, seq, B_WIDTH)
        wf3 = wf2d.reshape(bsz, seq, LANES)

        diff_o = _diff_attention(a3, b3, lam_rows, g_subln[l].reshape(1, LANES))
        fox_o = _fox_attention(b3, aq2d.reshape(bsz, seq, LANES), ak2d.reshape(bsz, seq, LANES))
        dsa_o = _dsa_attention(a3, b3, wf3)

        x2d = _out_proj(x2d, b2d, diff_o.reshape(m, 512), dsa_o.reshape(m, 256), fox_o.reshape(m, 256),
                        w_o[l], ln_g[l].reshape(1, D_MODEL), ln_b[l].reshape(1, D_MODEL))
    return x2d.reshape(bsz, seq, D_MODEL)
```

```python
import functools
import math

import numpy as np
import jax
import jax.numpy as jnp
from jax import lax
from jax.experimental import pallas as pl
from jax.experimental.pallas import tpu as pltpu

D_MODEL = 1024
DEPTH = 4
HEAD_DIM = 64
DIFF_HEADS = 4
DSA_HEADS = 4
IDX_HEADS = 8
FOX_HEADS = 4
DSA_TOPK_MAX = 256
ROPE_THETA = 10000.0
LN_EPS = 1e-5
SUBLN_EPS = 1e-5
DEEPNORM_ALPHA = (2 * DEPTH) ** 0.25
IDX_WEIGHT_SCALE = (IDX_HEADS * HEAD_DIM) ** -0.5
LOG2E = math.log2(math.e)
Q_SCALE = HEAD_DIM ** -0.5 * LOG2E

LANES = 128
VMEM_LIMIT_BYTES = 56 * 1024 * 1024

_OFF = {}
_o = 0
for _name, _n in (('diff_q', 512), ('diff_k', 512), ('diff_v', 512), ('dsa_q', 256), ('dsa_k', 64),
                  ('dsa_v', 64), ('idx_q', 512), ('idx_k', 64), ('idx_w', 8), ('fox_q', 256),
                  ('fox_k', 256), ('fox_v', 256), ('fox_f', 4), ('gate', 1024)):
    _OFF[_name] = (_o, _n)
    _o += _n
IN_WIDTH = _o


def _cols(name, scale=1.0, repeat=1):
    o, n = _OFF[name]
    idx = np.tile(np.arange(o, o + n), repeat)
    return idx, np.full(idx.shape, scale, np.float32)


def _pad(n):
    return np.zeros((n,), np.int64), np.zeros((n,), np.float32)


def _layout(parts):
    idx = np.concatenate([p[0] for p in parts])
    scale = np.concatenate([p[1] for p in parts])
    return idx, scale


A_IDX, A_SCALE = _layout([_cols('diff_q', Q_SCALE), _cols('diff_k'), _cols('idx_q'),
                          _cols('dsa_q', Q_SCALE), _cols('idx_k', repeat=2), _cols('dsa_k', repeat=2)])
A_WIDTH = A_IDX.shape[0]
B_IDX, B_SCALE = _layout([_cols('gate'), _cols('diff_v'), _cols('fox_q', Q_SCALE), _cols('fox_k'),
                          _cols('fox_v'), _cols('dsa_v', repeat=2), _pad(128)])
B_WIDTH = B_IDX.shape[0]
C_IDX, C_SCALE = _layout([_cols('idx_w'), _cols('fox_f'), _pad(LANES - 12)])
W_LANE0 = 0
F_LANE0 = 8

_c = np.arange(A_WIDTH)
_first_half = (_c % HEAD_DIM) < HEAD_DIM // 2
A_ROT_PARTNER = np.where(_first_half, _c + HEAD_DIM // 2, _c - HEAD_DIM // 2)
A_ROT_SIGN = np.where(_first_half, -1.0, 1.0).astype(np.float32)

_NT = (((1,), (1,)), ((), ()))
INT_MIN = -2 ** 31
KEY_NEG_INF = -2139095041

_SORT4 = ((0, 1), (2, 3), (0, 2), (1, 3), (1, 2))
_BITONIC12 = ((0, 8), (1, 9), (2, 10), (3, 11), (4, 8), (5, 9), (6, 10), (7, 11),
              (0, 2), (1, 3), (4, 6), (5, 7), (8, 10), (9, 11),
              (0, 1), (2, 3), (4, 5), (6, 7), (8, 9), (10, 11))
DSA_SLOTS = 12
DSA_SCORE_CHUNK = 512


def _cparams(sem):
    return pltpu.CompilerParams(dimension_semantics=sem, vmem_limit_bytes=VMEM_LIMIT_BYTES)


PROJ_COLS = 512


def _proj_rope_kernel(x_ref, w_ref, wr_ref, cos_ref, sin_ref, o_ref):
    xb = x_ref[...].astype(jnp.bfloat16)
    cos = cos_ref[...]
    sin = sin_ref[...]
    for n in range(o_ref.shape[1] // PROJ_COLS):
        cols = slice(n * PROJ_COLS, (n + 1) * PROJ_COLS)
        h = jnp.dot(xb, w_ref[:, cols], preferred_element_type=jnp.float32)
        hr = jnp.dot(xb, wr_ref[:, cols], preferred_element_type=jnp.float32)
        for c in range(PROJ_COLS // LANES):
            sl = slice(c * LANES, (c + 1) * LANES)
            o_ref[:, n * PROJ_COLS + c * LANES:n * PROJ_COLS + (c + 1) * LANES] = (
                h[:, sl] * cos + hr[:, sl] * sin).astype(o_ref.dtype)


def _proj_rope(x2d, w, wr, cos, sin, seq, tm=512):
    m = x2d.shape[0]
    nseq = seq // tm
    return pl.pallas_call(
        _proj_rope_kernel,
        grid=(m // tm,),
        in_specs=[
            pl.BlockSpec((tm, D_MODEL), lambda i: (i, 0)),
            pl.BlockSpec((D_MODEL, A_WIDTH), lambda i: (0, 0)),
            pl.BlockSpec((D_MODEL, A_WIDTH), lambda i: (0, 0)),
            pl.BlockSpec((tm, LANES), lambda i: (i % nseq, 0)),
            pl.BlockSpec((tm, LANES), lambda i: (i % nseq, 0)),
        ],
        out_specs=pl.BlockSpec((tm, A_WIDTH), lambda i: (i, 0)),
        out_shape=jax.ShapeDtypeStruct((m, A_WIDTH), jnp.bfloat16),
        compiler_params=_cparams(("parallel",)),
        name="proj_rope",
    )(x2d, w, wr, cos, sin)


def _proj_plain_kernel(x_ref, w_ref, o_ref):
    xb = x_ref[...].astype(jnp.bfloat16)
    for n in range(o_ref.shape[1] // PROJ_COLS):
        cols = slice(n * PROJ_COLS, (n + 1) * PROJ_COLS)
        o_ref[:, cols] = jnp.dot(xb, w_ref[:, cols], preferred_element_type=jnp.float32).astype(o_ref.dtype)


def _proj_plain(x2d, w, tm=512):
    m = x2d.shape[0]
    return pl.pallas_call(
        _proj_plain_kernel,
        grid=(m // tm,),
        in_specs=[
            pl.BlockSpec((tm, D_MODEL), lambda i: (i, 0)),
            pl.BlockSpec((D_MODEL, B_WIDTH), lambda i: (0, 0)),
        ],
        out_specs=pl.BlockSpec((tm, B_WIDTH), lambda i: (i, 0)),
        out_shape=jax.ShapeDtypeStruct((m, B_WIDTH), jnp.bfloat16),
        compiler_params=_cparams(("parallel",)),
        name="proj_plain",
    )(x2d, w)


def _split3(v):
    hi = v.astype(jnp.bfloat16)
    r = v - hi.astype(jnp.float32)
    mid = r.astype(jnp.bfloat16)
    lo = (r - mid.astype(jnp.float32)).astype(jnp.bfloat16)
    return hi, mid, lo


def _gate_aug_constants():
    place = np.zeros((6, LANES, LANES), np.float32)
    ones = np.zeros((8, LANES), np.float32)
    for h in range(FOX_HEADS):
        for j in range(3):
            place[j, F_LANE0 + h, 8 * h + j] = 1.0
            place[3 + j, F_LANE0 + h, 8 * h + 3 + j] = -1.0
            ones[0, 8 * h + 3 + j] = 1.0
            ones[1, 8 * h + j] = 1.0
    return jnp.asarray(place, jnp.bfloat16), jnp.asarray(ones)


def _proj_small_kernel(x_ref, w_ref, bf_ref, tri_ref, place_ref, ones_ref, o_ref, aq_ref, ak_ref, carry_ref,
                       *, blocks_per_seq):
    i = pl.program_id(0)
    xb = x_ref[...].astype(jnp.bfloat16)
    h = jnp.dot(xb, w_ref[...], preferred_element_type=jnp.float32)
    lane = lax.broadcasted_iota(jnp.int32, h.shape, 1)
    ff = h + bf_ref[...]
    logf = jnp.minimum(ff, 0.0) - jnp.log(1.0 + jnp.exp(-jnp.abs(ff)))
    o = jnp.where(lane < F_LANE0, h * IDX_WEIGHT_SCALE, logf * LOG2E)
    o_ref[...] = o

    @pl.when(i % blocks_per_seq == 0)
    def _():
        carry_ref[...] = jnp.zeros_like(carry_ref)

    tri = tri_ref[...]
    hi, mid, lo = _split3(o)
    cs = (jnp.dot(tri, hi, preferred_element_type=jnp.float32)
          + jnp.dot(tri, mid, preferred_element_type=jnp.float32)
          + jnp.dot(tri, lo, preferred_element_type=jnp.float32))
    cs = cs + carry_ref[0:1, :]
    tm = cs.shape[0]
    carry_ref[...] = jnp.broadcast_to(cs[tm - 1:tm, :], carry_ref.shape)
    parts = _split3(cs)
    aq = ones_ref[0:1, :]
    ak = ones_ref[1:2, :]
    for j in range(3):
        aq = aq + jnp.dot(parts[j], place_ref[j], preferred_element_type=jnp.float32)
        ak = ak + jnp.dot(parts[j], place_ref[3 + j], preferred_element_type=jnp.float32)
    aq_ref[...] = aq.astype(aq_ref.dtype)
    ak_ref[...] = ak.astype(ak_ref.dtype)


def _proj_small(x2d, w, bf_row, seq, tm=512):
    m = x2d.shape[0]
    tri = jnp.asarray(np.tril(np.ones((tm, tm), np.float32)), jnp.bfloat16)
    place, ones = _gate_aug_constants()
    kern = functools.partial(_proj_small_kernel, blocks_per_seq=seq // tm)
    row = lambda i: (i, 0)
    return pl.pallas_call(
        kern,
        grid=(m // tm,),
        in_specs=[
            pl.BlockSpec((tm, D_MODEL), row),
            pl.BlockSpec((D_MODEL, LANES), lambda i: (0, 0)),
            pl.BlockSpec((1, LANES), lambda i: (0, 0)),
            pl.BlockSpec((tm, tm), lambda i: (0, 0)),
            pl.BlockSpec((6, LANES, LANES), lambda i: (0, 0, 0)),
            pl.BlockSpec((8, LANES), lambda i: (0, 0)),
        ],
        out_specs=[pl.BlockSpec((tm, LANES), row), pl.BlockSpec((tm, LANES), row), pl.BlockSpec((tm, LANES), row)],
        out_shape=[jax.ShapeDtypeStruct((m, LANES), jnp.float32),
                   jax.ShapeDtypeStruct((m, LANES), jnp.bfloat16),
                   jax.ShapeDtypeStruct((m, LANES), jnp.bfloat16)],
        scratch_shapes=[pltpu.VMEM((8, LANES), jnp.float32)],
        compiler_params=_cparams(("arbitrary",)),
        name="proj_small",
    )(x2d, w, bf_row, tri, place, ones)


def _flash_init(m_ref, acc_ref):
    m_ref[...] = jnp.full(m_ref.shape, -1e30, jnp.float32)
    acc_ref[...] = jnp.zeros(acc_ref.shape, jnp.float32)


def _flash_rows(s, v1, m_ref, acc_ref, rows):
    tk = s.shape[1]
    m_prev = m_ref[rows, :]
    m_next = jnp.maximum(m_prev, jnp.max(s, axis=1, keepdims=True))
    p = jnp.exp2((s - pltpu.repeat(m_next, tk // LANES, axis=1)).astype(jnp.bfloat16))
    alpha = jnp.exp2(m_prev - m_next)
    m_ref[rows, :] = m_next
    pv = jnp.dot(p, v1, preferred_element_type=jnp.float32)
    acc_ref[rows, :] = acc_ref[rows, :] * pltpu.repeat(alpha, v1.shape[1] // LANES, axis=1) + pv


def _pipelined_chunks(n_full, produce, consume, buf0, buf1):
    produce(0, buf0)

    def body(p, carry):
        produce(2 * p + 1, buf1)
        consume(2 * p, buf0, False)
        produce(2 * p + 2, buf0)
        consume(2 * p + 1, buf1, False)
        return carry

    lax.fori_loop(0, n_full // 2, body, 0)

    @pl.when(n_full % 2 == 1)
    def _():
        produce(n_full, buf1)
        consume(n_full - 1, buf0, False)
        consume(n_full, buf1, True)

    @pl.when(n_full % 2 == 0)
    def _():
        consume(n_full, buf0, True)


def _stack_pair(q, qs_ref, t):
    lane = lax.broadcasted_iota(jnp.int32, q.shape, 1)
    zero = jnp.zeros_like(q)
    qs_ref[0:t, :] = jnp.where(lane < HEAD_DIM, q, zero)
    qs_ref[t:2 * t, :] = jnp.where(lane >= HEAD_DIM, q, zero)


def _with_ones(v):
    return jnp.concatenate([v, jnp.ones((v.shape[0], LANES), v.dtype)], axis=1)


def _diff_kernel(lam_ref, g_ref, q_ref, k_ref, v_ref, o_ref, qs_ref, m_ref, acc_ref, s0_ref, s1_ref,
                 *, tq, tk):
    i = pl.program_id(2)
    _stack_pair(q_ref[0], qs_ref, tq)
    _flash_init(m_ref, acc_ref)
    groups = (slice(0, tq), slice(tq, 2 * tq))

    def produce(j, s_ref):
        start = pl.multiple_of(j * tk, tk)
        k = k_ref[0, pl.ds(start, tk), :]
        for rows in groups:
            s_ref[rows, :] = lax.dot_general(qs_ref[rows, :], k, _NT, preferred_element_type=jnp.float32)

    def consume(j, s_ref, last):
        start = pl.multiple_of(j * tk, tk)
        v1 = _with_ones(v_ref[0, pl.ds(start, tk), :])
        for rows in groups:
            s = s_ref[rows, :]
            if last:
                r = lax.broadcasted_iota(jnp.int32, (tq, tk), 0)
                c = lax.broadcasted_iota(jnp.int32, (tq, tk), 1)
                s = jnp.where(start + c <= i * tq + r, s, -jnp.inf)
            _flash_rows(s, v1, m_ref, acc_ref, rows)

    _pipelined_chunks((i * tq) // tk, produce, consume, s0_ref, s1_ref)

    lam_rows = lam_ref[...]
    s1 = jnp.sum(lam_rows[0:1, :] * lam_rows[1:2, :], axis=1, keepdims=True)
    s2 = jnp.sum(lam_rows[2:3, :] * lam_rows[3:4, :], axis=1, keepdims=True)
    lam_init = lam_rows[4:5, 0:1]
    lam = jnp.exp(s1) - jnp.exp(s2) + lam_init
    o0 = acc_ref[0:tq, 0:LANES] / acc_ref[0:tq, LANES:2 * LANES]
    o1 = acc_ref[tq:2 * tq, 0:LANES] / acc_ref[tq:2 * tq, LANES:2 * LANES]
    a = o0 - lam * o1
    ms = jnp.mean(a * a, axis=1, keepdims=True)
    a = a * lax.rsqrt(ms + SUBLN_EPS) * g_ref[...] * (1.0 - lam_init)
    o_ref[0] = a.astype(o_ref.dtype)


def _diff_attention(a3, b3, lam_rows, g_row, tq=1024, tk=1024):
    bsz, seq, _ = a3.shape
    tk = min(tk, seq)
    tq = min(tq, seq)
    kern = functools.partial(_diff_kernel, tq=tq, tk=tk)
    return pl.pallas_call(
        kern,
        grid=(bsz, DIFF_HEADS, seq // tq),
        in_specs=[
            pl.BlockSpec((8, LANES), lambda b, h, i: (0, 0)),
            pl.BlockSpec((1, LANES), lambda b, h, i: (0, 0)),
            pl.BlockSpec((1, tq, LANES), lambda b, h, i: (b, i, h)),
            pl.BlockSpec((1, seq, LANES), lambda b, h, i: (b, 0, 4 + h)),
            pl.BlockSpec((1, seq, LANES), lambda b, h, i: (b, 0, 8 + h)),
        ],
        out_specs=pl.BlockSpec((1, tq, LANES), lambda b, h, i: (b, i, h)),
        out_shape=jax.ShapeDtypeStruct((bsz, seq, DIFF_HEADS * LANES), jnp.bfloat16),
        scratch_shapes=[pltpu.VMEM((2 * tq, LANES), jnp.bfloat16),
                        pltpu.VMEM((2 * tq, LANES), jnp.float32),
                        pltpu.VMEM((2 * tq, 2 * LANES), jnp.float32),
                        pltpu.VMEM((2 * tq, tk), jnp.float32),
                        pltpu.VMEM((2 * tq, tk), jnp.float32)],
        compiler_params=_cparams(("parallel", "parallel", "arbitrary")),
        name="diff_attention",
    )(lam_rows, g_row, a3, a3, b3)


def _fox_kernel(q_ref, k_ref, v_ref, aq_ref, ak_ref, o_ref, qs_ref, m_ref, acc_ref, s0_ref, s1_ref, *, tq, tk):
    g = pl.program_id(1)
    i = pl.program_id(2)
    q = q_ref[0]
    aq = aq_ref[0]
    lane = lax.broadcasted_iota(jnp.int32, (tq, LANES), 1)
    zero = jnp.zeros_like(q)
    for a in range(2):
        rows = slice(a * tq, (a + 1) * tq)
        head_lanes = (lane < HEAD_DIM) if a == 0 else (lane >= HEAD_DIM)
        first = 8 * (2 * g + a)
        qs_ref[rows, 0:LANES] = jnp.where(head_lanes, q, zero)
        qs_ref[rows, LANES:2 * LANES] = jnp.where((lane >= first) & (lane < first + 6), aq, zero)
    _flash_init(m_ref, acc_ref)
    groups = (slice(0, tq), slice(tq, 2 * tq))

    def produce(j, s_ref):
        start = pl.multiple_of(j * tk, tk)
        k2 = jnp.concatenate([k_ref[0, pl.ds(start, tk), :], ak_ref[0, pl.ds(start, tk), :]], axis=1)
        for rows in groups:
            s_ref[rows, :] = lax.dot_general(qs_ref[rows, :], k2, _NT, preferred_element_type=jnp.float32)

    def consume(j, s_ref, last):
        start = pl.multiple_of(j * tk, tk)
        v1 = _with_ones(v_ref[0, pl.ds(start, tk), :])
        for rows in groups:
            s = s_ref[rows, :]
            if last:
                r = lax.broadcasted_iota(jnp.int32, (tq, tk), 0)
                c = lax.broadcasted_iota(jnp.int32, (tq, tk), 1)
                s = jnp.where(start + c <= i * tq + r, s, -jnp.inf)
            _flash_rows(s, v1, m_ref, acc_ref, rows)

    _pipelined_chunks((i * tq) // tk, produce, consume, s0_ref, s1_ref)

    o0 = acc_ref[0:tq, 0:LANES] / acc_ref[0:tq, LANES:2 * LANES]
    o1 = acc_ref[tq:2 * tq, 0:LANES] / acc_ref[tq:2 * tq, LANES:2 * LANES]
    o_ref[0] = jnp.where(lane < HEAD_DIM, o0, o1).astype(o_ref.dtype)


def _fox_attention(b3, aq3, ak3, tq=512, tk=1024):
    bsz, seq, _ = b3.shape
    tk = min(tk, seq)
    kern = functools.partial(_fox_kernel, tq=tq, tk=tk)
    return pl.pallas_call(
        kern,
        grid=(bsz, FOX_HEADS // 2, seq // tq),
        in_specs=[
            pl.BlockSpec((1, tq, LANES), lambda b, g, i: (b, i, 12 + g)),
            pl.BlockSpec((1, seq, LANES), lambda b, g, i: (b, 0, 14 + g)),
            pl.BlockSpec((1, seq, LANES), lambda b, g, i: (b, 0, 16 + g)),
            pl.BlockSpec((1, tq, LANES), lambda b, g, i: (b, i, 0)),
            pl.BlockSpec((1, seq, LANES), lambda b, g, i: (b, 0, 0)),
        ],
        out_specs=pl.BlockSpec((1, tq, LANES), lambda b, g, i: (b, i, g)),
        out_shape=jax.ShapeDtypeStruct((bsz, seq, FOX_HEADS * HEAD_DIM), jnp.bfloat16),
        scratch_shapes=[pltpu.VMEM((2 * tq, 2 * LANES), jnp.bfloat16),
                        pltpu.VMEM((2 * tq, LANES), jnp.float32),
                        pltpu.VMEM((2 * tq, 2 * LANES), jnp.float32),
                        pltpu.VMEM((2 * tq, tk), jnp.float32),
                        pltpu.VMEM((2 * tq, tk), jnp.float32)],
        compiler_params=_cparams(("parallel", "parallel", "arbitrary")),
        name="fox_attention",
    )(b3, b3, b3, aq3, ak3)


def _score_key(score):
    bits = pltpu.bitcast(score, jnp.int32)
    return bits ^ ((bits >> 31) & 0x7FFFFFFF)


def _dsa_kernel(iq_ref, sq_ref, w_ref, ik_ref, sk_ref, v_ref, o_ref,
                keys_ref, cand_ref, candk_ref, qi_ref, qd_ref, wb_ref, m_ref, acc_ref, s0_ref, s1_ref,
                thr_ref, cnt_ref, flag_ref, *, tq, tc, topk, idx_bits, slots):
    i = pl.program_id(1)
    lane = lax.broadcasted_iota(jnp.int32, (tq, LANES), 1)
    low = lane < HEAD_DIM

    for h in range(IDX_HEADS):
        blk = iq_ref[0, :, (h // 2) * LANES:(h // 2 + 1) * LANES]
        qi_ref[h * tq:(h + 1) * tq, :] = jnp.where(low if h % 2 == 0 else ~low, blk, jnp.zeros_like(blk))
    dsa_order = (0, 2, 1, 3)
    for slab, h in enumerate(dsa_order):
        blk = sq_ref[0, :, (h // 2) * LANES:(h // 2 + 1) * LANES]
        qd_ref[slab * tq:(slab + 1) * tq, :] = jnp.where(low if h % 2 == 0 else ~low, blk, jnp.zeros_like(blk))
    wt = w_ref[0]
    for h in range(IDX_HEADS):
        col = jnp.sum(jnp.where(lane == W_LANE0 + h, wt, 0.0), axis=1, keepdims=True)
        wb_ref[h] = jnp.broadcast_to(col, (tq, LANES))

    nlast = (i * tq) // tc
    nch = nlast + 1
    cand_ref[...] = jnp.full(cand_ref.shape, -jnp.inf, jnp.float32)
    row_pos = i * tq + lax.broadcasted_iota(jnp.int32, (tq, tc), 0)
    col_iota = lax.broadcasted_iota(jnp.int32, (tq, tc), 1)
    idx_groups = (slice(0, 4 * tq), slice(4 * tq, 8 * tq))
    dsa_groups = (slice(0, 2 * tq), slice(2 * tq, 4 * tq))

    def rep(x):
        return pltpu.repeat(x, tc // LANES, axis=1)

    def idx_produce(j, s_ref):
        start = pl.multiple_of(j * tc, tc)
        ik = ik_ref[0, pl.ds(start, tc), :]
        for rows in idx_groups:
            s_ref[rows, :] = lax.dot_general(qi_ref[rows, :], ik, _NT, preferred_element_type=jnp.float32)

    def idx_consume(j, s_ref, last):
        start = pl.multiple_of(j * tc, tc)
        score = jnp.zeros((tq, tc), jnp.float32)
        for h in range(IDX_HEADS):
            score = score + jnp.maximum(s_ref[h * tq:(h + 1) * tq, :], 0.0) * rep(wb_ref[h])
        if last:
            score = jnp.where(start + col_iota <= row_pos, score, -jnp.inf)
        keys_ref[:, pl.ds(start, tc)] = _score_key(score)
        for rg in range(tq // 8):
            rs = slice(rg * 8, (rg + 1) * 8)
            xs = [score[rs, u * LANES:(u + 1) * LANES] for u in range(4)]
            for p, q in _SORT4:
                xs[p], xs[q] = jnp.maximum(xs[p], xs[q]), jnp.minimum(xs[p], xs[q])
            c = [cand_ref[rs, sl * LANES:(sl + 1) * LANES] for sl in range(12)]
            for t in range(4):
                c[8 + t] = jnp.maximum(c[8 + t], xs[3 - t])
            for p, q in _BITONIC12:
                c[p], c[q] = jnp.maximum(c[p], c[q]), jnp.minimum(c[p], c[q])
            for sl in range(12):
                cand_ref[rs, sl * LANES:(sl + 1) * LANES] = c[sl]

    _pipelined_chunks(nlast, idx_produce, idx_consume, s0_ref, s1_ref)

    def count(pred):
        def body(c, acc):
            start = pl.multiple_of(c * tc, tc)
            hit = pred(keys_ref[:, pl.ds(start, tc)], start).astype(jnp.int32)
            for u in range(tc // LANES):
                acc = acc + hit[:, u * LANES:(u + 1) * LANES]
            return acc
        part = lax.fori_loop(0, nch, body, jnp.zeros((tq, LANES), jnp.int32))
        return jnp.broadcast_to(jnp.sum(part, axis=1, keepdims=True), (tq, LANES))

    n_row = i * tq + lax.broadcasted_iota(jnp.int32, (tq, LANES), 0) + 1

    def search_all_keys():
        def pending(state):
            bi, thr, cnt_thr = state
            return jnp.logical_and(bi < 32, jnp.max(jnp.where(n_row > topk, cnt_thr, topk)) > topk)

        def bit_step(state):
            bi, thr, cnt_thr = state
            cand = thr + lax.shift_left(jnp.int32(1), 31 - bi)
            cnt = count(lambda kc, start: kc >= rep(cand))
            ok = cnt >= topk
            return bi + 1, jnp.where(ok, cand, thr), jnp.where(ok, cnt, cnt_thr)

        total = jnp.zeros((tq, LANES), jnp.int32) + nch * tc
        init = (jnp.int32(0), jnp.full((tq, LANES), INT_MIN, jnp.int32), total)
        _, thr, cnt_thr = lax.while_loop(pending, bit_step, init)
        thr_ref[...] = thr
        cnt_ref[...] = cnt_thr

    def search_candidates():
        def pair_step(p, state):
            thr, cnt_thr = state
            one_lo = lax.shift_left(jnp.int32(1), 30 - 2 * p)
            cands = [thr + one_lo, thr + 2 * one_lo, thr + 3 * one_lo]
            parts = [jnp.zeros((tq, LANES), jnp.int32) for _ in cands]
            for sl in range(slots):
                piece = candk_ref[:, sl * LANES:(sl + 1) * LANES]
                parts = [jnp.where(piece >= c, sl + 1, part) for c, part in zip(cands, parts)]
            for c, part in zip(cands, parts):
                cnt = jnp.broadcast_to(jnp.sum(part, axis=1, keepdims=True), (tq, LANES))
                ok = cnt >= topk
                thr = jnp.where(ok, c, thr)
                cnt_thr = jnp.where(ok, cnt, cnt_thr)
            return thr, cnt_thr

        init = (jnp.full((tq, LANES), INT_MIN, jnp.int32), jnp.full((tq, LANES), slots * LANES, jnp.int32))
        thr, cnt_thr = lax.fori_loop(0, 16, pair_step, init, unroll=4)
        thr_ref[...] = thr
        cnt_ref[...] = cnt_thr

    cand_chunks = slots * LANES // tc
    flag_ref[0] = 1

    @pl.when(nch > cand_chunks)
    def _():
        candk_ref[...] = _score_key(cand_ref[...])
        search_candidates()
        kept_min = jnp.max(candk_ref[:, (slots - 1) * LANES:slots * LANES], axis=1, keepdims=True)
        unsafe = jnp.where(kept_min >= thr_ref[...], 1, 0)
        flag_ref[0] = jnp.max(unsafe)

    @pl.when(flag_ref[0] > 0)
    def _():
        search_all_keys()

    thr = thr_ref[...]
    cnt_thr = cnt_ref[...]

    excess = jnp.where(thr > KEY_NEG_INF, cnt_thr - topk, 0)

    @pl.when(jnp.max(excess) > 0)
    def _():
        c_gt = count(lambda kc, start: kc > rep(thr))
        keep = topk - c_gt

        def idx_step(bi, x):
            cand = x + lax.shift_left(jnp.int32(1), idx_bits - 1 - bi)
            below = count(lambda kc, start: (kc == rep(thr)) & (start + col_iota < rep(cand)))
            return jnp.where(below < keep, cand, x)

        last = lax.fori_loop(0, idx_bits, idx_step, jnp.zeros((tq, LANES), jnp.int32))

        def demote(c, carry):
            start = pl.multiple_of(c * tc, tc)
            kc = keys_ref[:, pl.ds(start, tc)]
            drop = (rep(excess) > 0) & (kc == rep(thr)) & (start + col_iota > rep(last))
            keys_ref[:, pl.ds(start, tc)] = jnp.where(drop, kc - 1, kc)
            return carry

        lax.fori_loop(0, nch, demote, 0)

    sel_thr = jnp.maximum(thr, KEY_NEG_INF + 1)

    _flash_init(m_ref, acc_ref)
    vlow = lax.broadcasted_iota(jnp.int32, (tc, LANES), 1) < HEAD_DIM

    def att_produce(j, s_ref):
        start = pl.multiple_of(j * tc, tc)
        sk = sk_ref[0, pl.ds(start, tc), :]
        for rows in dsa_groups:
            s_ref[rows, :] = lax.dot_general(qd_ref[rows, :], sk, _NT, preferred_element_type=jnp.float32)

    def att_consume(j, s_ref, last):
        start = pl.multiple_of(j * tc, tc)
        v2 = v_ref[0, pl.ds(start, tc), :]
        one = jnp.ones_like(v2)
        v_even = jnp.where(vlow, v2, one)
        v_odd = jnp.where(vlow, one, v2)
        sel = keys_ref[:, pl.ds(start, tc)] >= rep(sel_thr)
        for rows, v1 in zip(dsa_groups, (v_even, v_odd)):
            s = s_ref[rows, :].reshape(2, tq, tc)
            s = jnp.where(sel[None], s, -jnp.inf).reshape(2 * tq, tc)
            _flash_rows(s, v1, m_ref, acc_ref, rows)

    _pipelined_chunks(nlast, att_produce, att_consume, s0_ref, s1_ref)

    outs = {}
    for slab, h in enumerate(dsa_order):
        acc = acc_ref[slab * tq:(slab + 1) * tq, :]
        outs[h] = acc / pltpu.roll(acc, HEAD_DIM, axis=1)
    for p in range(DSA_HEADS // 2):
        o_ref[0, :, p * LANES:(p + 1) * LANES] = jnp.where(low, outs[2 * p], outs[2 * p + 1]).astype(o_ref.dtype)


def _dsa_attention(a3, b3, w3, tq=128):
    bsz, seq, _ = a3.shape
    tc, slots = DSA_SCORE_CHUNK, DSA_SLOTS
    topk = min(DSA_TOPK_MAX, seq // 4)
    idx_bits = int(math.log2(seq))
    assert 2 ** idx_bits == seq and tc % tq == 0 and (slots * LANES) % tc == 0 and seq % tc == 0
    kern = functools.partial(_dsa_kernel, tq=tq, tc=tc, topk=topk, idx_bits=idx_bits, slots=slots)
    return pl.pallas_call(
        kern,
        grid=(bsz, seq // tq),
        in_specs=[
            pl.BlockSpec((1, tq, 4 * LANES), lambda b, i: (b, i, 2)),
            pl.BlockSpec((1, tq, 2 * LANES), lambda b, i: (b, i, 6)),
            pl.BlockSpec((1, tq, LANES), lambda b, i: (b, i, 0)),
            pl.BlockSpec((1, seq, LANES), lambda b, i: (b, 0, 14)),
            pl.BlockSpec((1, seq, LANES), lambda b, i: (b, 0, 15)),
            pl.BlockSpec((1, seq, LANES), lambda b, i: (b, 0, 18)),
        ],
        out_specs=pl.BlockSpec((1, tq, 2 * LANES), lambda b, i: (b, i, 0)),
        out_shape=jax.ShapeDtypeStruct((bsz, seq, DSA_HEADS * HEAD_DIM), jnp.bfloat16),
        scratch_shapes=[pltpu.VMEM((tq, seq), jnp.int32),
                        pltpu.VMEM((tq, slots * LANES), jnp.float32),
                        pltpu.VMEM((tq, slots * LANES), jnp.int32),
                        pltpu.VMEM((IDX_HEADS * tq, LANES), jnp.bfloat16),
                        pltpu.VMEM((DSA_HEADS * tq, LANES), jnp.bfloat16),
                        pltpu.VMEM((IDX_HEADS, tq, LANES), jnp.float32),
                        pltpu.VMEM((DSA_HEADS * tq, LANES), jnp.float32),
                        pltpu.VMEM((DSA_HEADS * tq, LANES), jnp.float32),
                        pltpu.VMEM((IDX_HEADS * tq, tc), jnp.float32),
                        pltpu.VMEM((IDX_HEADS * tq, tc), jnp.float32),
                        pltpu.VMEM((tq, LANES), jnp.int32),
                        pltpu.VMEM((tq, LANES), jnp.int32),
                        pltpu.SMEM((1,), jnp.int32)],
        compiler_params=_cparams(("parallel", "arbitrary")),
        name="dsa_attention",
    )(a3, a3, w3, a3, a3, b3)


def _out_kernel(x_ref, gate_ref, a_ref, bo_ref, co_ref, wa_ref, wb_ref, wc_ref, g_ref, b_ref, o_ref):
    gate = gate_ref[...].astype(jnp.float32)
    silu = gate / (1.0 + jnp.exp(-gate))

    def gated(ref, lo, hi):
        return (ref[...].astype(jnp.float32) * silu[:, lo:hi]).astype(jnp.bfloat16)

    out = jnp.dot(gated(a_ref, 0, 512), wa_ref[...], preferred_element_type=jnp.float32)
    out = out + jnp.dot(gated(bo_ref, 512, 768), wb_ref[...], preferred_element_type=jnp.float32)
    out = out + jnp.dot(gated(co_ref, 768, 1024), wc_ref[...], preferred_element_type=jnp.float32)
    y = DEEPNORM_ALPHA * x_ref[...] + out
    mu = jnp.mean(y, axis=1, keepdims=True)
    d = y - mu
    var = jnp.mean(d * d, axis=1, keepdims=True)
    o_ref[...] = d * lax.rsqrt(var + LN_EPS) * g_ref[...] + b_ref[...]


def _out_proj(x2d, b2d, a2d, bo2d, co2d, w_out, ln_g, ln_b, tm=512):
    m = x2d.shape[0]
    row = lambda i: (i, 0)
    fixed = lambda i: (0, 0)
    return pl.pallas_call(
        _out_kernel,
        grid=(m // tm,),
        in_specs=[
            pl.BlockSpec((tm, D_MODEL), row),
            pl.BlockSpec((tm, D_MODEL), row),
            pl.BlockSpec((tm, 512), row),
            pl.BlockSpec((tm, 256), row),
            pl.BlockSpec((tm, 256), row),
            pl.BlockSpec((512, D_MODEL), fixed),
            pl.BlockSpec((256, D_MODEL), lambda i: (2, 0)),
            pl.BlockSpec((256, D_MODEL), lambda i: (3, 0)),
            pl.BlockSpec((1, D_MODEL), fixed),
            pl.BlockSpec((1, D_MODEL), fixed),
        ],
        out_specs=pl.BlockSpec((tm, D_MODEL), row),
        out_shape=jax.ShapeDtypeStruct((m, D_MODEL), jnp.float32),
        compiler_params=_cparams(("parallel",)),
        name="out_proj",
    )(x2d, b2d, a2d, bo2d, co2d, w_out, w_out, w_out, ln_g, ln_b)


def _rope_tables(seq):
    inv = ROPE_THETA ** (-jnp.arange(0, HEAD_DIM, 2, dtype=jnp.float32) / HEAD_DIM)
    ang = jnp.arange(seq, dtype=jnp.int32).astype(jnp.float32)[:, None] * inv[None, :]
    reps = LANES // (HEAD_DIM // 2)
    return jnp.tile(jnp.cos(ang), (1, reps)), jnp.tile(jnp.sin(ang), (1, reps))


def kernel(x, w_in, b_f, lam_q1, lam_k1, lam_q2, lam_k2, g_subln, w_out, ln_g, ln_b):
    bsz, seq, _ = x.shape
    m = bsz * seq
    cos, sin = _rope_tables(seq)

    w_a = w_in[:, :, A_IDX] * A_SCALE
    w_ar = (w_a[:, :, A_ROT_PARTNER] * A_ROT_SIGN).astype(jnp.bfloat16)
    w_a = w_a.astype(jnp.bfloat16)
    w_b = (w_in[:, :, B_IDX] * B_SCALE).astype(jnp.bfloat16)
    w_c = (w_in[:, :, C_IDX] * C_SCALE).astype(jnp.bfloat16)
    w_o = w_out.astype(jnp.bfloat16)
    bf_rows = jnp.zeros((DEPTH, 1, LANES), jnp.float32).at[:, 0, F_LANE0:F_LANE0 + FOX_HEADS].set(b_f)

    x2d = x.reshape(m, D_MODEL)
    for l in range(DEPTH):
        lam_init = 0.8 - 0.6 * math.exp(-0.3 * l)
        lam_rows = jnp.zeros((8, LANES), jnp.float32)
        lam_rows = lam_rows.at[0, :HEAD_DIM].set(lam_q1[l]).at[1, :HEAD_DIM].set(lam_k1[l])
        lam_rows = lam_rows.at[2, :HEAD_DIM].set(lam_q2[l]).at[3, :HEAD_DIM].set(lam_k2[l])
        lam_rows = lam_rows.at[4, :].set(lam_init)

        a2d = _proj_rope(x2d, w_a[l], w_ar[l], cos, sin, seq)
        b2d = _proj_plain(x2d, w_b[l])
        wf2d, aq2d, ak2d = _proj_small(x2d, w_c[l], bf_rows[l], seq)

        a3 = a2d.reshape(bsz, seq, A_WIDTH)
        b3 = b2d.reshape(bsz, seq, B_WIDTH)
        wf3 = wf2d.reshape(bsz, seq, LANES)

        diff_o = _diff_attention(a3, b3, lam_rows, g_subln[l].reshape(1, LANES))
        fox_o = _fox_attention(b3, aq2d.reshape(bsz, seq, LANES), ak2d.reshape(bsz, seq, LANES))
        dsa_o = _dsa_attention(a3, b3, wf3)

        x2d = _out_proj(x2d, b2d, diff_o.reshape(m, 512), dsa_o.reshape(m, 256), fox_o.reshape(m, 256),
                        w_o[l], ln_g[l].reshape(1, D_MODEL), ln_b[l].reshape(1, D_MODEL))
    return x2d.reshape(bsz, seq, D_MODEL)
```

```python
import functools
import math

import numpy as np
import jax
import jax.numpy as jnp
from jax import lax
from jax.experimental import pallas as pl
from jax.experimental.pallas import tpu as pltpu

D_MODEL = 1024
DEPTH = 4
HEAD_DIM = 64
DIFF_HEADS = 4
DSA_HEADS = 4
IDX_HEADS = 8
FOX_HEADS = 4
DSA_TOPK_MAX = 256
ROPE_THETA = 10000.0
LN_EPS = 1e-5
SUBLN_EPS = 1e-5
DEEPNORM_ALPHA = (2 * DEPTH) ** 0.25
IDX_WEIGHT_SCALE = (IDX_HEADS * HEAD_DIM) ** -0.5
LOG2E = math.log2(math.e)
Q_SCALE = HEAD_DIM ** -0.5 * LOG2E

LANES = 128
VMEM_LIMIT_BYTES = 56 * 1024 * 1024

_OFF = {}
_o = 0
for _name, _n in (('diff_q', 512), ('diff_k', 512), ('diff_v', 512), ('dsa_q', 256), ('dsa_k', 64),
                  ('dsa_v', 64), ('idx_q', 512), ('idx_k', 64), ('idx_w', 8), ('fox_q', 256),
                  ('fox_k', 256), ('fox_v', 256), ('fox_f', 4), ('gate', 1024)):
    _OFF[_name] = (_o, _n)
    _o += _n
IN_WIDTH = _o


def _cols(name, scale=1.0, repeat=1):
    o, n = _OFF[name]
    idx = np.tile(np.arange(o, o + n), repeat)
    return idx, np.full(idx.shape, scale, np.float32)


def _pad(n):
    return np.zeros((n,), np.int64), np.zeros((n,), np.float32)


def _layout(parts):
    idx = np.concatenate([p[0] for p in parts])
    scale = np.concatenate([p[1] for p in parts])
    return idx, scale


A_IDX, A_SCALE = _layout([_cols('diff_q', Q_SCALE), _cols('diff_k'), _cols('idx_q'),
                          _cols('dsa_q', Q_SCALE), _cols('idx_k', repeat=2), _cols('dsa_k', repeat=2)])
A_WIDTH = A_IDX.shape[0]
B_IDX, B_SCALE = _layout([_cols('gate'), _cols('diff_v'), _cols('fox_q', Q_SCALE), _cols('fox_k'),
                          _cols('fox_v'), _cols('dsa_v', repeat=2), _pad(128)])
B_WIDTH = B_IDX.shape[0]
C_IDX, C_SCALE = _layout([_cols('idx_w'), _cols('fox_f'), _pad(LANES - 12)])
W_LANE0 = 0
F_LANE0 = 8

_c = np.arange(A_WIDTH)
_first_half = (_c % HEAD_DIM) < HEAD_DIM // 2
A_ROT_PARTNER = np.where(_first_half, _c + HEAD_DIM // 2, _c - HEAD_DIM // 2)
A_ROT_SIGN = np.where(_first_half, -1.0, 1.0).astype(np.float32)

_NT = (((1,), (1,)), ((), ()))
INT_MIN = -2 ** 31
KEY_NEG_INF = -2139095041

_SORT4 = ((0, 1), (2, 3), (0, 2), (1, 3), (1, 2))
_BITONIC12 = ((0, 8), (1, 9), (2, 10), (3, 11), (4, 8), (5, 9), (6, 10), (7, 11),
              (0, 2), (1, 3), (4, 6), (5, 7), (8, 10), (9, 11),
              (0, 1), (2, 3), (4, 5), (6, 7), (8, 9), (10, 11))
DSA_SLOTS = 12
DSA_SCORE_CHUNK = 512


def _cparams(sem):
    return pltpu.CompilerParams(dimension_semantics=sem, vmem_limit_bytes=VMEM_LIMIT_BYTES)


PROJ_COLS = 512


def _proj_rope_kernel(x_ref, w_ref, wr_ref, cos_ref, sin_ref, o_ref):
    xb = x_ref[...].astype(jnp.bfloat16)
    cos = cos_ref[...]
    sin = sin_ref[...]
    for n in range(o_ref.shape[1] // PROJ_COLS):
        cols = slice(n * PROJ_COLS, (n + 1) * PROJ_COLS)
        h = jnp.dot(xb, w_ref[:, cols], preferred_element_type=jnp.float32)
        hr = jnp.dot(xb, wr_ref[:, cols], preferred_element_type=jnp.float32)
        for c in range(PROJ_COLS // LANES):
            sl = slice(c * LANES, (c + 1) * LANES)
            o_ref[:, n * PROJ_COLS + c * LANES:n * PROJ_COLS + (c + 1) * LANES] = (
                h[:, sl] * cos + hr[:, sl] * sin).astype(o_ref.dtype)


def _proj_rope(x2d, w, wr, cos, sin, seq, tm=512):
    m = x2d.shape[0]
    nseq = seq // tm
    return pl.pallas_call(
        _proj_rope_kernel,
        grid=(m // tm,),
        in_specs=[
            pl.BlockSpec((tm, D_MODEL), lambda i: (i, 0)),
            pl.BlockSpec((D_MODEL, A_WIDTH), lambda i: (0, 0)),
            pl.BlockSpec((D_MODEL, A_WIDTH), lambda i: (0, 0)),
            pl.BlockSpec((tm, LANES), lambda i: (i % nseq, 0)),
            pl.BlockSpec((tm, LANES), lambda i: (i % nseq, 0)),
        ],
        out_specs=pl.BlockSpec((tm, A_WIDTH), lambda i: (i, 0)),
        out_shape=jax.ShapeDtypeStruct((m, A_WIDTH), jnp.bfloat16),
        compiler_params=_cparams(("parallel",)),
        name="proj_rope",
    )(x2d, w, wr, cos, sin)


def _proj_plain_kernel(x_ref, w_ref, o_ref):
    xb = x_ref[...].astype(jnp.bfloat16)
    for n in range(o_ref.shape[1] // PROJ_COLS):
        cols = slice(n * PROJ_COLS, (n + 1) * PROJ_COLS)
        o_ref[:, cols] = jnp.dot(xb, w_ref[:, cols], preferred_element_type=jnp.float32).astype(o_ref.dtype)


def _proj_plain(x2d, w, tm=512):
    m = x2d.shape[0]
    return pl.pallas_call(
        _proj_plain_kernel,
        grid=(m // tm,),
        in_specs=[
            pl.BlockSpec((tm, D_MODEL), lambda i: (i, 0)),
            pl.BlockSpec((D_MODEL, B_WIDTH), lambda i: (0, 0)),
        ],
        out_specs=pl.BlockSpec((tm, B_WIDTH), lambda i: (i, 0)),
        out_shape=jax.ShapeDtypeStruct((m, B_WIDTH), jnp.bfloat16),
        compiler_params=_cparams(("parallel",)),
        name="proj_plain",
    )(x2d, w)


def _split3(v):
    hi = v.astype(jnp.bfloat16)
    r = v - hi.astype(jnp.float32)
    mid = r.astype(jnp.bfloat16)
    lo = (r - mid.astype(jnp.float32)).astype(jnp.bfloat16)
    return hi, mid, lo


def _gate_aug_constants():
    place = np.zeros((6, LANES, LANES), np.float32)
    ones = np.zeros((8, LANES), np.float32)
    for h in range(FOX_HEADS):
        for j in range(3):
            place[j, F_LANE0 + h, 8 * h + j] = 1.0
            place[3 + j, F_LANE0 + h, 8 * h + 3 + j] = -1.0
            ones[0, 8 * h + 3 + j] = 1.0
            ones[1, 8 * h + j] = 1.0
    return jnp.asarray(place, jnp.bfloat16), jnp.asarray(ones)


def _proj_small_kernel(x_ref, w_ref, bf_ref, tri_ref, place_ref, ones_ref, o_ref, aq_ref, ak_ref, carry_ref,
                       *, blocks_per_seq):
    i = pl.program_id(0)
    xb = x_ref[...].astype(jnp.bfloat16)
    h = jnp.dot(xb, w_ref[...], preferred_element_type=jnp.float32)
    lane = lax.broadcasted_iota(jnp.int32, h.shape, 1)
    ff = h + bf_ref[...]
    logf = jnp.minimum(ff, 0.0) - jnp.log(1.0 + jnp.exp(-jnp.abs(ff)))
    o = jnp.where(lane < F_LANE0, h * IDX_WEIGHT_SCALE, logf * LOG2E)
    o_ref[...] = o

    @pl.when(i % blocks_per_seq == 0)
    def _():
        carry_ref[...] = jnp.zeros_like(carry_ref)

    tri = tri_ref[...]
    hi, mid, lo = _split3(o)
    cs = (jnp.dot(tri, hi, preferred_element_type=jnp.float32)
          + jnp.dot(tri, mid, preferred_element_type=jnp.float32)
          + jnp.dot(tri, lo, preferred_element_type=jnp.float32))
    cs = cs + carry_ref[0:1, :]
    tm = cs.shape[0]
    carry_ref[...] = jnp.broadcast_to(cs[tm - 1:tm, :], carry_ref.shape)
    parts = _split3(cs)
    aq = ones_ref[0:1, :]
    ak = ones_ref[1:2, :]
    for j in range(3):
        aq = aq + jnp.dot(parts[j], place_ref[j], preferred_element_type=jnp.float32)
        ak = ak + jnp.dot(parts[j], place_ref[3 + j], preferred_element_type=jnp.float32)
    aq_ref[...] = aq.astype(aq_ref.dtype)
    ak_ref[...] = ak.astype(ak_ref.dtype)


def _proj_small(x2d, w, bf_row, seq, tm=512):
    m = x2d.shape[0]
    tri = jnp.asarray(np.tril(np.ones((tm, tm), np.float32)), jnp.bfloat16)
    place, ones = _gate_aug_constants()
    kern = functools.partial(_proj_small_kernel, blocks_per_seq=seq // tm)
    row = lambda i: (i, 0)
    return pl.pallas_call(
        kern,
        grid=(m // tm,),
        in_specs=[
            pl.BlockSpec((tm, D_MODEL), row),
            pl.BlockSpec((D_MODEL, LANES), lambda i: (0, 0)),
            pl.BlockSpec((1, LANES), lambda i: (0, 0)),
            pl.BlockSpec((tm, tm), lambda i: (0, 0)),
            pl.BlockSpec((6, LANES, LANES), lambda i: (0, 0, 0)),
            pl.BlockSpec((8, LANES), lambda i: (0, 0)),
        ],
        out_specs=[pl.BlockSpec((tm, LANES), row), pl.BlockSpec((tm, LANES), row), pl.BlockSpec((tm, LANES), row)],
        out_shape=[jax.ShapeDtypeStruct((m, LANES), jnp.float32),
                   jax.ShapeDtypeStruct((m, LANES), jnp.bfloat16),
                   jax.ShapeDtypeStruct((m, LANES), jnp.bfloat16)],
        scratch_shapes=[pltpu.VMEM((8, LANES), jnp.float32)],
        compiler_params=_cparams(("arbitrary",)),
        name="proj_small",
    )(x2d, w, bf_row, tri, place, ones)


def _flash_init(m_ref, acc_ref):
    m_ref[...] = jnp.full(m_ref.shape, -1e30, jnp.float32)
    acc_ref[...] = jnp.zeros(acc_ref.shape, jnp.float32)


def _flash_rows(s, v1, m_ref, acc_ref, rows):
    tk = s.shape[1]
    m_prev = m_ref[rows, :]
    m_next = jnp.maximum(m_prev, jnp.max(s, axis=1, keepdims=True))
    p = jnp.exp2((s - pltpu.repeat(m_next, tk // LANES, axis=1)).astype(jnp.bfloat16))
    alpha = jnp.exp2(m_prev - m_next)
    m_ref[rows, :] = m_next
    pv = jnp.dot(p, v1, preferred_element_type=jnp.float32)
    acc_ref[rows, :] = acc_ref[rows, :] * pltpu.repeat(alpha, v1.shape[1] // LANES, axis=1) + pv


def _pipelined_chunks(n_full, produce, consume, buf0, buf1):
    produce(0, buf0)

    def body(p, carry):
        produce(2 * p + 1, buf1)
        consume(2 * p, buf0, False)
        produce(2 * p + 2, buf0)
        consume(2 * p + 1, buf1, False)
        return carry

    lax.fori_loop(0, n_full // 2, body, 0)

    @pl.when(n_full % 2 == 1)
    def _():
        produce(n_full, buf1)
        consume(n_full - 1, buf0, False)
        consume(n_full, buf1, True)

    @pl.when(n_full % 2 == 0)
    def _():
        consume(n_full, buf0, True)


def _stack_pair(q, qs_ref, t):
    lane = lax.broadcasted_iota(jnp.int32, q.shape, 1)
    zero = jnp.zeros_like(q)
    qs_ref[0:t, :] = jnp.where(lane < HEAD_DIM, q, zero)
    qs_ref[t:2 * t, :] = jnp.where(lane >= HEAD_DIM, q, zero)


def _with_ones(v):
    return jnp.concatenate([v, jnp.ones((v.shape[0], LANES), v.dtype)], axis=1)


def _diff_kernel(lam_ref, g_ref, q_ref, k_ref, v_ref, o_ref, qs_ref, m_ref, acc_ref, s0_ref, s1_ref,
                 *, tq, tk):
    i = pl.program_id(2)
    _stack_pair(q_ref[0], qs_ref, tq)
    _flash_init(m_ref, acc_ref)
    groups = (slice(0, tq), slice(tq, 2 * tq))

    def produce(j, s_ref):
        start = pl.multiple_of(j * tk, tk)
        k = k_ref[0, pl.ds(start, tk), :]
        for rows in groups:
            s_ref[rows, :] = lax.dot_general(qs_ref[rows, :], k, _NT, preferred_element_type=jnp.float32)

    def consume(j, s_ref, last):
        start = pl.multiple_of(j * tk, tk)
        v1 = _with_ones(v_ref[0, pl.ds(start, tk), :])
        for rows in groups:
            s = s_ref[rows, :]
            if last:
                r = lax.broadcasted_iota(jnp.int32, (tq, tk), 0)
                c = lax.broadcasted_iota(jnp.int32, (tq, tk), 1)
                s = jnp.where(start + c <= i * tq + r, s, -jnp.inf)
            _flash_rows(s, v1, m_ref, acc_ref, rows)

    _pipelined_chunks((i * tq) // tk, produce, consume, s0_ref, s1_ref)

    lam_rows = lam_ref[...]
    s1 = jnp.sum(lam_rows[0:1, :] * lam_rows[1:2, :], axis=1, keepdims=True)
    s2 = jnp.sum(lam_rows[2:3, :] * lam_rows[3:4, :], axis=1, keepdims=True)
    lam_init = lam_rows[4:5, 0:1]
    lam = jnp.exp(s1) - jnp.exp(s2) + lam_init
    o0 = acc_ref[0:tq, 0:LANES] / acc_ref[0:tq, LANES:2 * LANES]
    o1 = acc_ref[tq:2 * tq, 0:LANES] / acc_ref[tq:2 * tq, LANES:2 * LANES]
    a = o0 - lam * o1
    ms = jnp.mean(a * a, axis=1, keepdims=True)
    a = a * lax.rsqrt(ms + SUBLN_EPS) * g_ref[...] * (1.0 - lam_init)
    o_ref[0] = a.astype(o_ref.dtype)


def _diff_attention(a3, b3, lam_rows, g_row, tq=1024, tk=1024):
    bsz, seq, _ = a3.shape
    tk = min(tk, seq)
    tq = min(tq, seq)
    kern = functools.partial(_diff_kernel, tq=tq, tk=tk)
    return pl.pallas_call(
        kern,
        grid=(bsz, DIFF_HEADS, seq // tq),
        in_specs=[
            pl.BlockSpec((8, LANES), lambda b, h, i: (0, 0)),
            pl.BlockSpec((1, LANES), lambda b, h, i: (0, 0)),
            pl.BlockSpec((1, tq, LANES), lambda b, h, i: (b, i, h)),
            pl.BlockSpec((1, seq, LANES), lambda b, h, i: (b, 0, 4 + h)),
            pl.BlockSpec((1, seq, LANES), lambda b, h, i: (b, 0, 8 + h)),
        ],
        out_specs=pl.BlockSpec((1, tq, LANES), lambda b, h, i: (b, i, h)),
        out_shape=jax.ShapeDtypeStruct((bsz, seq, DIFF_HEADS * LANES), jnp.bfloat16),
        scratch_shapes=[pltpu.VMEM((2 * tq, LANES), jnp.bfloat16),
                        pltpu.VMEM((2 * tq, LANES), jnp.float32),
                        pltpu.VMEM((2 * tq, 2 * LANES), jnp.float32),
                        pltpu.VMEM((2 * tq, tk), jnp.float32),
                        pltpu.VMEM((2 * tq, tk), jnp.float32)],
        compiler_params=_cparams(("parallel", "parallel", "arbitrary")),
        name="diff_attention",
    )(lam_rows, g_row, a3, a3, b3)


def _fox_kernel(q_ref, k_ref, v_ref, aq_ref, ak_ref, o_ref, qs_ref, m_ref, acc_ref, s0_ref, s1_ref, *, tq, tk):
    g = pl.program_id(1)
    i = pl.program_id(2)
    q = q_ref[0]
    aq = aq_ref[0]
    lane = lax.broadcasted_iota(jnp.int32, (tq, LANES), 1)
    zero = jnp.zeros_like(q)
    for a in range(2):
        rows = slice(a * tq, (a + 1) * tq)
        head_lanes = (lane < HEAD_DIM) if a == 0 else (lane >= HEAD_DIM)
        first = 8 * (2 * g + a)
        qs_ref[rows, 0:LANES] = jnp.where(head_lanes, q, zero)
        qs_ref[rows, LANES:2 * LANES] = jnp.where((lane >= first) & (lane < first + 6), aq, zero)
    _flash_init(m_ref, acc_ref)
    groups = (slice(0, tq), slice(tq, 2 * tq))

    def produce(j, s_ref):
        start = pl.multiple_of(j * tk, tk)
        k2 = jnp.concatenate([k_ref[0, pl.ds(start, tk), :], ak_ref[0, pl.ds(start, tk), :]], axis=1)
        for rows in groups:
            s_ref[rows, :] = lax.dot_general(qs_ref[rows, :], k2, _NT, preferred_element_type=jnp.float32)

    def consume(j, s_ref, last):
        start = pl.multiple_of(j * tk, tk)
        v1 = _with_ones(v_ref[0, pl.ds(start, tk), :])
        for rows in groups:
            s = s_ref[rows, :]
            if last:
                r = lax.broadcasted_iota(jnp.int32, (tq, tk), 0)
                c = lax.broadcasted_iota(jnp.int32, (tq, tk), 1)
                s = jnp.where(start + c <= i * tq + r, s, -jnp.inf)
            _flash_rows(s, v1, m_ref, acc_ref, rows)

    _pipelined_chunks((i * tq) // tk, produce, consume, s0_ref, s1_ref)

    o0 = acc_ref[0:tq, 0:LANES] / acc_ref[0:tq, LANES:2 * LANES]
    o1 = acc_ref[tq:2 * tq, 0:LANES] / acc_ref[tq:2 * tq, LANES:2 * LANES]
    o_ref[0] = jnp.where(lane < HEAD_DIM, o0, o1).astype(o_ref.dtype)


def _fox_attention(b3, aq3, ak3, tq=512, tk=1024):
    bsz, seq, _ = b3.shape
    tk = min(tk, seq)
    kern = functools.partial(_fox_kernel, tq=tq, tk=tk)
    return pl.pallas_call(
        kern,
        grid=(bsz, FOX_HEADS // 2, seq // tq),
        in_specs=[
            pl.BlockSpec((1, tq, LANES), lambda b, g, i: (b, i, 12 + g)),
            pl.BlockSpec((1, seq, LANES), lambda b, g, i: (b, 0, 14 + g)),
            pl.BlockSpec((1, seq, LANES), lambda b, g, i: (b, 0, 16 + g)),
            pl.BlockSpec((1, tq, LANES), lambda b, g, i: (b, i, 0)),
            pl.BlockSpec((1, seq, LANES), lambda b, g, i: (b, 0, 0)),
        ],
        out_specs=pl.BlockSpec((1, tq, LANES), lambda b, g, i: (b, i, g)),
        out_shape=jax.ShapeDtypeStruct((bsz, seq, FOX_HEADS * HEAD_DIM), jnp.bfloat16),
        scratch_shapes=[pltpu.VMEM((2 * tq, 2 * LANES), jnp.bfloat16),
                        pltpu.VMEM((2 * tq, LANES), jnp.float32),
                        pltpu.VMEM((2 * tq, 2 * LANES), jnp.float32),
                        pltpu.VMEM((2 * tq, tk), jnp.float32),
                        pltpu.VMEM((2 * tq, tk), jnp.float32)],
        compiler_params=_cparams(("parallel", "parallel", "arbitrary")),
        name="fox_attention",
    )(b3, b3, b3, aq3, ak3)


def _score_key(score):
    bits = pltpu.bitcast(score, jnp.int32)
    return bits ^ ((bits >> 31) & 0x7FFFFFFF)


def _dsa_kernel(iq_ref, sq_ref, w_ref, ik_ref, sk_ref, v_ref, o_ref,
                keys_ref, cand_ref, candk_ref, qi_ref, qd_ref, wb_ref, m_ref, acc_ref, s0_ref, s1_ref,
                thr_ref, cnt_ref, flag_ref, *, tq, tc, topk, idx_bits, slots):
    i = pl.program_id(1)
    lane = lax.broadcasted_iota(jnp.int32, (tq, LANES), 1)
    low = lane < HEAD_DIM

    for h in range(IDX_HEADS):
        blk = iq_ref[0, :, (h // 2) * LANES:(h // 2 + 1) * LANES]
        qi_ref[h * tq:(h + 1) * tq, :] = jnp.where(low if h % 2 == 0 else ~low, blk, jnp.zeros_like(blk))
    dsa_order = (0, 2, 1, 3)
    for slab, h in enumerate(dsa_order):
        blk = sq_ref[0, :, (h // 2) * LANES:(h // 2 + 1) * LANES]
        qd_ref[slab * tq:(slab + 1) * tq, :] = jnp.where(low if h % 2 == 0 else ~low, blk, jnp.zeros_like(blk))
    wt = w_ref[0]
    for h in range(IDX_HEADS):
        col = jnp.sum(jnp.where(lane == W_LANE0 + h, wt, 0.0), axis=1, keepdims=True)
        wb_ref[h] = jnp.broadcast_to(col, (tq, LANES))

    nlast = (i * tq) // tc
    nch = nlast + 1
    cand_ref[...] = jnp.full(cand_ref.shape, -jnp.inf, jnp.float32)
    row_pos = i * tq + lax.broadcasted_iota(jnp.int32, (tq, tc), 0)
    col_iota = lax.broadcasted_iota(jnp.int32, (tq, tc), 1)
    idx_groups = (slice(0, 4 * tq), slice(4 * tq, 8 * tq))
    dsa_groups = (slice(0, 2 * tq), slice(2 * tq, 4 * tq))

    def rep(x):
        return pltpu.repeat(x, tc // LANES, axis=1)

    def idx_produce(j, s_ref):
        start = pl.multiple_of(j * tc, tc)
        ik = ik_ref[0, pl.ds(start, tc), :]
        for rows in idx_groups:
            s_ref[rows, :] = lax.dot_general(qi_ref[rows, :], ik, _NT, preferred_element_type=jnp.float32)

    def idx_consume(j, s_ref, last):
        start = pl.multiple_of(j * tc, tc)
        score = jnp.zeros((tq, tc), jnp.float32)
        for h in range(IDX_HEADS):
            score = score + jnp.maximum(s_ref[h * tq:(h + 1) * tq, :], 0.0) * rep(wb_ref[h])
        if last:
            score = jnp.where(start + col_iota <= row_pos, score, -jnp.inf)
        keys_ref[:, pl.ds(start, tc)] = _score_key(score)
        for rg in range(tq // 8):
            rs = slice(rg * 8, (rg + 1) * 8)
            xs = [score[rs, u * LANES:(u + 1) * LANES] for u in range(4)]
            for p, q in _SORT4:
                xs[p], xs[q] = jnp.maximum(xs[p], xs[q]), jnp.minimum(xs[p], xs[q])
            c = [cand_ref[rs, sl * LANES:(sl + 1) * LANES] for sl in range(12)]
            for t in range(4):
                c[8 + t] = jnp.maximum(c[8 + t], xs[3 - t])
            for p, q in _BITONIC12:
                c[p], c[q] = jnp.maximum(c[p], c[q]), jnp.minimum(c[p], c[q])
            for sl in range(12):
                cand_ref[rs, sl * LANES:(sl + 1) * LANES] = c[sl]

    _pipelined_chunks(nlast, idx_produce, idx_consume, s0_ref, s1_ref)

    def count(pred):
        def body(c, acc):
            start = pl.multiple_of(c * tc, tc)
            hit = pred(keys_ref[:, pl.ds(start, tc)], start).astype(jnp.int32)
            for u in range(tc // LANES):
                acc = acc + hit[:, u * LANES:(u + 1) * LANES]
            return acc
        part = lax.fori_loop(0, nch, body, jnp.zeros((tq, LANES), jnp.int32))
        return jnp.broadcast_to(jnp.sum(part, axis=1, keepdims=True), (tq, LANES))

    n_row = i * tq + lax.broadcasted_iota(jnp.int32, (tq, LANES), 0) + 1

    def search_all_keys():
        def pending(state):
            bi, thr, cnt_thr = state
            return jnp.logical_and(bi < 32, jnp.max(jnp.where(n_row > topk, cnt_thr, topk)) > topk)

        def bit_step(state):
            bi, thr, cnt_thr = state
            cand = thr + lax.shift_left(jnp.int32(1), 31 - bi)
            cnt = count(lambda kc, start: kc >= rep(cand))
            ok = cnt >= topk
            return bi + 1, jnp.where(ok, cand, thr), jnp.where(ok, cnt, cnt_thr)

        total = jnp.zeros((tq, LANES), jnp.int32) + nch * tc
        init = (jnp.int32(0), jnp.full((tq, LANES), INT_MIN, jnp.int32), total)
        _, thr, cnt_thr = lax.while_loop(pending, bit_step, init)
        thr_ref[...] = thr
        cnt_ref[...] = cnt_thr

    def search_candidates():
        def pair_step(p, state):
            thr, cnt_thr = state
            one_lo = lax.shift_left(jnp.int32(1), 30 - 2 * p)
            cands = [thr + one_lo, thr + 2 * one_lo, thr + 3 * one_lo]
            parts = [jnp.zeros((tq, LANES), jnp.float32) for _ in cands]
            for sl in range(slots):
                piece = candk_ref[:, sl * LANES:(sl + 1) * LANES]
                parts = [jnp.where(piece >= c, float(sl + 1), part) for c, part in zip(cands, parts)]
            for c, part in zip(cands, parts):
                cnt = jnp.broadcast_to(jnp.sum(part, axis=1, keepdims=True), (tq, LANES))
                ok = cnt >= float(topk)
                thr = jnp.where(ok, c, thr)
                cnt_thr = jnp.where(ok, cnt, cnt_thr)
            return thr, cnt_thr

        init = (jnp.full((tq, LANES), INT_MIN, jnp.int32), jnp.full((tq, LANES), float(slots * LANES), jnp.float32))
        thr, cnt_thr = lax.fori_loop(0, 16, pair_step, init, unroll=4)
        thr_ref[...] = thr
        cnt_ref[...] = cnt_thr.astype(jnp.int32)

    cand_chunks = slots * LANES // tc
    flag_ref[0] = 1

    @pl.when(nch > cand_chunks)
    def _():
        candk_ref[...] = _score_key(cand_ref[...])
        search_candidates()
        kept_min = jnp.max(candk_ref[:, (slots - 1) * LANES:slots * LANES], axis=1, keepdims=True)
        unsafe = jnp.where(kept_min >= thr_ref[...], 1, 0)
        flag_ref[0] = jnp.max(unsafe)

    @pl.when(flag_ref[0] > 0)
    def _():
        search_all_keys()

    thr = thr_ref[...]
    cnt_thr = cnt_ref[...]

    excess = jnp.where(thr > KEY_NEG_INF, cnt_thr - topk, 0)

    @pl.when(jnp.max(excess) > 0)
    def _():
        c_gt = count(lambda kc, start: kc > rep(thr))
        keep = topk - c_gt

        def idx_step(bi, x):
            cand = x + lax.shift_left(jnp.int32(1), idx_bits - 1 - bi)
            below = count(lambda kc, start: (kc == rep(thr)) & (start + col_iota < rep(cand)))
            return jnp.where(below < keep, cand, x)

        last = lax.fori_loop(0, idx_bits, idx_step, jnp.zeros((tq, LANES), jnp.int32))

        def demote(c, carry):
            start = pl.multiple_of(c * tc, tc)
            kc = keys_ref[:, pl.ds(start, tc)]
            drop = (rep(excess) > 0) & (kc == rep(thr)) & (start + col_iota > rep(last))
            keys_ref[:, pl.ds(start, tc)] = jnp.where(drop, kc - 1, kc)
            return carry

        lax.fori_loop(0, nch, demote, 0)

    sel_thr = jnp.maximum(thr, KEY_NEG_INF + 1)

    _flash_init(m_ref, acc_ref)
    vlow = lax.broadcasted_iota(jnp.int32, (tc, LANES), 1) < HEAD_DIM

    def att_produce(j, s_ref):
        start = pl.multiple_of(j * tc, tc)
        sk = sk_ref[0, pl.ds(start, tc), :]
        for rows in dsa_groups:
            s_ref[rows, :] = lax.dot_general(qd_ref[rows, :], sk, _NT, preferred_element_type=jnp.float32)

    def att_consume(j, s_ref, last):
        start = pl.multiple_of(j * tc, tc)
        v2 = v_ref[0, pl.ds(start, tc), :]
        one = jnp.ones_like(v2)
        v_even = jnp.where(vlow, v2, one)
        v_odd = jnp.where(vlow, one, v2)
        sel = keys_ref[:, pl.ds(start, tc)] >= rep(sel_thr)
        for rows, v1 in zip(dsa_groups, (v_even, v_odd)):
            s = s_ref[rows, :].reshape(2, tq, tc)
            s = jnp.where(sel[None], s, -jnp.inf).reshape(2 * tq, tc)
            _flash_rows(s, v1, m_ref, acc_ref, rows)

    _pipelined_chunks(nlast, att_produce, att_consume, s0_ref, s1_ref)

    outs = {}
    for slab, h in enumerate(dsa_order):
        acc = acc_ref[slab * tq:(slab + 1) * tq, :]
        outs[h] = acc / pltpu.roll(acc, HEAD_DIM, axis=1)
    for p in range(DSA_HEADS // 2):
        o_ref[0, :, p * LANES:(p + 1) * LANES] = jnp.where(low, outs[2 * p], outs[2 * p + 1]).astype(o_ref.dtype)


def _dsa_attention(a3, b3, w3, tq=128):
    bsz, seq, _ = a3.shape
    tc, slots = DSA_SCORE_CHUNK, DSA_SLOTS
    topk = min(DSA_TOPK_MAX, seq // 4)
    idx_bits = int(math.log2(seq))
    assert 2 ** idx_bits == seq and tc % tq == 0 and (slots * LANES) % tc == 0 and seq % tc == 0
    kern = functools.partial(_dsa_kernel, tq=tq, tc=tc, topk=topk, idx_bits=idx_bits, slots=slots)
    return pl.pallas_call(
        kern,
        grid=(bsz, seq // tq),
        in_specs=[
            pl.BlockSpec((1, tq, 4 * LANES), lambda b, i: (b, i, 2)),
            pl.BlockSpec((1, tq, 2 * LANES), lambda b, i: (b, i, 6)),
            pl.BlockSpec((1, tq, LANES), lambda b, i: (b, i, 0)),
            pl.BlockSpec((1, seq, LANES), lambda b, i: (b, 0, 14)),
            pl.BlockSpec((1, seq, LANES), lambda b, i: (b, 0, 15)),
            pl.BlockSpec((1, seq, LANES), lambda b, i: (b, 0, 18)),
        ],
        out_specs=pl.BlockSpec((1, tq, 2 * LANES), lambda b, i: (b, i, 0)),
        out_shape=jax.ShapeDtypeStruct((bsz, seq, DSA_HEADS * HEAD_DIM), jnp.bfloat16),
        scratch_shapes=[pltpu.VMEM((tq, seq), jnp.int32),
                        pltpu.VMEM((tq, slots * LANES), jnp.float32),
                        pltpu.VMEM((tq, slots * LANES), jnp.int32),
                        pltpu.VMEM((IDX_HEADS * tq, LANES), jnp.bfloat16),
                        pltpu.VMEM((DSA_HEADS * tq, LANES), jnp.bfloat16),
                        pltpu.VMEM((IDX_HEADS, tq, LANES), jnp.float32),
                        pltpu.VMEM((DSA_HEADS * tq, LANES), jnp.float32),
                        pltpu.VMEM((DSA_HEADS * tq, LANES), jnp.float32),
                        pltpu.VMEM((IDX_HEADS * tq, tc), jnp.float32),
                        pltpu.VMEM((IDX_HEADS * tq, tc), jnp.float32),
                        pltpu.VMEM((tq, LANES), jnp.int32),
                        pltpu.VMEM((tq, LANES), jnp.int32),
                        pltpu.SMEM((1,), jnp.int32)],
        compiler_params=_cparams(("parallel", "arbitrary")),
        name="dsa_attention",
    )(a3, a3, w3, a3, a3, b3)


def _out_kernel(x_ref, gate_ref, a_ref, bo_ref, co_ref, wa_ref, wb_ref, wc_ref, g_ref, b_ref, o_ref):
    gate = gate_ref[...].astype(jnp.float32)
    silu = gate / (1.0 + jnp.exp(-gate))

    def gated(ref, lo, hi):
        return (ref[...].astype(jnp.float32) * silu[:, lo:hi]).astype(jnp.bfloat16)

    out = jnp.dot(gated(a_ref, 0, 512), wa_ref[...], preferred_element_type=jnp.float32)
    out = out + jnp.dot(gated(bo_ref, 512, 768), wb_ref[...], preferred_element_type=jnp.float32)
    out = out + jnp.dot(gated(co_ref, 768, 1024), wc_ref[...], preferred_element_type=jnp.float32)
    y = DEEPNORM_ALPHA * x_ref[...] + out
    mu = jnp.mean(y, axis=1, keepdims=True)
    d = y - mu
    var = jnp.mean(d * d, axis=1, keepdims=True)
    o_ref[...] = d * lax.rsqrt(var + LN_EPS) * g_ref[...] + b_ref[...]


def _out_proj(x2d, b2d, a2d, bo2d, co2d, w_out, ln_g, ln_b, tm=512):
    m = x2d.shape[0]
    row = lambda i: (i, 0)
    fixed = lambda i: (0, 0)
    return pl.pallas_call(
        _out_kernel,
        grid=(m // tm,),
        in_specs=[
            pl.BlockSpec((tm, D_MODEL), row),
            pl.BlockSpec((tm, D_MODEL), row),
            pl.BlockSpec((tm, 512), row),
            pl.BlockSpec((tm, 256), row),
            pl.BlockSpec((tm, 256), row),
            pl.BlockSpec((512, D_MODEL), fixed),
            pl.BlockSpec((256, D_MODEL), lambda i: (2, 0)),
            pl.BlockSpec((256, D_MODEL), lambda i: (3, 0)),
            pl.BlockSpec((1, D_MODEL), fixed),
            pl.BlockSpec((1, D_MODEL), fixed),
        ],
        out_specs=pl.BlockSpec((tm, D_MODEL), row),
        out_shape=jax.ShapeDtypeStruct((m, D_MODEL), jnp.float32),
        compiler_params=_cparams(("parallel",)),
        name="out_proj",
    )(x2d, b2d, a2d, bo2d, co2d, w_out, w_out, w_out, ln_g, ln_b)


def _rope_tables(seq):
    inv = ROPE_THETA ** (-jnp.arange(0, HEAD_DIM, 2, dtype=jnp.float32) / HEAD_DIM)
    ang = jnp.arange(seq, dtype=jnp.int32).astype(jnp.float32)[:, None] * inv[None, :]
    reps = LANES // (HEAD_DIM // 2)
    return jnp.tile(jnp.cos(ang), (1, reps)), jnp.tile(jnp.sin(ang), (1, reps))


def kernel(x, w_in, b_f, lam_q1, lam_k1, lam_q2, lam_k2, g_subln, w_out, ln_g, ln_b):
    bsz, seq, _ = x.shape
    m = bsz * seq
    cos, sin = _rope_tables(seq)

    w_a = w_in[:, :, A_IDX] * A_SCALE
    w_ar = (w_a[:, :, A_ROT_PARTNER] * A_ROT_SIGN).astype(jnp.bfloat16)
    w_a = w_a.astype(jnp.bfloat16)
    w_b = (w_in[:, :, B_IDX] * B_SCALE).astype(jnp.bfloat16)
    w_c = (w_in[:, :, C_IDX] * C_SCALE).astype(jnp.bfloat16)
    w_o = w_out.astype(jnp.bfloat16)
    bf_rows = jnp.zeros((DEPTH, 1, LANES), jnp.float32).at[:, 0, F_LANE0:F_LANE0 + FOX_HEADS].set(b_f)

    x2d = x.reshape(m, D_MODEL)
    for l in range(DEPTH):
        lam_init = 0.8 - 0.6 * math.exp(-0.3 * l)
        lam_rows = jnp.zeros((8, LANES), jnp.float32)
        lam_rows = lam_rows.at[0, :HEAD_DIM].set(lam_q1[l]).at[1, :HEAD_DIM].set(lam_k1[l])
        lam_rows = lam_rows.at[2, :HEAD_DIM].set(lam_q2[l]).at[3, :HEAD_DIM].set(lam_k2[l])
        lam_rows = lam_rows.at[4, :].set(lam_init)

        a2d = _proj_rope(x2d, w_a[l], w_ar[l], cos, sin, seq)
        b2d = _proj_plain(x2d, w_b[l])
        wf2d, aq2d, ak2d = _proj_small(x2d, w_c[l], bf_rows[l], seq)

        a3 = a2d.reshape(bsz, seq, A_WIDTH)
        b3 = b2d.reshape(bsz, seq, B_WIDTH)
        wf3 = wf2d.reshape(bsz, seq, LANES)

        diff_o = _diff_attention(a3, b3, lam_rows, g_subln[l].reshape(1, LANES))
        fox_o = _fox_attention(b3, aq2d.reshape(bsz, seq, LANES), ak2d.reshape(bsz, seq, LANES))
        dsa_o = _dsa_attention(a3, b3, wf3)

        x2d = _out_proj(x2d, b2d, diff_o.reshape(m, 512), dsa_o.reshape(m, 256), fox_o.reshape(m, 256),
                        w_o[l], ln_g[l].reshape(1, D_MODEL), ln_b[l].reshape(1, D_MODEL))
    return x2d.reshape(bsz, seq, D_MODEL)
```

```python
import functools
import math

import numpy as np
import jax
import jax.numpy as jnp
from jax import lax
from jax.experimental import pallas as pl
from jax.experimental.pallas import tpu as pltpu

D_MODEL = 1024
DEPTH = 4
HEAD_DIM = 64
DIFF_HEADS = 4
DSA_HEADS = 4
IDX_HEADS = 8
FOX_HEADS = 4
DSA_TOPK_MAX = 256
ROPE_THETA = 10000.0
LN_EPS = 1e-5
SUBLN_EPS = 1e-5
DEEPNORM_ALPHA = (2 * DEPTH) ** 0.25
IDX_WEIGHT_SCALE = (IDX_HEADS * HEAD_DIM) ** -0.5
LOG2E = math.log2(math.e)
Q_SCALE = HEAD_DIM ** -0.5 * LOG2E

LANES = 128
VMEM_LIMIT_BYTES = 56 * 1024 * 1024

_OFF = {}
_o = 0
for _name, _n in (('diff_q', 512), ('diff_k', 512), ('diff_v', 512), ('dsa_q', 256), ('dsa_k', 64),
                  ('dsa_v', 64), ('idx_q', 512), ('idx_k', 64), ('idx_w', 8), ('fox_q', 256),
                  ('fox_k', 256), ('fox_v', 256), ('fox_f', 4), ('gate', 1024)):
    _OFF[_name] = (_o, _n)
    _o += _n
IN_WIDTH = _o


def _cols(name, scale=1.0, repeat=1):
    o, n = _OFF[name]
    idx = np.tile(np.arange(o, o + n), repeat)
    return idx, np.full(idx.shape, scale, np.float32)


def _pad(n):
    return np.zeros((n,), np.int64), np.zeros((n,), np.float32)


def _layout(parts):
    idx = np.concatenate([p[0] for p in parts])
    scale = np.concatenate([p[1] for p in parts])
    return idx, scale


A_IDX, A_SCALE = _layout([_cols('diff_q', Q_SCALE), _cols('diff_k'), _cols('idx_q'),
                          _cols('dsa_q', Q_SCALE), _cols('idx_k', repeat=2), _cols('dsa_k', repeat=2)])
A_WIDTH = A_IDX.shape[0]
B_IDX, B_SCALE = _layout([_cols('gate'), _cols('diff_v'), _cols('fox_q', Q_SCALE), _cols('fox_k'),
                          _cols('fox_v'), _cols('dsa_v', repeat=2), _pad(128)])
B_WIDTH = B_IDX.shape[0]
C_IDX, C_SCALE = _layout([_cols('idx_w'), _cols('fox_f'), _pad(LANES - 12)])
W_LANE0 = 0
F_LANE0 = 8

_c = np.arange(A_WIDTH)
_first_half = (_c % HEAD_DIM) < HEAD_DIM // 2
A_ROT_PARTNER = np.where(_first_half, _c + HEAD_DIM // 2, _c - HEAD_DIM // 2)
A_ROT_SIGN = np.where(_first_half, -1.0, 1.0).astype(np.float32)

_NT = (((1,), (1,)), ((), ()))
INT_MIN = -2 ** 31
KEY_NEG_INF = -2139095041

_SORT4 = ((0, 1), (2, 3), (0, 2), (1, 3), (1, 2))
_BITONIC12 = ((0, 8), (1, 9), (2, 10), (3, 11), (4, 8), (5, 9), (6, 10), (7, 11),
              (0, 2), (1, 3), (4, 6), (5, 7), (8, 10), (9, 11),
              (0, 1), (2, 3), (4, 5), (6, 7), (8, 9), (10, 11))
DSA_SLOTS = 12
DSA_SCORE_CHUNK = 512


def _cparams(sem):
    return pltpu.CompilerParams(dimension_semantics=sem, vmem_limit_bytes=VMEM_LIMIT_BYTES)


PROJ_COLS = 512


def _proj_rope_kernel(x_ref, w_ref, wr_ref, cos_ref, sin_ref, o_ref):
    xb = x_ref[...].astype(jnp.bfloat16)
    cos = cos_ref[...]
    sin = sin_ref[...]
    for n in range(o_ref.shape[1] // PROJ_COLS):
        cols = slice(n * PROJ_COLS, (n + 1) * PROJ_COLS)
        h = jnp.dot(xb, w_ref[:, cols], preferred_element_type=jnp.float32)
        hr = jnp.dot(xb, wr_ref[:, cols], preferred_element_type=jnp.float32)
        for c in range(PROJ_COLS // LANES):
            sl = slice(c * LANES, (c + 1) * LANES)
            o_ref[:, n * PROJ_COLS + c * LANES:n * PROJ_COLS + (c + 1) * LANES] = (
                h[:, sl] * cos + hr[:, sl] * sin).astype(o_ref.dtype)


def _proj_rope(x2d, w, wr, cos, sin, seq, tm=512):
    m = x2d.shape[0]
    nseq = seq // tm
    return pl.pallas_call(
        _proj_rope_kernel,
        grid=(m // tm,),
        in_specs=[
            pl.BlockSpec((tm, D_MODEL), lambda i: (i, 0)),
            pl.BlockSpec((D_MODEL, A_WIDTH), lambda i: (0, 0)),
            pl.BlockSpec((D_MODEL, A_WIDTH), lambda i: (0, 0)),
            pl.BlockSpec((tm, LANES), lambda i: (i % nseq, 0)),
            pl.BlockSpec((tm, LANES), lambda i: (i % nseq, 0)),
        ],
        out_specs=pl.BlockSpec((tm, A_WIDTH), lambda i: (i, 0)),
        out_shape=jax.ShapeDtypeStruct((m, A_WIDTH), jnp.bfloat16),
        compiler_params=_cparams(("parallel",)),
        name="proj_rope",
    )(x2d, w, wr, cos, sin)


def _proj_plain_kernel(x_ref, w_ref, o_ref):
    xb = x_ref[...].astype(jnp.bfloat16)
    for n in range(o_ref.shape[1] // PROJ_COLS):
        cols = slice(n * PROJ_COLS, (n + 1) * PROJ_COLS)
        o_ref[:, cols] = jnp.dot(xb, w_ref[:, cols], preferred_element_type=jnp.float32).astype(o_ref.dtype)


def _proj_plain(x2d, w, tm=512):
    m = x2d.shape[0]
    return pl.pallas_call(
        _proj_plain_kernel,
        grid=(m // tm,),
        in_specs=[
            pl.BlockSpec((tm, D_MODEL), lambda i: (i, 0)),
            pl.BlockSpec((D_MODEL, B_WIDTH), lambda i: (0, 0)),
        ],
        out_specs=pl.BlockSpec((tm, B_WIDTH), lambda i: (i, 0)),
        out_shape=jax.ShapeDtypeStruct((m, B_WIDTH), jnp.bfloat16),
        compiler_params=_cparams(("parallel",)),
        name="proj_plain",
    )(x2d, w)


def _split3(v):
    hi = v.astype(jnp.bfloat16)
    r = v - hi.astype(jnp.float32)
    mid = r.astype(jnp.bfloat16)
    lo = (r - mid.astype(jnp.float32)).astype(jnp.bfloat16)
    return hi, mid, lo


def _gate_aug_constants():
    place = np.zeros((6, LANES, LANES), np.float32)
    ones = np.zeros((8, LANES), np.float32)
    for h in range(FOX_HEADS):
        for j in range(3):
            place[j, F_LANE0 + h, 8 * h + j] = 1.0
            place[3 + j, F_LANE0 + h, 8 * h + 3 + j] = -1.0
            ones[0, 8 * h + 3 + j] = 1.0
            ones[1, 8 * h + j] = 1.0
    return jnp.asarray(place, jnp.bfloat16), jnp.asarray(ones)


def _proj_small_kernel(x_ref, w_ref, bf_ref, tri_ref, place_ref, ones_ref, o_ref, aq_ref, ak_ref, carry_ref,
                       *, blocks_per_seq):
    i = pl.program_id(0)
    xb = x_ref[...].astype(jnp.bfloat16)
    h = jnp.dot(xb, w_ref[...], preferred_element_type=jnp.float32)
    lane = lax.broadcasted_iota(jnp.int32, h.shape, 1)
    ff = h + bf_ref[...]
    logf = jnp.minimum(ff, 0.0) - jnp.log(1.0 + jnp.exp(-jnp.abs(ff)))
    o = jnp.where(lane < F_LANE0, h * IDX_WEIGHT_SCALE, logf * LOG2E)
    o_ref[...] = o

    @pl.when(i % blocks_per_seq == 0)
    def _():
        carry_ref[...] = jnp.zeros_like(carry_ref)

    tri = tri_ref[...]
    hi, mid, lo = _split3(o)
    cs = (jnp.dot(tri, hi, preferred_element_type=jnp.float32)
          + jnp.dot(tri, mid, preferred_element_type=jnp.float32)
          + jnp.dot(tri, lo, preferred_element_type=jnp.float32))
    cs = cs + carry_ref[0:1, :]
    tm = cs.shape[0]
    carry_ref[...] = jnp.broadcast_to(cs[tm - 1:tm, :], carry_ref.shape)
    parts = _split3(cs)
    aq = ones_ref[0:1, :]
    ak = ones_ref[1:2, :]
    for j in range(3):
        aq = aq + jnp.dot(parts[j], place_ref[j], preferred_element_type=jnp.float32)
        ak = ak + jnp.dot(parts[j], place_ref[3 + j], preferred_element_type=jnp.float32)
    aq_ref[...] = aq.astype(aq_ref.dtype)
    ak_ref[...] = ak.astype(ak_ref.dtype)


def _proj_small(x2d, w, bf_row, seq, tm=512):
    m = x2d.shape[0]
    tri = jnp.asarray(np.tril(np.ones((tm, tm), np.float32)), jnp.bfloat16)
    place, ones = _gate_aug_constants()
    kern = functools.partial(_proj_small_kernel, blocks_per_seq=seq // tm)
    row = lambda i: (i, 0)
    return pl.pallas_call(
        kern,
        grid=(m // tm,),
        in_specs=[
            pl.BlockSpec((tm, D_MODEL), row),
            pl.BlockSpec((D_MODEL, LANES), lambda i: (0, 0)),
            pl.BlockSpec((1, LANES), lambda i: (0, 0)),
            pl.BlockSpec((tm, tm), lambda i: (0, 0)),
            pl.BlockSpec((6, LANES, LANES), lambda i: (0, 0, 0)),
            pl.BlockSpec((8, LANES), lambda i: (0, 0)),
        ],
        out_specs=[pl.BlockSpec((tm, LANES), row), pl.BlockSpec((tm, LANES), row), pl.BlockSpec((tm, LANES), row)],
        out_shape=[jax.ShapeDtypeStruct((m, LANES), jnp.float32),
                   jax.ShapeDtypeStruct((m, LANES), jnp.bfloat16),
                   jax.ShapeDtypeStruct((m, LANES), jnp.bfloat16)],
        scratch_shapes=[pltpu.VMEM((8, LANES), jnp.float32)],
        compiler_params=_cparams(("arbitrary",)),
        name="proj_small",
    )(x2d, w, bf_row, tri, place, ones)


def _flash_init(m_ref, acc_ref):
    m_ref[...] = jnp.full(m_ref.shape, -1e30, jnp.float32)
    acc_ref[...] = jnp.zeros(acc_ref.shape, jnp.float32)


def _flash_rows(s, v1, m_ref, acc_ref, rows):
    tk = s.shape[1]
    m_prev = m_ref[rows, :]
    m_next = jnp.maximum(m_prev, jnp.max(s, axis=1, keepdims=True))
    p = jnp.exp2((s - pltpu.repeat(m_next, tk // LANES, axis=1)).astype(jnp.bfloat16))
    alpha = jnp.exp2(m_prev - m_next)
    m_ref[rows, :] = m_next
    pv = jnp.dot(p, v1, preferred_element_type=jnp.float32)
    acc_ref[rows, :] = acc_ref[rows, :] * pltpu.repeat(alpha, v1.shape[1] // LANES, axis=1) + pv


def _pipelined_chunks(n_full, produce, consume, buf0, buf1, per_trip=4):
    bufs = (buf0, buf1)
    produce(0, buf0)

    def body(p, carry):
        for t in range(per_trip):
            produce(per_trip * p + t + 1, bufs[(t + 1) % 2])
            consume(per_trip * p + t, bufs[t % 2], False)
        return carry

    lax.fori_loop(0, n_full // per_trip, body, 0)
    base = (n_full // per_trip) * per_trip

    for rem in range(per_trip):
        @pl.when(n_full - base == rem)
        def _(rem=rem):
            for t in range(rem):
                produce(base + t + 1, bufs[(t + 1) % 2])
                consume(base + t, bufs[t % 2], False)
            consume(base + rem, bufs[rem % 2], True)


def _stack_pair(q, qs_ref, t):
    lane = lax.broadcasted_iota(jnp.int32, q.shape, 1)
    zero = jnp.zeros_like(q)
    qs_ref[0:t, :] = jnp.where(lane < HEAD_DIM, q, zero)
    qs_ref[t:2 * t, :] = jnp.where(lane >= HEAD_DIM, q, zero)


def _with_ones(v):
    return jnp.concatenate([v, jnp.ones((v.shape[0], LANES), v.dtype)], axis=1)


def _diff_kernel(lam_ref, g_ref, q_ref, k_ref, v_ref, o_ref, qs_ref, m_ref, acc_ref, s0_ref, s1_ref,
                 *, tq, tk):
    i = pl.program_id(2)
    _stack_pair(q_ref[0], qs_ref, tq)
    _flash_init(m_ref, acc_ref)
    groups = (slice(0, tq), slice(tq, 2 * tq))

    def produce(j, s_ref):
        start = pl.multiple_of(j * tk, tk)
        k = k_ref[0, pl.ds(start, tk), :]
        for rows in groups:
            s_ref[rows, :] = lax.dot_general(qs_ref[rows, :], k, _NT, preferred_element_type=jnp.float32)

    def consume(j, s_ref, last):
        start = pl.multiple_of(j * tk, tk)
        v1 = _with_ones(v_ref[0, pl.ds(start, tk), :])
        for rows in groups:
            s = s_ref[rows, :]
            if last:
                r = lax.broadcasted_iota(jnp.int32, (tq, tk), 0)
                c = lax.broadcasted_iota(jnp.int32, (tq, tk), 1)
                s = jnp.where(start + c <= i * tq + r, s, -jnp.inf)
            _flash_rows(s, v1, m_ref, acc_ref, rows)

    _pipelined_chunks((i * tq) // tk, produce, consume, s0_ref, s1_ref)

    lam_rows = lam_ref[...]
    s1 = jnp.sum(lam_rows[0:1, :] * lam_rows[1:2, :], axis=1, keepdims=True)
    s2 = jnp.sum(lam_rows[2:3, :] * lam_rows[3:4, :], axis=1, keepdims=True)
    lam_init = lam_rows[4:5, 0:1]
    lam = jnp.exp(s1) - jnp.exp(s2) + lam_init
    o0 = acc_ref[0:tq, 0:LANES] / acc_ref[0:tq, LANES:2 * LANES]
    o1 = acc_ref[tq:2 * tq, 0:LANES] / acc_ref[tq:2 * tq, LANES:2 * LANES]
    a = o0 - lam * o1
    ms = jnp.mean(a * a, axis=1, keepdims=True)
    a = a * lax.rsqrt(ms + SUBLN_EPS) * g_ref[...] * (1.0 - lam_init)
    o_ref[0] = a.astype(o_ref.dtype)


def _diff_attention(a3, b3, lam_rows, g_row, tq=1024, tk=1024):
    bsz, seq, _ = a3.shape
    tk = min(tk, seq)
    tq = min(tq, seq)
    kern = functools.partial(_diff_kernel, tq=tq, tk=tk)
    return pl.pallas_call(
        kern,
        grid=(bsz, DIFF_HEADS, seq // tq),
        in_specs=[
            pl.BlockSpec((8, LANES), lambda b, h, i: (0, 0)),
            pl.BlockSpec((1, LANES), lambda b, h, i: (0, 0)),
            pl.BlockSpec((1, tq, LANES), lambda b, h, i: (b, i, h)),
            pl.BlockSpec((1, seq, LANES), lambda b, h, i: (b, 0, 4 + h)),
            pl.BlockSpec((1, seq, LANES), lambda b, h, i: (b, 0, 8 + h)),
        ],
        out_specs=pl.BlockSpec((1, tq, LANES), lambda b, h, i: (b, i, h)),
        out_shape=jax.ShapeDtypeStruct((bsz, seq, DIFF_HEADS * LANES), jnp.bfloat16),
        scratch_shapes=[pltpu.VMEM((2 * tq, LANES), jnp.bfloat16),
                        pltpu.VMEM((2 * tq, LANES), jnp.float32),
                        pltpu.VMEM((2 * tq, 2 * LANES), jnp.float32),
                        pltpu.VMEM((2 * tq, tk), jnp.float32),
                        pltpu.VMEM((2 * tq, tk), jnp.float32)],
        compiler_params=_cparams(("parallel", "parallel", "arbitrary")),
        name="diff_attention",
    )(lam_rows, g_row, a3, a3, b3)


def _fox_kernel(q_ref, k_ref, v_ref, aq_ref, ak_ref, o_ref, qs_ref, m_ref, acc_ref, s0_ref, s1_ref, *, tq, tk):
    g = pl.program_id(1)
    i = pl.program_id(2)
    q = q_ref[0]
    aq = aq_ref[0]
    lane = lax.broadcasted_iota(jnp.int32, (tq, LANES), 1)
    zero = jnp.zeros_like(q)
    for a in range(2):
        rows = slice(a * tq, (a + 1) * tq)
        head_lanes = (lane < HEAD_DIM) if a == 0 else (lane >= HEAD_DIM)
        first = 8 * (2 * g + a)
        qs_ref[rows, 0:LANES] = jnp.where(head_lanes, q, zero)
        qs_ref[rows, LANES:2 * LANES] = jnp.where((lane >= first) & (lane < first + 6), aq, zero)
    _flash_init(m_ref, acc_ref)
    groups = (slice(0, tq), slice(tq, 2 * tq))

    def produce(j, s_ref):
        start = pl.multiple_of(j * tk, tk)
        k2 = jnp.concatenate([k_ref[0, pl.ds(start, tk), :], ak_ref[0, pl.ds(start, tk), :]], axis=1)
        for rows in groups:
            s_ref[rows, :] = lax.dot_general(qs_ref[rows, :], k2, _NT, preferred_element_type=jnp.float32)

    def consume(j, s_ref, last):
        start = pl.multiple_of(j * tk, tk)
        v1 = _with_ones(v_ref[0, pl.ds(start, tk), :])
        for rows in groups:
            s = s_ref[rows, :]
            if last:
                r = lax.broadcasted_iota(jnp.int32, (tq, tk), 0)
                c = lax.broadcasted_iota(jnp.int32, (tq, tk), 1)
                s = jnp.where(start + c <= i * tq + r, s, -jnp.inf)
            _flash_rows(s, v1, m_ref, acc_ref, rows)

    _pipelined_chunks((i * tq) // tk, produce, consume, s0_ref, s1_ref)

    o0 = acc_ref[0:tq, 0:LANES] / acc_ref[0:tq, LANES:2 * LANES]
    o1 = acc_ref[tq:2 * tq, 0:LANES] / acc_ref[tq:2 * tq, LANES:2 * LANES]
    o_ref[0] = jnp.where(lane < HEAD_DIM, o0, o1).astype(o_ref.dtype)


def _fox_attention(b3, aq3, ak3, tq=512, tk=1024):
    bsz, seq, _ = b3.shape
    tk = min(tk, seq)
    kern = functools.partial(_fox_kernel, tq=tq, tk=tk)
    return pl.pallas_call(
        kern,
        grid=(bsz, FOX_HEADS // 2, seq // tq),
        in_specs=[
            pl.BlockSpec((1, tq, LANES), lambda b, g, i: (b, i, 12 + g)),
            pl.BlockSpec((1, seq, LANES), lambda b, g, i: (b, 0, 14 + g)),
            pl.BlockSpec((1, seq, LANES), lambda b, g, i: (b, 0, 16 + g)),
            pl.BlockSpec((1, tq, LANES), lambda b, g, i: (b, i, 0)),
            pl.BlockSpec((1, seq, LANES), lambda b, g, i: (b, 0, 0)),
        ],
        out_specs=pl.BlockSpec((1, tq, LANES), lambda b, g, i: (b, i, g)),
        out_shape=jax.ShapeDtypeStruct((bsz, seq, FOX_HEADS * HEAD_DIM), jnp.bfloat16),
        scratch_shapes=[pltpu.VMEM((2 * tq, 2 * LANES), jnp.bfloat16),
                        pltpu.VMEM((2 * tq, LANES), jnp.float32),
                        pltpu.VMEM((2 * tq, 2 * LANES), jnp.float32),
                        pltpu.VMEM((2 * tq, tk), jnp.float32),
                        pltpu.VMEM((2 * tq, tk), jnp.float32)],
        compiler_params=_cparams(("parallel", "parallel", "arbitrary")),
        name="fox_attention",
    )(b3, b3, b3, aq3, ak3)


def _score_key(score):
    bits = pltpu.bitcast(score, jnp.int32)
    return bits ^ ((bits >> 31) & 0x7FFFFFFF)


def _dsa_kernel(iq_ref, sq_ref, w_ref, ik_ref, sk_ref, v_ref, o_ref,
                keys_ref, cand_ref, candk_ref, qi_ref, qd_ref, wb_ref, m_ref, acc_ref, s0_ref, s1_ref,
                thr_ref, cnt_ref, flag_ref, *, tq, tc, topk, idx_bits, slots):
    i = pl.program_id(1)
    lane = lax.broadcasted_iota(jnp.int32, (tq, LANES), 1)
    low = lane < HEAD_DIM

    for h in range(IDX_HEADS):
        blk = iq_ref[0, :, (h // 2) * LANES:(h // 2 + 1) * LANES]
        qi_ref[h * tq:(h + 1) * tq, :] = jnp.where(low if h % 2 == 0 else ~low, blk, jnp.zeros_like(blk))
    dsa_order = (0, 2, 1, 3)
    for slab, h in enumerate(dsa_order):
        blk = sq_ref[0, :, (h // 2) * LANES:(h // 2 + 1) * LANES]
        qd_ref[slab * tq:(slab + 1) * tq, :] = jnp.where(low if h % 2 == 0 else ~low, blk, jnp.zeros_like(blk))
    wt = w_ref[0]
    for h in range(IDX_HEADS):
        col = jnp.sum(jnp.where(lane == W_LANE0 + h, wt, 0.0), axis=1, keepdims=True)
        wb_ref[h] = jnp.broadcast_to(col, (tq, LANES))

    nlast = (i * tq) // tc
    nch = nlast + 1
    cand_ref[...] = jnp.full(cand_ref.shape, -jnp.inf, jnp.float32)
    row_pos = i * tq + lax.broadcasted_iota(jnp.int32, (tq, tc), 0)
    col_iota = lax.broadcasted_iota(jnp.int32, (tq, tc), 1)
    idx_groups = (slice(0, 4 * tq), slice(4 * tq, 8 * tq))
    dsa_groups = (slice(0, 2 * tq), slice(2 * tq, 4 * tq))

    def rep(x):
        return pltpu.repeat(x, tc // LANES, axis=1)

    def idx_produce(j, s_ref):
        start = pl.multiple_of(j * tc, tc)
        ik = ik_ref[0, pl.ds(start, tc), :]
        for rows in idx_groups:
            s_ref[rows, :] = lax.dot_general(qi_ref[rows, :], ik, _NT, preferred_element_type=jnp.float32)

    def idx_consume(j, s_ref, last):
        start = pl.multiple_of(j * tc, tc)
        score = jnp.zeros((tq, tc), jnp.float32)
        for h in range(IDX_HEADS):
            score = score + jnp.maximum(s_ref[h * tq:(h + 1) * tq, :], 0.0) * rep(wb_ref[h])
        if last:
            score = jnp.where(start + col_iota <= row_pos, score, -jnp.inf)
        keys_ref[:, pl.ds(start, tc)] = _score_key(score)
        for rg in range(tq // 8):
            rs = slice(rg * 8, (rg + 1) * 8)
            xs = [score[rs, u * LANES:(u + 1) * LANES] for u in range(4)]
            for p, q in _SORT4:
                xs[p], xs[q] = jnp.maximum(xs[p], xs[q]), jnp.minimum(xs[p], xs[q])
            c = [cand_ref[rs, sl * LANES:(sl + 1) * LANES] for sl in range(12)]
            for t in range(4):
                c[8 + t] = jnp.maximum(c[8 + t], xs[3 - t])
            for p, q in _BITONIC12:
                c[p], c[q] = jnp.maximum(c[p], c[q]), jnp.minimum(c[p], c[q])
            for sl in range(12):
                cand_ref[rs, sl * LANES:(sl + 1) * LANES] = c[sl]

    _pipelined_chunks(nlast, idx_produce, idx_consume, s0_ref, s1_ref, per_trip=2)

    def count(pred):
        def body(c, acc):
            start = pl.multiple_of(c * tc, tc)
            hit = pred(keys_ref[:, pl.ds(start, tc)], start).astype(jnp.int32)
            for u in range(tc // LANES):
                acc = acc + hit[:, u * LANES:(u + 1) * LANES]
            return acc
        part = lax.fori_loop(0, nch, body, jnp.zeros((tq, LANES), jnp.int32))
        return jnp.broadcast_to(jnp.sum(part, axis=1, keepdims=True), (tq, LANES))

    n_row = i * tq + lax.broadcasted_iota(jnp.int32, (tq, LANES), 0) + 1

    def search_all_keys():
        def pending(state):
            bi, thr, cnt_thr = state
            return jnp.logical_and(bi < 32, jnp.max(jnp.where(n_row > topk, cnt_thr, topk)) > topk)

        def bit_step(state):
            bi, thr, cnt_thr = state
            cand = thr + lax.shift_left(jnp.int32(1), 31 - bi)
            cnt = count(lambda kc, start: kc >= rep(cand))
            ok = cnt >= topk
            return bi + 1, jnp.where(ok, cand, thr), jnp.where(ok, cnt, cnt_thr)

        total = jnp.zeros((tq, LANES), jnp.int32) + nch * tc
        init = (jnp.int32(0), jnp.full((tq, LANES), INT_MIN, jnp.int32), total)
        _, thr, cnt_thr = lax.while_loop(pending, bit_step, init)
        thr_ref[...] = thr
        cnt_ref[...] = cnt_thr

    def search_candidates():
        def pair_step(p, state):
            thr, cnt_thr = state
            one_lo = lax.shift_left(jnp.int32(1), 30 - 2 * p)
            cands = [thr + one_lo, thr + 2 * one_lo, thr + 3 * one_lo]
            parts = [jnp.zeros((tq, LANES), jnp.float32) for _ in cands]
            for sl in range(slots):
                piece = candk_ref[:, sl * LANES:(sl + 1) * LANES]
                parts = [jnp.where(piece >= c, float(sl + 1), part) for c, part in zip(cands, parts)]
            for c, part in zip(cands, parts):
                cnt = jnp.broadcast_to(jnp.sum(part, axis=1, keepdims=True), (tq, LANES))
                ok = cnt >= float(topk)
                thr = jnp.where(ok, c, thr)
                cnt_thr = jnp.where(ok, cnt, cnt_thr)
            return thr, cnt_thr

        init = (jnp.full((tq, LANES), INT_MIN, jnp.int32), jnp.full((tq, LANES), float(slots * LANES), jnp.float32))
        thr, cnt_thr = lax.fori_loop(0, 16, pair_step, init, unroll=4)
        thr_ref[...] = thr
        cnt_ref[...] = cnt_thr.astype(jnp.int32)

    cand_chunks = slots * LANES // tc
    flag_ref[0] = 1

    @pl.when(nch > cand_chunks)
    def _():
        candk_ref[...] = _score_key(cand_ref[...])
        search_candidates()
        kept_min = jnp.max(candk_ref[:, (slots - 1) * LANES:slots * LANES], axis=1, keepdims=True)
        unsafe = jnp.where(kept_min >= thr_ref[...], 1, 0)
        flag_ref[0] = jnp.max(unsafe)

    @pl.when(flag_ref[0] > 0)
    def _():
        search_all_keys()

    thr = thr_ref[...]
    cnt_thr = cnt_ref[...]

    excess = jnp.where(thr > KEY_NEG_INF, cnt_thr - topk, 0)

    @pl.when(jnp.max(excess) > 0)
    def _():
        c_gt = count(lambda kc, start: kc > rep(thr))
        keep = topk - c_gt

        def idx_step(bi, x):
            cand = x + lax.shift_left(jnp.int32(1), idx_bits - 1 - bi)
            below = count(lambda kc, start: (kc == rep(thr)) & (start + col_iota < rep(cand)))
            return jnp.where(below < keep, cand, x)

        last = lax.fori_loop(0, idx_bits, idx_step, jnp.zeros((tq, LANES), jnp.int32))

        def demote(c, carry):
            start = pl.multiple_of(c * tc, tc)
            kc = keys_ref[:, pl.ds(start, tc)]
            drop = (rep(excess) > 0) & (kc == rep(thr)) & (start + col_iota > rep(last))
            keys_ref[:, pl.ds(start, tc)] = jnp.where(drop, kc - 1, kc)
            return carry

        lax.fori_loop(0, nch, demote, 0)

    sel_thr = jnp.maximum(thr, KEY_NEG_INF + 1)

    _flash_init(m_ref, acc_ref)
    vlow = lax.broadcasted_iota(jnp.int32, (tc, LANES), 1) < HEAD_DIM

    def att_produce(j, s_ref):
        start = pl.multiple_of(j * tc, tc)
        sk = sk_ref[0, pl.ds(start, tc), :]
        for rows in dsa_groups:
            s_ref[rows, :] = lax.dot_general(qd_ref[rows, :], sk, _NT, preferred_element_type=jnp.float32)

    def att_consume(j, s_ref, last):
        start = pl.multiple_of(j * tc, tc)
        v2 = v_ref[0, pl.ds(start, tc), :]
        one = jnp.ones_like(v2)
        v_even = jnp.where(vlow, v2, one)
        v_odd = jnp.where(vlow, one, v2)
        sel = keys_ref[:, pl.ds(start, tc)] >= rep(sel_thr)
        for rows, v1 in zip(dsa_groups, (v_even, v_odd)):
            s = s_ref[rows, :].reshape(2, tq, tc)
            s = jnp.where(sel[None], s, -jnp.inf).reshape(2 * tq, tc)
            _flash_rows(s, v1, m_ref, acc_ref, rows)

    _pipelined_chunks(nlast, att_produce, att_consume, s0_ref, s1_ref)

    outs = {}
    for slab, h in enumerate(dsa_order):
        acc = acc_ref[slab * tq:(slab + 1) * tq, :]
        outs[h] = acc / pltpu.roll(acc, HEAD_DIM, axis=1)
    for p in range(DSA_HEADS // 2):
        o_ref[0, :, p * LANES:(p + 1) * LANES] = jnp.where(low, outs[2 * p], outs[2 * p + 1]).astype(o_ref.dtype)


def _dsa_attention(a3, b3, w3, tq=128):
    bsz, seq, _ = a3.shape
    tc, slots = DSA_SCORE_CHUNK, DSA_SLOTS
    topk = min(DSA_TOPK_MAX, seq // 4)
    idx_bits = int(math.log2(seq))
    assert 2 ** idx_bits == seq and tc % tq == 0 and (slots * LANES) % tc == 0 and seq % tc == 0
    kern = functools.partial(_dsa_kernel, tq=tq, tc=tc, topk=topk, idx_bits=idx_bits, slots=slots)
    return pl.pallas_call(
        kern,
        grid=(bsz, seq // tq),
        in_specs=[
            pl.BlockSpec((1, tq, 4 * LANES), lambda b, i: (b, i, 2)),
            pl.BlockSpec((1, tq, 2 * LANES), lambda b, i: (b, i, 6)),
            pl.BlockSpec((1, tq, LANES), lambda b, i: (b, i, 0)),
            pl.BlockSpec((1, seq, LANES), lambda b, i: (b, 0, 14)),
            pl.BlockSpec((1, seq, LANES), lambda b, i: (b, 0, 15)),
            pl.BlockSpec((1, seq, LANES), lambda b, i: (b, 0, 18)),
        ],
        out_specs=pl.BlockSpec((1, tq, 2 * LANES), lambda b, i: (b, i, 0)),
        out_shape=jax.ShapeDtypeStruct((bsz, seq, DSA_HEADS * HEAD_DIM), jnp.bfloat16),
        scratch_shapes=[pltpu.VMEM((tq, seq), jnp.int32),
                        pltpu.VMEM((tq, slots * LANES), jnp.float32),
                        pltpu.VMEM((tq, slots * LANES), jnp.int32),
                        pltpu.VMEM((IDX_HEADS * tq, LANES), jnp.bfloat16),
                        pltpu.VMEM((DSA_HEADS * tq, LANES), jnp.bfloat16),
                        pltpu.VMEM((IDX_HEADS, tq, LANES), jnp.float32),
                        pltpu.VMEM((DSA_HEADS * tq, LANES), jnp.float32),
                        pltpu.VMEM((DSA_HEADS * tq, LANES), jnp.float32),
                        pltpu.VMEM((IDX_HEADS * tq, tc), jnp.float32),
                        pltpu.VMEM((IDX_HEADS * tq, tc), jnp.float32),
                        pltpu.VMEM((tq, LANES), jnp.int32),
                        pltpu.VMEM((tq, LANES), jnp.int32),
                        pltpu.SMEM((1,), jnp.int32)],
        compiler_params=_cparams(("parallel", "arbitrary")),
        name="dsa_attention",
    )(a3, a3, w3, a3, a3, b3)


def _out_kernel(x_ref, gate_ref, a_ref, bo_ref, co_ref, wa_ref, wb_ref, wc_ref, g_ref, b_ref, o_ref):
    gate = gate_ref[...].astype(jnp.float32)
    silu = gate / (1.0 + jnp.exp(-gate))

    def gated(ref, lo, hi):
        return (ref[...].astype(jnp.float32) * silu[:, lo:hi]).astype(jnp.bfloat16)

    out = jnp.dot(gated(a_ref, 0, 512), wa_ref[...], preferred_element_type=jnp.float32)
    out = out + jnp.dot(gated(bo_ref, 512, 768), wb_ref[...], preferred_element_type=jnp.float32)
    out = out + jnp.dot(gated(co_ref, 768, 1024), wc_ref[...], preferred_element_type=jnp.float32)
    y = DEEPNORM_ALPHA * x_ref[...] + out
    mu = jnp.mean(y, axis=1, keepdims=True)
    d = y - mu
    var = jnp.mean(d * d, axis=1, keepdims=True)
    o_ref[...] = d * lax.rsqrt(var + LN_EPS) * g_ref[...] + b_ref[...]


def _out_proj(x2d, b2d, a2d, bo2d, co2d, w_out, ln_g, ln_b, tm=512):
    m = x2d.shape[0]
    row = lambda i: (i, 0)
    fixed = lambda i: (0, 0)
    return pl.pallas_call(
        _out_kernel,
        grid=(m // tm,),
        in_specs=[
            pl.BlockSpec((tm, D_MODEL), row),
            pl.BlockSpec((tm, D_MODEL), row),
            pl.BlockSpec((tm, 512), row),
            pl.BlockSpec((tm, 256), row),
            pl.BlockSpec((tm, 256), row),
            pl.BlockSpec((512, D_MODEL), fixed),
            pl.BlockSpec((256, D_MODEL), lambda i: (2, 0)),
            pl.BlockSpec((256, D_MODEL), lambda i: (3, 0)),
            pl.BlockSpec((1, D_MODEL), fixed),
            pl.BlockSpec((1, D_MODEL), fixed),
        ],
        out_specs=pl.BlockSpec((tm, D_MODEL), row),
        out_shape=jax.ShapeDtypeStruct((m, D_MODEL), jnp.float32),
        compiler_params=_cparams(("parallel",)),
        name="out_proj",
    )(x2d, b2d, a2d, bo2d, co2d, w_out, w_out, w_out, ln_g, ln_b)


def _rope_tables(seq):
    inv = ROPE_THETA ** (-jnp.arange(0, HEAD_DIM, 2, dtype=jnp.float32) / HEAD_DIM)
    ang = jnp.arange(seq, dtype=jnp.int32).astype(jnp.float32)[:, None] * inv[None, :]
    reps = LANES // (HEAD_DIM // 2)
    return jnp.tile(jnp.cos(ang), (1, reps)), jnp.tile(jnp.sin(ang), (1, reps))


def kernel(x, w_in, b_f, lam_q1, lam_k1, lam_q2, lam_k2, g_subln, w_out, ln_g, ln_b):
    bsz, seq, _ = x.shape
    m = bsz * seq
    cos, sin = _rope_tables(seq)

    w_a = w_in[:, :, A_IDX] * A_SCALE
    w_ar = (w_a[:, :, A_ROT_PARTNER] * A_ROT_SIGN).astype(jnp.bfloat16)
    w_a = w_a.astype(jnp.bfloat16)
    w_b = (w_in[:, :, B_IDX] * B_SCALE).astype(jnp.bfloat16)
    w_c = (w_in[:, :, C_IDX] * C_SCALE).astype(jnp.bfloat16)
    w_o = w_out.astype(jnp.bfloat16)
    bf_rows = jnp.zeros((DEPTH, 1, LANES), jnp.float32).at[:, 0, F_LANE0:F_LANE0 + FOX_HEADS].set(b_f)

    x2d = x.reshape(m, D_MODEL)
    for l in range(DEPTH):
        lam_init = 0.8 - 0.6 * math.exp(-0.3 * l)
        lam_rows = jnp.zeros((8, LANES), jnp.float32)
        lam_rows = lam_rows.at[0, :HEAD_DIM].set(lam_q1[l]).at[1, :HEAD_DIM].set(lam_k1[l])
        lam_rows = lam_rows.at[2, :HEAD_DIM].set(lam_q2[l]).at[3, :HEAD_DIM].set(lam_k2[l])
        lam_rows = lam_rows.at[4, :].set(lam_init)

        a2d = _proj_rope(x2d, w_a[l], w_ar[l], cos, sin, seq)
        b2d = _proj_plain(x2d, w_b[l])
        wf2d, aq2d, ak2d = _proj_small(x2d, w_c[l], bf_rows[l], seq)

        a3 = a2d.reshape(bsz, seq, A_WIDTH)
        b3 = b2d.reshape(bsz, seq, B_WIDTH)
        wf3 = wf2d.reshape(bsz, seq, LANES)

        diff_o = _diff_attention(a3, b3, lam_rows, g_subln[l].reshape(1, LANES))
        fox_o = _fox_attention(b3, aq2d.reshape(bsz, seq, LANES), ak2d.reshape(bsz, seq, LANES))
        dsa_o = _dsa_attention(a3, b3, wf3)

        x2d = _out_proj(x2d, b2d, diff_o.reshape(m, 512), dsa_o.reshape(m, 256), fox_o.reshape(m, 256),
                        w_o[l], ln_g[l].reshape(1, D_MODEL), ln_b[l].reshape(1, D_MODEL))
    return x2d.reshape(bsz, seq, D_MODEL)
```

```python
import functools
import math

import numpy as np
import jax
import jax.numpy as jnp
from jax import lax
from jax.experimental import pallas as pl
from jax.experimental.pallas import tpu as pltpu

D_MODEL = 1024
DEPTH = 4
HEAD_DIM = 64
DIFF_HEADS = 4
DSA_HEADS = 4
IDX_HEADS = 8
FOX_HEADS = 4
DSA_TOPK_MAX = 256
ROPE_THETA = 10000.0
LN_EPS = 1e-5
SUBLN_EPS = 1e-5
DEEPNORM_ALPHA = (2 * DEPTH) ** 0.25
IDX_WEIGHT_SCALE = (IDX_HEADS * HEAD_DIM) ** -0.5
LOG2E = math.log2(math.e)
Q_SCALE = HEAD_DIM ** -0.5 * LOG2E

LANES = 128
VMEM_LIMIT_BYTES = 56 * 1024 * 1024

_OFF = {}
_o = 0
for _name, _n in (('diff_q', 512), ('diff_k', 512), ('diff_v', 512), ('dsa_q', 256), ('dsa_k', 64),
                  ('dsa_v', 64), ('idx_q', 512), ('idx_k', 64), ('idx_w', 8), ('fox_q', 256),
                  ('fox_k', 256), ('fox_v', 256), ('fox_f', 4), ('gate', 1024)):
    _OFF[_name] = (_o, _n)
    _o += _n
IN_WIDTH = _o


def _cols(name, scale=1.0, repeat=1):
    o, n = _OFF[name]
    idx = np.tile(np.arange(o, o + n), repeat)
    return idx, np.full(idx.shape, scale, np.float32)


def _pad(n):
    return np.zeros((n,), np.int64), np.zeros((n,), np.float32)


def _layout(parts):
    idx = np.concatenate([p[0] for p in parts])
    scale = np.concatenate([p[1] for p in parts])
    return idx, scale


A_IDX, A_SCALE = _layout([_cols('diff_q', Q_SCALE), _cols('diff_k'), _cols('idx_q'),
                          _cols('dsa_q', Q_SCALE), _cols('idx_k', repeat=2), _cols('dsa_k', repeat=2)])
A_WIDTH = A_IDX.shape[0]
B_IDX, B_SCALE = _layout([_cols('gate'), _cols('diff_v'), _cols('fox_q', Q_SCALE), _cols('fox_k'),
                          _cols('fox_v'), _cols('dsa_v', repeat=2), _pad(128)])
B_WIDTH = B_IDX.shape[0]
C_IDX, C_SCALE = _layout([_cols('idx_w'), _cols('fox_f'), _pad(LANES - 12)])
W_LANE0 = 0
F_LANE0 = 8

_c = np.arange(A_WIDTH)
_first_half = (_c % HEAD_DIM) < HEAD_DIM // 2
A_ROT_PARTNER = np.where(_first_half, _c + HEAD_DIM // 2, _c - HEAD_DIM // 2)
A_ROT_SIGN = np.where(_first_half, -1.0, 1.0).astype(np.float32)

_NT = (((1,), (1,)), ((), ()))
INT_MIN = -2 ** 31
KEY_NEG_INF = -2139095041

_SORT4 = ((0, 1), (2, 3), (0, 2), (1, 3), (1, 2))
_BITONIC12 = ((0, 8), (1, 9), (2, 10), (3, 11), (4, 8), (5, 9), (6, 10), (7, 11),
              (0, 2), (1, 3), (4, 6), (5, 7), (8, 10), (9, 11),
              (0, 1), (2, 3), (4, 5), (6, 7), (8, 9), (10, 11))
DSA_SLOTS = 12
DSA_SCORE_CHUNK = 512


def _cparams(sem):
    return pltpu.CompilerParams(dimension_semantics=sem, vmem_limit_bytes=VMEM_LIMIT_BYTES)


PROJ_COLS = 512


def _proj_rope_kernel(x_ref, w_ref, wr_ref, cos_ref, sin_ref, o_ref):
    xb = x_ref[...].astype(jnp.bfloat16)
    cos = cos_ref[...]
    sin = sin_ref[...]
    for n in range(o_ref.shape[1] // PROJ_COLS):
        cols = slice(n * PROJ_COLS, (n + 1) * PROJ_COLS)
        h = jnp.dot(xb, w_ref[:, cols], preferred_element_type=jnp.float32)
        hr = jnp.dot(xb, wr_ref[:, cols], preferred_element_type=jnp.float32)
        for c in range(PROJ_COLS // LANES):
            sl = slice(c * LANES, (c + 1) * LANES)
            o_ref[:, n * PROJ_COLS + c * LANES:n * PROJ_COLS + (c + 1) * LANES] = (
                h[:, sl] * cos + hr[:, sl] * sin).astype(o_ref.dtype)


def _proj_rope(x2d, w, wr, cos, sin, seq, tm=512):
    m = x2d.shape[0]
    nseq = seq // tm
    return pl.pallas_call(
        _proj_rope_kernel,
        grid=(m // tm,),
        in_specs=[
            pl.BlockSpec((tm, D_MODEL), lambda i: (i, 0)),
            pl.BlockSpec((D_MODEL, A_WIDTH), lambda i: (0, 0)),
            pl.BlockSpec((D_MODEL, A_WIDTH), lambda i: (0, 0)),
            pl.BlockSpec((tm, LANES), lambda i: (i % nseq, 0)),
            pl.BlockSpec((tm, LANES), lambda i: (i % nseq, 0)),
        ],
        out_specs=pl.BlockSpec((tm, A_WIDTH), lambda i: (i, 0)),
        out_shape=jax.ShapeDtypeStruct((m, A_WIDTH), jnp.bfloat16),
        compiler_params=_cparams(("parallel",)),
        name="proj_rope",
    )(x2d, w, wr, cos, sin)


def _proj_plain_kernel(x_ref, w_ref, o_ref):
    xb = x_ref[...].astype(jnp.bfloat16)
    for n in range(o_ref.shape[1] // PROJ_COLS):
        cols = slice(n * PROJ_COLS, (n + 1) * PROJ_COLS)
        o_ref[:, cols] = jnp.dot(xb, w_ref[:, cols], preferred_element_type=jnp.float32).astype(o_ref.dtype)


def _proj_plain(x2d, w, tm=512):
    m = x2d.shape[0]
    return pl.pallas_call(
        _proj_plain_kernel,
        grid=(m // tm,),
        in_specs=[
            pl.BlockSpec((tm, D_MODEL), lambda i: (i, 0)),
            pl.BlockSpec((D_MODEL, B_WIDTH), lambda i: (0, 0)),
        ],
        out_specs=pl.BlockSpec((tm, B_WIDTH), lambda i: (i, 0)),
        out_shape=jax.ShapeDtypeStruct((m, B_WIDTH), jnp.bfloat16),
        compiler_params=_cparams(("parallel",)),
        name="proj_plain",
    )(x2d, w)


def _split3(v):
    hi = v.astype(jnp.bfloat16)
    r = v - hi.astype(jnp.float32)
    mid = r.astype(jnp.bfloat16)
    lo = (r - mid.astype(jnp.float32)).astype(jnp.bfloat16)
    return hi, mid, lo


def _gate_aug_constants():
    place = np.zeros((6, LANES, LANES), np.float32)
    ones = np.zeros((8, LANES), np.float32)
    for h in range(FOX_HEADS):
        for j in range(3):
            place[j, F_LANE0 + h, 8 * h + j] = 1.0
            place[3 + j, F_LANE0 + h, 8 * h + 3 + j] = -1.0
            ones[0, 8 * h + 3 + j] = 1.0
            ones[1, 8 * h + j] = 1.0
    return jnp.asarray(place, jnp.bfloat16), jnp.asarray(ones)


def _proj_small_kernel(x_ref, w_ref, bf_ref, tri_ref, place_ref, ones_ref, o_ref, aq_ref, ak_ref, carry_ref,
                       *, blocks_per_seq):
    i = pl.program_id(0)
    xb = x_ref[...].astype(jnp.bfloat16)
    h = jnp.dot(xb, w_ref[...], preferred_element_type=jnp.float32)
    lane = lax.broadcasted_iota(jnp.int32, h.shape, 1)
    ff = h + bf_ref[...]
    logf = jnp.minimum(ff, 0.0) - jnp.log(1.0 + jnp.exp(-jnp.abs(ff)))
    o = jnp.where(lane < F_LANE0, h * IDX_WEIGHT_SCALE, logf * LOG2E)
    o_ref[...] = o

    @pl.when(i % blocks_per_seq == 0)
    def _():
        carry_ref[...] = jnp.zeros_like(carry_ref)

    tri = tri_ref[...]
    hi, mid, lo = _split3(o)
    cs = (jnp.dot(tri, hi, preferred_element_type=jnp.float32)
          + jnp.dot(tri, mid, preferred_element_type=jnp.float32)
          + jnp.dot(tri, lo, preferred_element_type=jnp.float32))
    cs = cs + carry_ref[0:1, :]
    tm = cs.shape[0]
    carry_ref[...] = jnp.broadcast_to(cs[tm - 1:tm, :], carry_ref.shape)
    parts = _split3(cs)
    aq = ones_ref[0:1, :]
    ak = ones_ref[1:2, :]
    for j in range(3):
        aq = aq + jnp.dot(parts[j], place_ref[j], preferred_element_type=jnp.float32)
        ak = ak + jnp.dot(parts[j], place_ref[3 + j], preferred_element_type=jnp.float32)
    aq_ref[...] = aq.astype(aq_ref.dtype)
    ak_ref[...] = ak.astype(ak_ref.dtype)


def _proj_small(x2d, w, bf_row, seq, tm=512):
    m = x2d.shape[0]
    tri = jnp.asarray(np.tril(np.ones((tm, tm), np.float32)), jnp.bfloat16)
    place, ones = _gate_aug_constants()
    kern = functools.partial(_proj_small_kernel, blocks_per_seq=seq // tm)
    row = lambda i: (i, 0)
    return pl.pallas_call(
        kern,
        grid=(m // tm,),
        in_specs=[
            pl.BlockSpec((tm, D_MODEL), row),
            pl.BlockSpec((D_MODEL, LANES), lambda i: (0, 0)),
            pl.BlockSpec((1, LANES), lambda i: (0, 0)),
            pl.BlockSpec((tm, tm), lambda i: (0, 0)),
            pl.BlockSpec((6, LANES, LANES), lambda i: (0, 0, 0)),
            pl.BlockSpec((8, LANES), lambda i: (0, 0)),
        ],
        out_specs=[pl.BlockSpec((tm, LANES), row), pl.BlockSpec((tm, LANES), row), pl.BlockSpec((tm, LANES), row)],
        out_shape=[jax.ShapeDtypeStruct((m, LANES), jnp.float32),
                   jax.ShapeDtypeStruct((m, LANES), jnp.bfloat16),
                   jax.ShapeDtypeStruct((m, LANES), jnp.bfloat16)],
        scratch_shapes=[pltpu.VMEM((8, LANES), jnp.float32)],
        compiler_params=_cparams(("arbitrary",)),
        name="proj_small",
    )(x2d, w, bf_row, tri, place, ones)


def _flash_init(m_ref, acc_ref):
    m_ref[...] = jnp.full(m_ref.shape, -1e30, jnp.float32)
    acc_ref[...] = jnp.zeros(acc_ref.shape, jnp.float32)


def _flash_rows(s, v1, m_ref, acc_ref, rows):
    tk = s.shape[1]
    m_prev = m_ref[rows, :]
    m_next = jnp.maximum(m_prev, jnp.max(s, axis=1, keepdims=True))
    p = jnp.exp2((s - pltpu.repeat(m_next, tk // LANES, axis=1)).astype(jnp.bfloat16))
    alpha = jnp.exp2(m_prev - m_next)
    m_ref[rows, :] = m_next
    pv = jnp.dot(p, v1, preferred_element_type=jnp.float32)
    acc_ref[rows, :] = acc_ref[rows, :] * pltpu.repeat(alpha, v1.shape[1] // LANES, axis=1) + pv


def _pipelined_chunks(n_full, produce, consume, buf0, buf1, per_trip=4):
    bufs = (buf0, buf1)
    produce(0, buf0)

    def body(p, carry):
        for t in range(per_trip):
            produce(per_trip * p + t + 1, bufs[(t + 1) % 2])
            consume(per_trip * p + t, bufs[t % 2], False)
        return carry

    lax.fori_loop(0, n_full // per_trip, body, 0)
    base = (n_full // per_trip) * per_trip

    for rem in range(per_trip):
        @pl.when(n_full - base == rem)
        def _(rem=rem):
            for t in range(rem):
                produce(base + t + 1, bufs[(t + 1) % 2])
                consume(base + t, bufs[t % 2], False)
            consume(base + rem, bufs[rem % 2], True)


def _stack_pair(q, qs_ref, t):
    lane = lax.broadcasted_iota(jnp.int32, q.shape, 1)
    zero = jnp.zeros_like(q)
    qs_ref[0:t, :] = jnp.where(lane < HEAD_DIM, q, zero)
    qs_ref[t:2 * t, :] = jnp.where(lane >= HEAD_DIM, q, zero)


def _with_ones(v):
    return jnp.concatenate([v, jnp.ones((v.shape[0], LANES), v.dtype)], axis=1)


def _diff_kernel(lam_ref, g_ref, q_ref, k_ref, v_ref, o_ref, qs_ref, m_ref, acc_ref, s0_ref, s1_ref,
                 *, tq, tk):
    i = pl.program_id(2)
    _stack_pair(q_ref[0], qs_ref, tq)
    _flash_init(m_ref, acc_ref)
    groups = (slice(0, tq), slice(tq, 2 * tq))

    def produce(j, s_ref):
        start = pl.multiple_of(j * tk, tk)
        k = k_ref[0, pl.ds(start, tk), :]
        for rows in groups:
            s_ref[rows, :] = lax.dot_general(qs_ref[rows, :], k, _NT, preferred_element_type=jnp.float32)

    def consume(j, s_ref, last):
        start = pl.multiple_of(j * tk, tk)
        v1 = _with_ones(v_ref[0, pl.ds(start, tk), :])
        for rows in groups:
            s = s_ref[rows, :]
            if last:
                r = lax.broadcasted_iota(jnp.int32, (tq, tk), 0)
                c = lax.broadcasted_iota(jnp.int32, (tq, tk), 1)
                s = jnp.where(start + c <= i * tq + r, s, -jnp.inf)
            _flash_rows(s, v1, m_ref, acc_ref, rows)

    _pipelined_chunks((i * tq) // tk, produce, consume, s0_ref, s1_ref)

    lam_rows = lam_ref[...]
    s1 = jnp.sum(lam_rows[0:1, :] * lam_rows[1:2, :], axis=1, keepdims=True)
    s2 = jnp.sum(lam_rows[2:3, :] * lam_rows[3:4, :], axis=1, keepdims=True)
    lam_init = lam_rows[4:5, 0:1]
    lam = jnp.exp(s1) - jnp.exp(s2) + lam_init
    o0 = acc_ref[0:tq, 0:LANES] / acc_ref[0:tq, LANES:2 * LANES]
    o1 = acc_ref[tq:2 * tq, 0:LANES] / acc_ref[tq:2 * tq, LANES:2 * LANES]
    a = o0 - lam * o1
    ms = jnp.mean(a * a, axis=1, keepdims=True)
    a = a * lax.rsqrt(ms + SUBLN_EPS) * g_ref[...] * (1.0 - lam_init)
    o_ref[0] = a.astype(o_ref.dtype)


def _diff_attention(a3, b3, lam_rows, g_row, tq=1024, tk=1024):
    bsz, seq, _ = a3.shape
    tk = min(tk, seq)
    tq = min(tq, seq)
    kern = functools.partial(_diff_kernel, tq=tq, tk=tk)
    return pl.pallas_call(
        kern,
        grid=(bsz, DIFF_HEADS, seq // tq),
        in_specs=[
            pl.BlockSpec((8, LANES), lambda b, h, i: (0, 0)),
            pl.BlockSpec((1, LANES), lambda b, h, i: (0, 0)),
            pl.BlockSpec((1, tq, LANES), lambda b, h, i: (b, i, h)),
            pl.BlockSpec((1, seq, LANES), lambda b, h, i: (b, 0, 4 + h)),
            pl.BlockSpec((1, seq, LANES), lambda b, h, i: (b, 0, 8 + h)),
        ],
        out_specs=pl.BlockSpec((1, tq, LANES), lambda b, h, i: (b, i, h)),
        out_shape=jax.ShapeDtypeStruct((bsz, seq, DIFF_HEADS * LANES), jnp.bfloat16),
        scratch_shapes=[pltpu.VMEM((2 * tq, LANES), jnp.bfloat16),
                        pltpu.VMEM((2 * tq, LANES), jnp.float32),
                        pltpu.VMEM((2 * tq, 2 * LANES), jnp.float32),
                        pltpu.VMEM((2 * tq, tk), jnp.float32),
                        pltpu.VMEM((2 * tq, tk), jnp.float32)],
        compiler_params=_cparams(("parallel", "parallel", "arbitrary")),
        name="diff_attention",
    )(lam_rows, g_row, a3, a3, b3)


def _fox_kernel(q_ref, k_ref, v_ref, aq_ref, ak_ref, o_ref, qs_ref, m_ref, acc_ref, s0_ref, s1_ref, *, tq, tk):
    g = pl.program_id(1)
    i = pl.program_id(2)
    q = q_ref[0]
    aq = aq_ref[0]
    lane = lax.broadcasted_iota(jnp.int32, (tq, LANES), 1)
    zero = jnp.zeros_like(q)
    for a in range(2):
        rows = slice(a * tq, (a + 1) * tq)
        head_lanes = (lane < HEAD_DIM) if a == 0 else (lane >= HEAD_DIM)
        first = 8 * (2 * g + a)
        qs_ref[rows, 0:LANES] = jnp.where(head_lanes, q, zero)
        qs_ref[rows, LANES:2 * LANES] = jnp.where((lane >= first) & (lane < first + 6), aq, zero)
    _flash_init(m_ref, acc_ref)
    groups = (slice(0, tq), slice(tq, 2 * tq))

    def produce(j, s_ref):
        start = pl.multiple_of(j * tk, tk)
        k2 = jnp.concatenate([k_ref[0, pl.ds(start, tk), :], ak_ref[0, pl.ds(start, tk), :]], axis=1)
        for rows in groups:
            s_ref[rows, :] = lax.dot_general(qs_ref[rows, :], k2, _NT, preferred_element_type=jnp.float32)

    def consume(j, s_ref, last):
        start = pl.multiple_of(j * tk, tk)
        v1 = _with_ones(v_ref[0, pl.ds(start, tk), :])
        for rows in groups:
            s = s_ref[rows, :]
            if last:
                r = lax.broadcasted_iota(jnp.int32, (tq, tk), 0)
                c = lax.broadcasted_iota(jnp.int32, (tq, tk), 1)
                s = jnp.where(start + c <= i * tq + r, s, -jnp.inf)
            _flash_rows(s, v1, m_ref, acc_ref, rows)

    _pipelined_chunks((i * tq) // tk, produce, consume, s0_ref, s1_ref)

    o0 = acc_ref[0:tq, 0:LANES] / acc_ref[0:tq, LANES:2 * LANES]
    o1 = acc_ref[tq:2 * tq, 0:LANES] / acc_ref[tq:2 * tq, LANES:2 * LANES]
    o_ref[0] = jnp.where(lane < HEAD_DIM, o0, o1).astype(o_ref.dtype)


def _fox_attention(b3, aq3, ak3, tq=512, tk=1024):
    bsz, seq, _ = b3.shape
    tk = min(tk, seq)
    kern = functools.partial(_fox_kernel, tq=tq, tk=tk)
    return pl.pallas_call(
        kern,
        grid=(bsz, FOX_HEADS // 2, seq // tq),
        in_specs=[
            pl.BlockSpec((1, tq, LANES), lambda b, g, i: (b, i, 12 + g)),
            pl.BlockSpec((1, seq, LANES), lambda b, g, i: (b, 0, 14 + g)),
            pl.BlockSpec((1, seq, LANES), lambda b, g, i: (b, 0, 16 + g)),
            pl.BlockSpec((1, tq, LANES), lambda b, g, i: (b, i, 0)),
            pl.BlockSpec((1, seq, LANES), lambda b, g, i: (b, 0, 0)),
        ],
        out_specs=pl.BlockSpec((1, tq, LANES), lambda b, g, i: (b, i, g)),
        out_shape=jax.ShapeDtypeStruct((bsz, seq, FOX_HEADS * HEAD_DIM), jnp.bfloat16),
        scratch_shapes=[pltpu.VMEM((2 * tq, 2 * LANES), jnp.bfloat16),
                        pltpu.VMEM((2 * tq, LANES), jnp.float32),
                        pltpu.VMEM((2 * tq, 2 * LANES), jnp.float32),
                        pltpu.VMEM((2 * tq, tk), jnp.float32),
                        pltpu.VMEM((2 * tq, tk), jnp.float32)],
        compiler_params=_cparams(("parallel", "parallel", "arbitrary")),
        name="fox_attention",
    )(b3, b3, b3, aq3, ak3)


def _score_key(score):
    bits = pltpu.bitcast(score, jnp.int32)
    return bits ^ ((bits >> 31) & 0x7FFFFFFF)


def _dsa_kernel(iq_ref, sq_ref, w_ref, ik_ref, sk_ref, v_ref, o_ref,
                keys_ref, cand_ref, candk_ref, qi_ref, qd_ref, wb_ref, m_ref, acc_ref, s0_ref, s1_ref,
                thr_ref, cnt_ref, flag_ref, *, tq, tc, topk, idx_bits, slots):
    i = pl.program_id(1)
    lane = lax.broadcasted_iota(jnp.int32, (tq, LANES), 1)
    low = lane < HEAD_DIM

    for h in range(IDX_HEADS):
        blk = iq_ref[0, :, (h // 2) * LANES:(h // 2 + 1) * LANES]
        qi_ref[h * tq:(h + 1) * tq, :] = jnp.where(low if h % 2 == 0 else ~low, blk, jnp.zeros_like(blk))
    dsa_order = (0, 2, 1, 3)
    for slab, h in enumerate(dsa_order):
        blk = sq_ref[0, :, (h // 2) * LANES:(h // 2 + 1) * LANES]
        qd_ref[slab * tq:(slab + 1) * tq, :] = jnp.where(low if h % 2 == 0 else ~low, blk, jnp.zeros_like(blk))
    wt = w_ref[0]
    for h in range(IDX_HEADS):
        col = jnp.sum(jnp.where(lane == W_LANE0 + h, wt, 0.0), axis=1, keepdims=True)
        wb_ref[h] = jnp.broadcast_to(col, (tq, LANES))

    nlast = (i * tq) // tc
    nch = nlast + 1
    cand_ref[...] = jnp.full(cand_ref.shape, -jnp.inf, jnp.float32)
    row_pos = i * tq + lax.broadcasted_iota(jnp.int32, (tq, tc), 0)
    col_iota = lax.broadcasted_iota(jnp.int32, (tq, tc), 1)
    idx_groups = (slice(0, 4 * tq), slice(4 * tq, 8 * tq))
    dsa_groups = (slice(0, 2 * tq), slice(2 * tq, 4 * tq))

    def rep(x):
        return pltpu.repeat(x, tc // LANES, axis=1)

    def idx_produce(j, s_ref):
        start = pl.multiple_of(j * tc, tc)
        ik = ik_ref[0, pl.ds(start, tc), :]
        for rows in idx_groups:
            s_ref[rows, :] = lax.dot_general(qi_ref[rows, :], ik, _NT, preferred_element_type=jnp.float32)

    def idx_consume(j, s_ref, last):
        start = pl.multiple_of(j * tc, tc)
        score = jnp.zeros((tq, tc), jnp.float32)
        for h in range(IDX_HEADS):
            score = score + jnp.maximum(s_ref[h * tq:(h + 1) * tq, :], 0.0) * rep(wb_ref[h])
        if last:
            score = jnp.where(start + col_iota <= row_pos, score, -jnp.inf)
        keys_ref[:, pl.ds(start, tc)] = _score_key(score)
        for rg in range(tq // 8):
            rs = slice(rg * 8, (rg + 1) * 8)
            xs = [score[rs, u * LANES:(u + 1) * LANES] for u in range(4)]
            for p, q in _SORT4:
                xs[p], xs[q] = jnp.maximum(xs[p], xs[q]), jnp.minimum(xs[p], xs[q])
            c = [cand_ref[rs, sl * LANES:(sl + 1) * LANES] for sl in range(12)]
            for t in range(4):
                c[8 + t] = jnp.maximum(c[8 + t], xs[3 - t])
            for p, q in _BITONIC12:
                c[p], c[q] = jnp.maximum(c[p], c[q]), jnp.minimum(c[p], c[q])
            for sl in range(12):
                cand_ref[rs, sl * LANES:(sl + 1) * LANES] = c[sl]

    _pipelined_chunks(nlast, idx_produce, idx_consume, s0_ref, s1_ref, per_trip=2)

    def count(pred):
        def body(c, acc):
            start = pl.multiple_of(c * tc, tc)
            hit = pred(keys_ref[:, pl.ds(start, tc)], start).astype(jnp.int32)
            for u in range(tc // LANES):
                acc = acc + hit[:, u * LANES:(u + 1) * LANES]
            return acc
        part = lax.fori_loop(0, nch, body, jnp.zeros((tq, LANES), jnp.int32))
        return jnp.broadcast_to(jnp.sum(part, axis=1, keepdims=True), (tq, LANES))

    n_row = i * tq + lax.broadcasted_iota(jnp.int32, (tq, LANES), 0) + 1

    def search_all_keys():
        def pending(state):
            bi, thr, cnt_thr = state
            return jnp.logical_and(bi < 32, jnp.max(jnp.where(n_row > topk, cnt_thr, topk)) > topk)

        def bit_step(state):
            bi, thr, cnt_thr = state
            cand = thr + lax.shift_left(jnp.int32(1), 31 - bi)
            cnt = count(lambda kc, start: kc >= rep(cand))
            ok = cnt >= topk
            return bi + 1, jnp.where(ok, cand, thr), jnp.where(ok, cnt, cnt_thr)

        total = jnp.zeros((tq, LANES), jnp.int32) + nch * tc
        init = (jnp.int32(0), jnp.full((tq, LANES), INT_MIN, jnp.int32), total)
        _, thr, cnt_thr = lax.while_loop(pending, bit_step, init)
        thr_ref[...] = thr
        cnt_ref[...] = cnt_thr

    def search_candidates():
        def pair_step(p, state):
            thr, cnt_thr = state
            one_lo = lax.shift_left(jnp.int32(1), 30 - 2 * p)
            cands = [thr + one_lo, thr + 2 * one_lo, thr + 3 * one_lo]
            parts = [jnp.zeros((tq, LANES), jnp.float32) for _ in cands]
            for sl in range(slots):
                piece = candk_ref[:, sl * LANES:(sl + 1) * LANES]
                parts = [jnp.where(piece >= c, float(sl + 1), part) for c, part in zip(cands, parts)]
            for c, part in zip(cands, parts):
                cnt = jnp.broadcast_to(jnp.sum(part, axis=1, keepdims=True), (tq, LANES))
                ok = cnt >= float(topk)
                thr = jnp.where(ok, c, thr)
                cnt_thr = jnp.where(ok, cnt, cnt_thr)
            return thr, cnt_thr

        init = (jnp.full((tq, LANES), INT_MIN, jnp.int32), jnp.full((tq, LANES), float(slots * LANES), jnp.float32))
        thr, cnt_thr = lax.fori_loop(0, 16, pair_step, init, unroll=8)
        thr_ref[...] = thr
        cnt_ref[...] = cnt_thr.astype(jnp.int32)

    cand_chunks = slots * LANES // tc
    flag_ref[0] = 1

    @pl.when(nch > cand_chunks)
    def _():
        candk_ref[...] = _score_key(cand_ref[...])
        search_candidates()
        kept_min = jnp.max(candk_ref[:, (slots - 1) * LANES:slots * LANES], axis=1, keepdims=True)
        unsafe = jnp.where(kept_min >= thr_ref[...], 1, 0)
        flag_ref[0] = jnp.max(unsafe)

    @pl.when(flag_ref[0] > 0)
    def _():
        search_all_keys()

    thr = thr_ref[...]
    cnt_thr = cnt_ref[...]

    excess = jnp.where(thr > KEY_NEG_INF, cnt_thr - topk, 0)

    @pl.when(jnp.max(excess) > 0)
    def _():
        c_gt = count(lambda kc, start: kc > rep(thr))
        keep = topk - c_gt

        def idx_step(bi, x):
            cand = x + lax.shift_left(jnp.int32(1), idx_bits - 1 - bi)
            below = count(lambda kc, start: (kc == rep(thr)) & (start + col_iota < rep(cand)))
            return jnp.where(below < keep, cand, x)

        last = lax.fori_loop(0, idx_bits, idx_step, jnp.zeros((tq, LANES), jnp.int32))

        def demote(c, carry):
            start = pl.multiple_of(c * tc, tc)
            kc = keys_ref[:, pl.ds(start, tc)]
            drop = (rep(excess) > 0) & (kc == rep(thr)) & (start + col_iota > rep(last))
            keys_ref[:, pl.ds(start, tc)] = jnp.where(drop, kc - 1, kc)
            return carry

        lax.fori_loop(0, nch, demote, 0)

    sel_thr = jnp.maximum(thr, KEY_NEG_INF + 1)

    _flash_init(m_ref, acc_ref)
    vlow = lax.broadcasted_iota(jnp.int32, (tc, LANES), 1) < HEAD_DIM

    def att_produce(j, s_ref):
        start = pl.multiple_of(j * tc, tc)
        sk = sk_ref[0, pl.ds(start, tc), :]
        for rows in dsa_groups:
            s_ref[rows, :] = lax.dot_general(qd_ref[rows, :], sk, _NT, preferred_element_type=jnp.float32)

    def att_consume(j, s_ref, last):
        start = pl.multiple_of(j * tc, tc)
        v2 = v_ref[0, pl.ds(start, tc), :]
        one = jnp.ones_like(v2)
        v_even = jnp.where(vlow, v2, one)
        v_odd = jnp.where(vlow, one, v2)
        sel = keys_ref[:, pl.ds(start, tc)] >= rep(sel_thr)
        for rows, v1 in zip(dsa_groups, (v_even, v_odd)):
            s = s_ref[rows, :].reshape(2, tq, tc)
            s = jnp.where(sel[None], s, -jnp.inf).reshape(2 * tq, tc)
            _flash_rows(s, v1, m_ref, acc_ref, rows)

    _pipelined_chunks(nlast, att_produce, att_consume, s0_ref, s1_ref)

    outs = {}
    for slab, h in enumerate(dsa_order):
        acc = acc_ref[slab * tq:(slab + 1) * tq, :]
        outs[h] = acc / pltpu.roll(acc, HEAD_DIM, axis=1)
    for p in range(DSA_HEADS // 2):
        o_ref[0, :, p * LANES:(p + 1) * LANES] = jnp.where(low, outs[2 * p], outs[2 * p + 1]).astype(o_ref.dtype)


def _dsa_attention(a3, b3, w3, tq=128):
    bsz, seq, _ = a3.shape
    tc, slots = DSA_SCORE_CHUNK, DSA_SLOTS
    topk = min(DSA_TOPK_MAX, seq // 4)
    idx_bits = int(math.log2(seq))
    assert 2 ** idx_bits == seq and tc % tq == 0 and (slots * LANES) % tc == 0 and seq % tc == 0
    kern = functools.partial(_dsa_kernel, tq=tq, tc=tc, topk=topk, idx_bits=idx_bits, slots=slots)
    return pl.pallas_call(
        kern,
        grid=(bsz, seq // tq),
        in_specs=[
            pl.BlockSpec((1, tq, 4 * LANES), lambda b, i: (b, i, 2)),
            pl.BlockSpec((1, tq, 2 * LANES), lambda b, i: (b, i, 6)),
            pl.BlockSpec((1, tq, LANES), lambda b, i: (b, i, 0)),
            pl.BlockSpec((1, seq, LANES), lambda b, i: (b, 0, 14)),
            pl.BlockSpec((1, seq, LANES), lambda b, i: (b, 0, 15)),
            pl.BlockSpec((1, seq, LANES), lambda b, i: (b, 0, 18)),
        ],
        out_specs=pl.BlockSpec((1, tq, 2 * LANES), lambda b, i: (b, i, 0)),
        out_shape=jax.ShapeDtypeStruct((bsz, seq, DSA_HEADS * HEAD_DIM), jnp.bfloat16),
        scratch_shapes=[pltpu.VMEM((tq, seq), jnp.int32),
                        pltpu.VMEM((tq, slots * LANES), jnp.float32),
                        pltpu.VMEM((tq, slots * LANES), jnp.int32),
                        pltpu.VMEM((IDX_HEADS * tq, LANES), jnp.bfloat16),
                        pltpu.VMEM((DSA_HEADS * tq, LANES), jnp.bfloat16),
                        pltpu.VMEM((IDX_HEADS, tq, LANES), jnp.float32),
                        pltpu.VMEM((DSA_HEADS * tq, LANES), jnp.float32),
                        pltpu.VMEM((DSA_HEADS * tq, LANES), jnp.float32),
                        pltpu.VMEM((IDX_HEADS * tq, tc), jnp.float32),
                        pltpu.VMEM((IDX_HEADS * tq, tc), jnp.float32),
                        pltpu.VMEM((tq, LANES), jnp.int32),
                        pltpu.VMEM((tq, LANES), jnp.int32),
                        pltpu.SMEM((1,), jnp.int32)],
        compiler_params=_cparams(("parallel", "arbitrary")),
        name="dsa_attention",
    )(a3, a3, w3, a3, a3, b3)


def _out_kernel(x_ref, gate_ref, a_ref, bo_ref, co_ref, wa_ref, wb_ref, wc_ref, g_ref, b_ref, o_ref):
    gate = gate_ref[...].astype(jnp.float32)
    silu = gate / (1.0 + jnp.exp(-gate))

    def gated(ref, lo, hi):
        return (ref[...].astype(jnp.float32) * silu[:, lo:hi]).astype(jnp.bfloat16)

    out = jnp.dot(gated(a_ref, 0, 512), wa_ref[...], preferred_element_type=jnp.float32)
    out = out + jnp.dot(gated(bo_ref, 512, 768), wb_ref[...], preferred_element_type=jnp.float32)
    out = out + jnp.dot(gated(co_ref, 768, 1024), wc_ref[...], preferred_element_type=jnp.float32)
    y = DEEPNORM_ALPHA * x_ref[...] + out
    mu = jnp.mean(y, axis=1, keepdims=True)
    d = y - mu
    var = jnp.mean(d * d, axis=1, keepdims=True)
    o_ref[...] = d * lax.rsqrt(var + LN_EPS) * g_ref[...] + b_ref[...]


def _out_proj(x2d, b2d, a2d, bo2d, co2d, w_out, ln_g, ln_b, tm=512):
    m = x2d.shape[0]
    row = lambda i: (i, 0)
    fixed = lambda i: (0, 0)
    return pl.pallas_call(
        _out_kernel,
        grid=(m // tm,),
        in_specs=[
            pl.BlockSpec((tm, D_MODEL), row),
            pl.BlockSpec((tm, D_MODEL), row),
            pl.BlockSpec((tm, 512), row),
            pl.BlockSpec((tm, 256), row),
            pl.BlockSpec((tm, 256), row),
            pl.BlockSpec((512, D_MODEL), fixed),
            pl.BlockSpec((256, D_MODEL), lambda i: (2, 0)),
            pl.BlockSpec((256, D_MODEL), lambda i: (3, 0)),
            pl.BlockSpec((1, D_MODEL), fixed),
            pl.BlockSpec((1, D_MODEL), fixed),
        ],
        out_specs=pl.BlockSpec((tm, D_MODEL), row),
        out_shape=jax.ShapeDtypeStruct((m, D_MODEL), jnp.float32),
        compiler_params=_cparams(("parallel",)),
        name="out_proj",
    )(x2d, b2d, a2d, bo2d, co2d, w_out, w_out, w_out, ln_g, ln_b)


def _rope_tables(seq):
    inv = ROPE_THETA ** (-jnp.arange(0, HEAD_DIM, 2, dtype=jnp.float32) / HEAD_DIM)
    ang = jnp.arange(seq, dtype=jnp.int32).astype(jnp.float32)[:, None] * inv[None, :]
    reps = LANES // (HEAD_DIM // 2)
    return jnp.tile(jnp.cos(ang), (1, reps)), jnp.tile(jnp.sin(ang), (1, reps))


def kernel(x, w_in, b_f, lam_q1, lam_k1, lam_q2, lam_k2, g_subln, w_out, ln_g, ln_b):
    bsz, seq, _ = x.shape
    m = bsz * seq
    cos, sin = _rope_tables(seq)

    w_a = w_in[:, :, A_IDX] * A_SCALE
    w_ar = (w_a[:, :, A_ROT_PARTNER] * A_ROT_SIGN).astype(jnp.bfloat16)
    w_a = w_a.astype(jnp.bfloat16)
    w_b = (w_in[:, :, B_IDX] * B_SCALE).astype(jnp.bfloat16)
    w_c = (w_in[:, :, C_IDX] * C_SCALE).astype(jnp.bfloat16)
    w_o = w_out.astype(jnp.bfloat16)
    bf_rows = jnp.zeros((DEPTH, 1, LANES), jnp.float32).at[:, 0, F_LANE0:F_LANE0 + FOX_HEADS].set(b_f)

    x2d = x.reshape(m, D_MODEL)
    for l in range(DEPTH):
        lam_init = 0.8 - 0.6 * math.exp(-0.3 * l)
        lam_rows = jnp.zeros((8, LANES), jnp.float32)
        lam_rows = lam_rows.at[0, :HEAD_DIM].set(lam_q1[l]).at[1, :HEAD_DIM].set(lam_k1[l])
        lam_rows = lam_rows.at[2, :HEAD_DIM].set(lam_q2[l]).at[3, :HEAD_DIM].set(lam_k2[l])
        lam_rows = lam_rows.at[4, :].set(lam_init)

        a2d = _proj_rope(x2d, w_a[l], w_ar[l], cos, sin, seq)
        b2d = _proj_plain(x2d, w_b[l])
        wf2d, aq2d, ak2d = _proj_small(x2d, w_c[l], bf_rows[l], seq)

        a3 = a2d.reshape(bsz, seq, A_WIDTH)
        b3 = b2d.reshape(bsz, seq, B_WIDTH)
        wf3 = wf2d.reshape(bsz, seq, LANES)

        diff_o = _diff_attention(a3, b3, lam_rows, g_subln[l].reshape(1, LANES))
        fox_o = _fox_attention(b3, aq2d.reshape(bsz, seq, LANES), ak2d.reshape(bsz, seq, LANES))
        dsa_o = _dsa_attention(a3, b3, wf3)

        x2d = _out_proj(x2d, b2d, diff_o.reshape(m, 512), dsa_o.reshape(m, 256), fox_o.reshape(m, 256),
                        w_o[l], ln_g[l].reshape(1, D_MODEL), ln_b[l].reshape(1, D_MODEL))
    return x2d.reshape(bsz, seq, D_MODEL)
```

```python
import functools
import math

import numpy as np
import jax
import jax.numpy as jnp
from jax import lax
from jax.experimental import pallas as pl
from jax.experimental.pallas import tpu as pltpu

D_MODEL = 1024
DEPTH = 4
HEAD_DIM = 64
DIFF_HEADS = 4
DSA_HEADS = 4
IDX_HEADS = 8
FOX_HEADS = 4
DSA_TOPK_MAX = 256
ROPE_THETA = 10000.0
LN_EPS = 1e-5
SUBLN_EPS = 1e-5
DEEPNORM_ALPHA = (2 * DEPTH) ** 0.25
IDX_WEIGHT_SCALE = (IDX_HEADS * HEAD_DIM) ** -0.5
LOG2E = math.log2(math.e)
Q_SCALE = HEAD_DIM ** -0.5 * LOG2E

LANES = 128
VMEM_LIMIT_BYTES = 56 * 1024 * 1024

_OFF = {}
_o = 0
for _name, _n in (('diff_q', 512), ('diff_k', 512), ('diff_v', 512), ('dsa_q', 256), ('dsa_k', 64),
                  ('dsa_v', 64), ('idx_q', 512), ('idx_k', 64), ('idx_w', 8), ('fox_q', 256),
                  ('fox_k', 256), ('fox_v', 256), ('fox_f', 4), ('gate', 1024)):
    _OFF[_name] = (_o, _n)
    _o += _n
IN_WIDTH = _o


def _cols(name, scale=1.0, repeat=1):
    o, n = _OFF[name]
    idx = np.tile(np.arange(o, o + n), repeat)
    return idx, np.full(idx.shape, scale, np.float32)


def _pad(n):
    return np.zeros((n,), np.int64), np.zeros((n,), np.float32)


def _layout(parts):
    idx = np.concatenate([p[0] for p in parts])
    scale = np.concatenate([p[1] for p in parts])
    return idx, scale


A_IDX, A_SCALE = _layout([_cols('diff_q', Q_SCALE), _cols('diff_k'), _cols('idx_q'),
                          _cols('dsa_q', Q_SCALE), _cols('idx_k', repeat=2), _cols('dsa_k', repeat=2)])
A_WIDTH = A_IDX.shape[0]
B_IDX, B_SCALE = _layout([_cols('gate'), _cols('diff_v'), _cols('fox_q', Q_SCALE), _cols('fox_k'),
                          _cols('fox_v'), _cols('dsa_v', repeat=2), _pad(128)])
B_WIDTH = B_IDX.shape[0]
C_IDX, C_SCALE = _layout([_cols('idx_w'), _cols('fox_f'), _pad(LANES - 12)])
W_LANE0 = 0
F_LANE0 = 8

_c = np.arange(A_WIDTH)
_first_half = (_c % HEAD_DIM) < HEAD_DIM // 2
A_ROT_PARTNER = np.where(_first_half, _c + HEAD_DIM // 2, _c - HEAD_DIM // 2)
A_ROT_SIGN = np.where(_first_half, -1.0, 1.0).astype(np.float32)

_NT = (((1,), (1,)), ((), ()))
INT_MIN = -2 ** 31
KEY_NEG_INF = -2139095041

_SORT4 = ((0, 1), (2, 3), (0, 2), (1, 3), (1, 2))
_BITONIC12 = ((0, 8), (1, 9), (2, 10), (3, 11), (4, 8), (5, 9), (6, 10), (7, 11),
              (0, 2), (1, 3), (4, 6), (5, 7), (8, 10), (9, 11),
              (0, 1), (2, 3), (4, 5), (6, 7), (8, 9), (10, 11))
DSA_SLOTS = 12
DSA_SCORE_CHUNK = 512


def _cparams(sem):
    return pltpu.CompilerParams(dimension_semantics=sem, vmem_limit_bytes=VMEM_LIMIT_BYTES)


PROJ_COLS = 512


def _proj_rope_kernel(x_ref, w_ref, wr_ref, cos_ref, sin_ref, o_ref):
    xb = x_ref[...].astype(jnp.bfloat16)
    cos = cos_ref[...]
    sin = sin_ref[...]
    for n in range(o_ref.shape[1] // PROJ_COLS):
        cols = slice(n * PROJ_COLS, (n + 1) * PROJ_COLS)
        h = jnp.dot(xb, w_ref[:, cols], preferred_element_type=jnp.float32)
        hr = jnp.dot(xb, wr_ref[:, cols], preferred_element_type=jnp.float32)
        for c in range(PROJ_COLS // LANES):
            sl = slice(c * LANES, (c + 1) * LANES)
            o_ref[:, n * PROJ_COLS + c * LANES:n * PROJ_COLS + (c + 1) * LANES] = (
                h[:, sl] * cos + hr[:, sl] * sin).astype(o_ref.dtype)


def _proj_rope(x2d, w, wr, cos, sin, seq, tm=512):
    m = x2d.shape[0]
    nseq = seq // tm
    return pl.pallas_call(
        _proj_rope_kernel,
        grid=(m // tm,),
        in_specs=[
            pl.BlockSpec((tm, D_MODEL), lambda i: (i, 0)),
            pl.BlockSpec((D_MODEL, A_WIDTH), lambda i: (0, 0)),
            pl.BlockSpec((D_MODEL, A_WIDTH), lambda i: (0, 0)),
            pl.BlockSpec((tm, LANES), lambda i: (i % nseq, 0)),
            pl.BlockSpec((tm, LANES), lambda i: (i % nseq, 0)),
        ],
        out_specs=pl.BlockSpec((tm, A_WIDTH), lambda i: (i, 0)),
        out_shape=jax.ShapeDtypeStruct((m, A_WIDTH), jnp.bfloat16),
        compiler_params=_cparams(("parallel",)),
        name="proj_rope",
    )(x2d, w, wr, cos, sin)


def _proj_both_kernel(x_ref, w_ref, wr_ref, wb_ref, cos_ref, sin_ref, oa_ref, ob_ref):
    xb = x_ref[...].astype(jnp.bfloat16)
    cos = cos_ref[...]
    sin = sin_ref[...]
    for n in range(oa_ref.shape[1] // PROJ_COLS):
        cols = slice(n * PROJ_COLS, (n + 1) * PROJ_COLS)
        h = jnp.dot(xb, w_ref[:, cols], preferred_element_type=jnp.float32)
        hr = jnp.dot(xb, wr_ref[:, cols], preferred_element_type=jnp.float32)
        for c in range(PROJ_COLS // LANES):
            sl = slice(c * LANES, (c + 1) * LANES)
            oa_ref[:, n * PROJ_COLS + c * LANES:n * PROJ_COLS + (c + 1) * LANES] = (
                h[:, sl] * cos + hr[:, sl] * sin).astype(oa_ref.dtype)
    for n in range(ob_ref.shape[1] // PROJ_COLS):
        cols = slice(n * PROJ_COLS, (n + 1) * PROJ_COLS)
        ob_ref[:, cols] = jnp.dot(xb, wb_ref[:, cols], preferred_element_type=jnp.float32).astype(ob_ref.dtype)


def _proj_both(x2d, w, wr, wb, cos, sin, seq, tm=512):
    m = x2d.shape[0]
    nseq = seq // tm
    fixed = lambda i: (0, 0)
    return pl.pallas_call(
        _proj_both_kernel,
        grid=(m // tm,),
        in_specs=[
            pl.BlockSpec((tm, D_MODEL), lambda i: (i, 0)),
            pl.BlockSpec((D_MODEL, A_WIDTH), fixed),
            pl.BlockSpec((D_MODEL, A_WIDTH), fixed),
            pl.BlockSpec((D_MODEL, B_WIDTH), fixed),
            pl.BlockSpec((tm, LANES), lambda i: (i % nseq, 0)),
            pl.BlockSpec((tm, LANES), lambda i: (i % nseq, 0)),
        ],
        out_specs=[pl.BlockSpec((tm, A_WIDTH), lambda i: (i, 0)), pl.BlockSpec((tm, B_WIDTH), lambda i: (i, 0))],
        out_shape=[jax.ShapeDtypeStruct((m, A_WIDTH), jnp.bfloat16),
                   jax.ShapeDtypeStruct((m, B_WIDTH), jnp.bfloat16)],
        compiler_params=_cparams(("parallel",)),
        name="proj_both",
    )(x2d, w, wr, wb, cos, sin)


def _proj_plain_kernel(x_ref, w_ref, o_ref):
    xb = x_ref[...].astype(jnp.bfloat16)
    for n in range(o_ref.shape[1] // PROJ_COLS):
        cols = slice(n * PROJ_COLS, (n + 1) * PROJ_COLS)
        o_ref[:, cols] = jnp.dot(xb, w_ref[:, cols], preferred_element_type=jnp.float32).astype(o_ref.dtype)


def _proj_plain(x2d, w, tm=512):
    m = x2d.shape[0]
    return pl.pallas_call(
        _proj_plain_kernel,
        grid=(m // tm,),
        in_specs=[
            pl.BlockSpec((tm, D_MODEL), lambda i: (i, 0)),
            pl.BlockSpec((D_MODEL, B_WIDTH), lambda i: (0, 0)),
        ],
        out_specs=pl.BlockSpec((tm, B_WIDTH), lambda i: (i, 0)),
        out_shape=jax.ShapeDtypeStruct((m, B_WIDTH), jnp.bfloat16),
        compiler_params=_cparams(("parallel",)),
        name="proj_plain",
    )(x2d, w)


def _split3(v):
    hi = v.astype(jnp.bfloat16)
    r = v - hi.astype(jnp.float32)
    mid = r.astype(jnp.bfloat16)
    lo = (r - mid.astype(jnp.float32)).astype(jnp.bfloat16)
    return hi, mid, lo


def _gate_aug_constants():
    place = np.zeros((6, LANES, LANES), np.float32)
    ones = np.zeros((8, LANES), np.float32)
    for h in range(FOX_HEADS):
        for j in range(3):
            place[j, F_LANE0 + h, 8 * h + j] = 1.0
            place[3 + j, F_LANE0 + h, 8 * h + 3 + j] = -1.0
            ones[0, 8 * h + 3 + j] = 1.0
            ones[1, 8 * h + j] = 1.0
    return jnp.asarray(place, jnp.bfloat16), jnp.asarray(ones)


def _proj_small_kernel(x_ref, w_ref, bf_ref, tri_ref, place_ref, ones_ref, o_ref, aq_ref, ak_ref, carry_ref,
                       *, blocks_per_seq):
    i = pl.program_id(0)
    xb = x_ref[...].astype(jnp.bfloat16)
    h = jnp.dot(xb, w_ref[...], preferred_element_type=jnp.float32)
    lane = lax.broadcasted_iota(jnp.int32, h.shape, 1)
    ff = h + bf_ref[...]
    logf = jnp.minimum(ff, 0.0) - jnp.log(1.0 + jnp.exp(-jnp.abs(ff)))
    o = jnp.where(lane < F_LANE0, h * IDX_WEIGHT_SCALE, logf * LOG2E)
    o_ref[...] = o

    @pl.when(i % blocks_per_seq == 0)
    def _():
        carry_ref[...] = jnp.zeros_like(carry_ref)

    tri = tri_ref[...]
    hi, mid, lo = _split3(o)
    cs = (jnp.dot(tri, hi, preferred_element_type=jnp.float32)
          + jnp.dot(tri, mid, preferred_element_type=jnp.float32)
          + jnp.dot(tri, lo, preferred_element_type=jnp.float32))
    cs = cs + carry_ref[0:1, :]
    tm = cs.shape[0]
    carry_ref[...] = jnp.broadcast_to(cs[tm - 1:tm, :], carry_ref.shape)
    parts = _split3(cs)
    aq = ones_ref[0:1, :]
    ak = ones_ref[1:2, :]
    for j in range(3):
        aq = aq + jnp.dot(parts[j], place_ref[j], preferred_element_type=jnp.float32)
        ak = ak + jnp.dot(parts[j], place_ref[3 + j], preferred_element_type=jnp.float32)
    aq_ref[...] = aq.astype(aq_ref.dtype)
    ak_ref[...] = ak.astype(ak_ref.dtype)


def _proj_small(x2d, w, bf_row, seq, tm=512):
    m = x2d.shape[0]
    tri = jnp.asarray(np.tril(np.ones((tm, tm), np.float32)), jnp.bfloat16)
    place, ones = _gate_aug_constants()
    kern = functools.partial(_proj_small_kernel, blocks_per_seq=seq // tm)
    row = lambda i: (i, 0)
    return pl.pallas_call(
        kern,
        grid=(m // tm,),
        in_specs=[
            pl.BlockSpec((tm, D_MODEL), row),
            pl.BlockSpec((D_MODEL, LANES), lambda i: (0, 0)),
            pl.BlockSpec((1, LANES), lambda i: (0, 0)),
            pl.BlockSpec((tm, tm), lambda i: (0, 0)),
            pl.BlockSpec((6, LANES, LANES), lambda i: (0, 0, 0)),
            pl.BlockSpec((8, LANES), lambda i: (0, 0)),
        ],
        out_specs=[pl.BlockSpec((tm, LANES), row), pl.BlockSpec((tm, LANES), row), pl.BlockSpec((tm, LANES), row)],
        out_shape=[jax.ShapeDtypeStruct((m, LANES), jnp.float32),
                   jax.ShapeDtypeStruct((m, LANES), jnp.bfloat16),
                   jax.ShapeDtypeStruct((m, LANES), jnp.bfloat16)],
        scratch_shapes=[pltpu.VMEM((8, LANES), jnp.float32)],
        compiler_params=_cparams(("arbitrary",)),
        name="proj_small",
    )(x2d, w, bf_row, tri, place, ones)


def _flash_init(m_ref, acc_ref):
    m_ref[...] = jnp.full(m_ref.shape, -1e30, jnp.float32)
    acc_ref[...] = jnp.zeros(acc_ref.shape, jnp.float32)


def _flash_rows(s, v1, m_ref, acc_ref, rows):
    tk = s.shape[1]
    m_prev = m_ref[rows, :]
    m_next = jnp.maximum(m_prev, jnp.max(s, axis=1, keepdims=True))
    p = jnp.exp2((s - pltpu.repeat(m_next, tk // LANES, axis=1)).astype(jnp.bfloat16))
    alpha = jnp.exp2(m_prev - m_next)
    m_ref[rows, :] = m_next
    pv = jnp.dot(p, v1, preferred_element_type=jnp.float32)
    acc_ref[rows, :] = acc_ref[rows, :] * pltpu.repeat(alpha, v1.shape[1] // LANES, axis=1) + pv


def _pipelined_chunks(n_full, produce, consume, buf0, buf1, per_trip=4):
    bufs = (buf0, buf1)
    produce(0, buf0)

    def body(p, carry):
        for t in range(per_trip):
            produce(per_trip * p + t + 1, bufs[(t + 1) % 2])
            consume(per_trip * p + t, bufs[t % 2], False)
        return carry

    lax.fori_loop(0, n_full // per_trip, body, 0)
    base = (n_full // per_trip) * per_trip

    for rem in range(per_trip):
        @pl.when(n_full - base == rem)
        def _(rem=rem):
            for t in range(rem):
                produce(base + t + 1, bufs[(t + 1) % 2])
                consume(base + t, bufs[t % 2], False)
            consume(base + rem, bufs[rem % 2], True)


def _stack_pair(q, qs_ref, t):
    lane = lax.broadcasted_iota(jnp.int32, q.shape, 1)
    zero = jnp.zeros_like(q)
    qs_ref[0:t, :] = jnp.where(lane < HEAD_DIM, q, zero)
    qs_ref[t:2 * t, :] = jnp.where(lane >= HEAD_DIM, q, zero)


def _with_ones(v):
    return jnp.concatenate([v, jnp.ones((v.shape[0], LANES), v.dtype)], axis=1)


def _diff_kernel(lam_ref, g_ref, q_ref, k_ref, v_ref, o_ref, qs_ref, m_ref, acc_ref, s0_ref, s1_ref,
                 *, tq, tk):
    i = pl.program_id(2)
    _stack_pair(q_ref[0], qs_ref, tq)
    _flash_init(m_ref, acc_ref)
    groups = (slice(0, tq), slice(tq, 2 * tq))

    def produce(j, s_ref):
        start = pl.multiple_of(j * tk, tk)
        k = k_ref[0, pl.ds(start, tk), :]
        for rows in groups:
            s_ref[rows, :] = lax.dot_general(qs_ref[rows, :], k, _NT, preferred_element_type=jnp.float32)

    def consume(j, s_ref, last):
        start = pl.multiple_of(j * tk, tk)
        v1 = _with_ones(v_ref[0, pl.ds(start, tk), :])
        for rows in groups:
            s = s_ref[rows, :]
            if last:
                r = lax.broadcasted_iota(jnp.int32, (tq, tk), 0)
                c = lax.broadcasted_iota(jnp.int32, (tq, tk), 1)
                s = jnp.where(start + c <= i * tq + r, s, -jnp.inf)
            _flash_rows(s, v1, m_ref, acc_ref, rows)

    _pipelined_chunks((i * tq) // tk, produce, consume, s0_ref, s1_ref)

    lam_rows = lam_ref[...]
    s1 = jnp.sum(lam_rows[0:1, :] * lam_rows[1:2, :], axis=1, keepdims=True)
    s2 = jnp.sum(lam_rows[2:3, :] * lam_rows[3:4, :], axis=1, keepdims=True)
    lam_init = lam_rows[4:5, 0:1]
    lam = jnp.exp(s1) - jnp.exp(s2) + lam_init
    o0 = acc_ref[0:tq, 0:LANES] / acc_ref[0:tq, LANES:2 * LANES]
    o1 = acc_ref[tq:2 * tq, 0:LANES] / acc_ref[tq:2 * tq, LANES:2 * LANES]
    a = o0 - lam * o1
    ms = jnp.mean(a * a, axis=1, keepdims=True)
    a = a * lax.rsqrt(ms + SUBLN_EPS) * g_ref[...] * (1.0 - lam_init)
    o_ref[0] = a.astype(o_ref.dtype)


def _diff_attention(a3, b3, lam_rows, g_row, tq=1024, tk=1024):
    bsz, seq, _ = a3.shape
    tk = min(tk, seq)
    tq = min(tq, seq)
    kern = functools.partial(_diff_kernel, tq=tq, tk=tk)
    return pl.pallas_call(
        kern,
        grid=(bsz, DIFF_HEADS, seq // tq),
        in_specs=[
            pl.BlockSpec((8, LANES), lambda b, h, i: (0, 0)),
            pl.BlockSpec((1, LANES), lambda b, h, i: (0, 0)),
            pl.BlockSpec((1, tq, LANES), lambda b, h, i: (b, i, h)),
            pl.BlockSpec((1, seq, LANES), lambda b, h, i: (b, 0, 4 + h)),
            pl.BlockSpec((1, seq, LANES), lambda b, h, i: (b, 0, 8 + h)),
        ],
        out_specs=pl.BlockSpec((1, tq, LANES), lambda b, h, i: (b, i, h)),
        out_shape=jax.ShapeDtypeStruct((bsz, seq, DIFF_HEADS * LANES), jnp.bfloat16),
        scratch_shapes=[pltpu.VMEM((2 * tq, LANES), jnp.bfloat16),
                        pltpu.VMEM((2 * tq, LANES), jnp.float32),
                        pltpu.VMEM((2 * tq, 2 * LANES), jnp.float32),
                        pltpu.VMEM((2 * tq, tk), jnp.float32),
                        pltpu.VMEM((2 * tq, tk), jnp.float32)],
        compiler_params=_cparams(("parallel", "parallel", "arbitrary")),
        name="diff_attention",
    )(lam_rows, g_row, a3, a3, b3)


def _fox_kernel(q_ref, k_ref, v_ref, aq_ref, ak_ref, o_ref, qs_ref, m_ref, acc_ref, s0_ref, s1_ref, *, tq, tk):
    g = pl.program_id(1)
    i = pl.program_id(2)
    q = q_ref[0]
    aq = aq_ref[0]
    lane = lax.broadcasted_iota(jnp.int32, (tq, LANES), 1)
    zero = jnp.zeros_like(q)
    for a in range(2):
        rows = slice(a * tq, (a + 1) * tq)
        head_lanes = (lane < HEAD_DIM) if a == 0 else (lane >= HEAD_DIM)
        first = 8 * (2 * g + a)
        qs_ref[rows, 0:LANES] = jnp.where(head_lanes, q, zero)
        qs_ref[rows, LANES:2 * LANES] = jnp.where((lane >= first) & (lane < first + 6), aq, zero)
    _flash_init(m_ref, acc_ref)
    groups = (slice(0, tq), slice(tq, 2 * tq))

    def produce(j, s_ref):
        start = pl.multiple_of(j * tk, tk)
        k2 = jnp.concatenate([k_ref[0, pl.ds(start, tk), :], ak_ref[0, pl.ds(start, tk), :]], axis=1)
        for rows in groups:
            s_ref[rows, :] = lax.dot_general(qs_ref[rows, :], k2, _NT, preferred_element_type=jnp.float32)

    def consume(j, s_ref, last):
        start = pl.multiple_of(j * tk, tk)
        v1 = _with_ones(v_ref[0, pl.ds(start, tk), :])
        for rows in groups:
            s = s_ref[rows, :]
            if last:
                r = lax.broadcasted_iota(jnp.int32, (tq, tk), 0)
                c = lax.broadcasted_iota(jnp.int32, (tq, tk), 1)
                s = jnp.where(start + c <= i * tq + r, s, -jnp.inf)
            _flash_rows(s, v1, m_ref, acc_ref, rows)

    _pipelined_chunks((i * tq) // tk, produce, consume, s0_ref, s1_ref)

    o0 = acc_ref[0:tq, 0:LANES] / acc_ref[0:tq, LANES:2 * LANES]
    o1 = acc_ref[tq:2 * tq, 0:LANES] / acc_ref[tq:2 * tq, LANES:2 * LANES]
    o_ref[0] = jnp.where(lane < HEAD_DIM, o0, o1).astype(o_ref.dtype)


def _fox_attention(b3, aq3, ak3, tq=512, tk=1024):
    bsz, seq, _ = b3.shape
    tk = min(tk, seq)
    kern = functools.partial(_fox_kernel, tq=tq, tk=tk)
    return pl.pallas_call(
        kern,
        grid=(bsz, FOX_HEADS // 2, seq // tq),
        in_specs=[
            pl.BlockSpec((1, tq, LANES), lambda b, g, i: (b, i, 12 + g)),
            pl.BlockSpec((1, seq, LANES), lambda b, g, i: (b, 0, 14 + g)),
            pl.BlockSpec((1, seq, LANES), lambda b, g, i: (b, 0, 16 + g)),
            pl.BlockSpec((1, tq, LANES), lambda b, g, i: (b, i, 0)),
            pl.BlockSpec((1, seq, LANES), lambda b, g, i: (b, 0, 0)),
        ],
        out_specs=pl.BlockSpec((1, tq, LANES), lambda b, g, i: (b, i, g)),
        out_shape=jax.ShapeDtypeStruct((bsz, seq, FOX_HEADS * HEAD_DIM), jnp.bfloat16),
        scratch_shapes=[pltpu.VMEM((2 * tq, 2 * LANES), jnp.bfloat16),
                        pltpu.VMEM((2 * tq, LANES), jnp.float32),
                        pltpu.VMEM((2 * tq, 2 * LANES), jnp.float32),
                        pltpu.VMEM((2 * tq, tk), jnp.float32),
                        pltpu.VMEM((2 * tq, tk), jnp.float32)],
        compiler_params=_cparams(("parallel", "parallel", "arbitrary")),
        name="fox_attention",
    )(b3, b3, b3, aq3, ak3)


def _score_key(score):
    bits = pltpu.bitcast(score, jnp.int32)
    return bits ^ ((bits >> 31) & 0x7FFFFFFF)


def _dsa_kernel(iq_ref, sq_ref, w_ref, ik_ref, sk_ref, v_ref, o_ref,
                keys_ref, cand_ref, candk_ref, qi_ref, qd_ref, wb_ref, m_ref, acc_ref, s0_ref, s1_ref,
                thr_ref, cnt_ref, flag_ref, *, tq, tc, topk, idx_bits, slots):
    i = pl.program_id(1)
    lane = lax.broadcasted_iota(jnp.int32, (tq, LANES), 1)
    low = lane < HEAD_DIM

    for h in range(IDX_HEADS):
        blk = iq_ref[0, :, (h // 2) * LANES:(h // 2 + 1) * LANES]
        qi_ref[h * tq:(h + 1) * tq, :] = jnp.where(low if h % 2 == 0 else ~low, blk, jnp.zeros_like(blk))
    dsa_order = (0, 2, 1, 3)
    for slab, h in enumerate(dsa_order):
        blk = sq_ref[0, :, (h // 2) * LANES:(h // 2 + 1) * LANES]
        qd_ref[slab * tq:(slab + 1) * tq, :] = jnp.where(low if h % 2 == 0 else ~low, blk, jnp.zeros_like(blk))
    wt = w_ref[0]
    for h in range(IDX_HEADS):
        col = jnp.sum(jnp.where(lane == W_LANE0 + h, wt, 0.0), axis=1, keepdims=True)
        wb_ref[h] = jnp.broadcast_to(col, (tq, LANES))

    nlast = (i * tq) // tc
    nch = nlast + 1
    cand_ref[...] = jnp.full(cand_ref.shape, -jnp.inf, jnp.float32)
    row_pos = i * tq + lax.broadcasted_iota(jnp.int32, (tq, tc), 0)
    col_iota = lax.broadcasted_iota(jnp.int32, (tq, tc), 1)
    idx_groups = (slice(0, 4 * tq), slice(4 * tq, 8 * tq))
    dsa_groups = (slice(0, 2 * tq), slice(2 * tq, 4 * tq))

    def rep(x):
        return pltpu.repeat(x, tc // LANES, axis=1)

    def idx_produce(j, s_ref):
        start = pl.multiple_of(j * tc, tc)
        ik = ik_ref[0, pl.ds(start, tc), :]
        for rows in idx_groups:
            s_ref[rows, :] = lax.dot_general(qi_ref[rows, :], ik, _NT, preferred_element_type=jnp.float32)

    def idx_consume(j, s_ref, last):
        start = pl.multiple_of(j * tc, tc)
        score = jnp.zeros((tq, tc), jnp.float32)
        for h in range(IDX_HEADS):
            score = score + jnp.maximum(s_ref[h * tq:(h + 1) * tq, :], 0.0) * rep(wb_ref[h])
        if last:
            score = jnp.where(start + col_iota <= row_pos, score, -jnp.inf)
        keys_ref[:, pl.ds(start, tc)] = _score_key(score)
        for rg in range(tq // 8):
            rs = slice(rg * 8, (rg + 1) * 8)
            xs = [score[rs, u * LANES:(u + 1) * LANES] for u in range(4)]
            for p, q in _SORT4:
                xs[p], xs[q] = jnp.maximum(xs[p], xs[q]), jnp.minimum(xs[p], xs[q])
            c = [cand_ref[rs, sl * LANES:(sl + 1) * LANES] for sl in range(12)]
            for t in range(4):
                c[8 + t] = jnp.maximum(c[8 + t], xs[3 - t])
            for p, q in _BITONIC12:
                c[p], c[q] = jnp.maximum(c[p], c[q]), jnp.minimum(c[p], c[q])
            for sl in range(12):
                cand_ref[rs, sl * LANES:(sl + 1) * LANES] = c[sl]

    _pipelined_chunks(nlast, idx_produce, idx_consume, s0_ref, s1_ref, per_trip=2)

    def count(pred):
        def body(c, acc):
            start = pl.multiple_of(c * tc, tc)
            hit = pred(keys_ref[:, pl.ds(start, tc)], start).astype(jnp.int32)
            for u in range(tc // LANES):
                acc = acc + hit[:, u * LANES:(u + 1) * LANES]
            return acc
        part = lax.fori_loop(0, nch, body, jnp.zeros((tq, LANES), jnp.int32))
        return jnp.broadcast_to(jnp.sum(part, axis=1, keepdims=True), (tq, LANES))

    n_row = i * tq + lax.broadcasted_iota(jnp.int32, (tq, LANES), 0) + 1

    def search_all_keys():
        def pending(state):
            bi, thr, cnt_thr = state
            return jnp.logical_and(bi < 32, jnp.max(jnp.where(n_row > topk, cnt_thr, topk)) > topk)

        def bit_step(state):
            bi, thr, cnt_thr = state
            cand = thr + lax.shift_left(jnp.int32(1), 31 - bi)
            cnt = count(lambda kc, start: kc >= rep(cand))
            ok = cnt >= topk
            return bi + 1, jnp.where(ok, cand, thr), jnp.where(ok, cnt, cnt_thr)

        total = jnp.zeros((tq, LANES), jnp.int32) + nch * tc
        init = (jnp.int32(0), jnp.full((tq, LANES), INT_MIN, jnp.int32), total)
        _, thr, cnt_thr = lax.while_loop(pending, bit_step, init)
        thr_ref[...] = thr
        cnt_ref[...] = cnt_thr

    def search_candidates():
        def pair_step(p, state):
            thr, cnt_thr = state
            one_lo = lax.shift_left(jnp.int32(1), 30 - 2 * p)
            cands = [thr + one_lo, thr + 2 * one_lo, thr + 3 * one_lo]
            parts = [jnp.zeros((tq, LANES), jnp.float32) for _ in cands]
            for sl in range(slots):
                piece = candk_ref[:, sl * LANES:(sl + 1) * LANES]
                parts = [jnp.where(piece >= c, float(sl + 1), part) for c, part in zip(cands, parts)]
            for c, part in zip(cands, parts):
                cnt = jnp.broadcast_to(jnp.sum(part, axis=1, keepdims=True), (tq, LANES))
                ok = cnt >= float(topk)
                thr = jnp.where(ok, c, thr)
                cnt_thr = jnp.where(ok, cnt, cnt_thr)
            return thr, cnt_thr

        init = (jnp.full((tq, LANES), INT_MIN, jnp.int32), jnp.full((tq, LANES), float(slots * LANES), jnp.float32))
        thr, cnt_thr = lax.fori_loop(0, 16, pair_step, init, unroll=4)
        thr_ref[...] = thr
        cnt_ref[...] = cnt_thr.astype(jnp.int32)

    cand_chunks = slots * LANES // tc
    flag_ref[0] = 1

    @pl.when(nch > cand_chunks)
    def _():
        candk_ref[...] = _score_key(cand_ref[...])
        search_candidates()
        kept_min = jnp.max(candk_ref[:, (slots - 1) * LANES:slots * LANES], axis=1, keepdims=True)
        unsafe = jnp.where(kept_min >= thr_ref[...], 1, 0)
        flag_ref[0] = jnp.max(unsafe)

    @pl.when(flag_ref[0] > 0)
    def _():
        search_all_keys()

    thr = thr_ref[...]
    cnt_thr = cnt_ref[...]

    excess = jnp.where(thr > KEY_NEG_INF, cnt_thr - topk, 0)

    @pl.when(jnp.max(excess) > 0)
    def _():
        c_gt = count(lambda kc, start: kc > rep(thr))
        keep = topk - c_gt

        def idx_step(bi, x):
            cand = x + lax.shift_left(jnp.int32(1), idx_bits - 1 - bi)
            below = count(lambda kc, start: (kc == rep(thr)) & (start + col_iota < rep(cand)))
            return jnp.where(below < keep, cand, x)

        last = lax.fori_loop(0, idx_bits, idx_step, jnp.zeros((tq, LANES), jnp.int32))

        def demote(c, carry):
            start = pl.multiple_of(c * tc, tc)
            kc = keys_ref[:, pl.ds(start, tc)]
            drop = (rep(excess) > 0) & (kc == rep(thr)) & (start + col_iota > rep(last))
            keys_ref[:, pl.ds(start, tc)] = jnp.where(drop, kc - 1, kc)
            return carry

        lax.fori_loop(0, nch, demote, 0)

    sel_thr = jnp.maximum(thr, KEY_NEG_INF + 1)

    _flash_init(m_ref, acc_ref)
    vlow = lax.broadcasted_iota(jnp.int32, (tc, LANES), 1) < HEAD_DIM

    def att_produce(j, s_ref):
        start = pl.multiple_of(j * tc, tc)
        sk = sk_ref[0, pl.ds(start, tc), :]
        for rows in dsa_groups:
            s_ref[rows, :] = lax.dot_general(qd_ref[rows, :], sk, _NT, preferred_element_type=jnp.float32)

    def att_consume(j, s_ref, last):
        start = pl.multiple_of(j * tc, tc)
        v2 = v_ref[0, pl.ds(start, tc), :]
        one = jnp.ones_like(v2)
        v_even = jnp.where(vlow, v2, one)
        v_odd = jnp.where(vlow, one, v2)
        sel = keys_ref[:, pl.ds(start, tc)] >= rep(sel_thr)
        for rows, v1 in zip(dsa_groups, (v_even, v_odd)):
            s = s_ref[rows, :].reshape(2, tq, tc)
            s = jnp.where(sel[None], s, -jnp.inf).reshape(2 * tq, tc)
            _flash_rows(s, v1, m_ref, acc_ref, rows)

    _pipelined_chunks(nlast, att_produce, att_consume, s0_ref, s1_ref)

    outs = {}
    for slab, h in enumerate(dsa_order):
        acc = acc_ref[slab * tq:(slab + 1) * tq, :]
        outs[h] = acc / pltpu.roll(acc, HEAD_DIM, axis=1)
    for p in range(DSA_HEADS // 2):
        o_ref[0, :, p * LANES:(p + 1) * LANES] = jnp.where(low, outs[2 * p], outs[2 * p + 1]).astype(o_ref.dtype)


def _dsa_attention(a3, b3, w3, tq=128):
    bsz, seq, _ = a3.shape
    tc, slots = DSA_SCORE_CHUNK, DSA_SLOTS
    topk = min(DSA_TOPK_MAX, seq // 4)
    idx_bits = int(math.log2(seq))
    assert 2 ** idx_bits == seq and tc % tq == 0 and (slots * LANES) % tc == 0 and seq % tc == 0
    kern = functools.partial(_dsa_kernel, tq=tq, tc=tc, topk=topk, idx_bits=idx_bits, slots=slots)
    return pl.pallas_call(
        kern,
        grid=(bsz, seq // tq),
        in_specs=[
            pl.BlockSpec((1, tq, 4 * LANES), lambda b, i: (b, i, 2)),
            pl.BlockSpec((1, tq, 2 * LANES), lambda b, i: (b, i, 6)),
            pl.BlockSpec((1, tq, LANES), lambda b, i: (b, i, 0)),
            pl.BlockSpec((1, seq, LANES), lambda b, i: (b, 0, 14)),
            pl.BlockSpec((1, seq, LANES), lambda b, i: (b, 0, 15)),
            pl.BlockSpec((1, seq, LANES), lambda b, i: (b, 0, 18)),
        ],
        out_specs=pl.BlockSpec((1, tq, 2 * LANES), lambda b, i: (b, i, 0)),
        out_shape=jax.ShapeDtypeStruct((bsz, seq, DSA_HEADS * HEAD_DIM), jnp.bfloat16),
        scratch_shapes=[pltpu.VMEM((tq, seq), jnp.int32),
                        pltpu.VMEM((tq, slots * LANES), jnp.float32),
                        pltpu.VMEM((tq, slots * LANES), jnp.int32),
                        pltpu.VMEM((IDX_HEADS * tq, LANES), jnp.bfloat16),
                        pltpu.VMEM((DSA_HEADS * tq, LANES), jnp.bfloat16),
                        pltpu.VMEM((IDX_HEADS, tq, LANES), jnp.float32),
                        pltpu.VMEM((DSA_HEADS * tq, LANES), jnp.float32),
                        pltpu.VMEM((DSA_HEADS * tq, LANES), jnp.float32),
                        pltpu.VMEM((IDX_HEADS * tq, tc), jnp.float32),
                        pltpu.VMEM((IDX_HEADS * tq, tc), jnp.float32),
                        pltpu.VMEM((tq, LANES), jnp.int32),
                        pltpu.VMEM((tq, LANES), jnp.int32),
                        pltpu.SMEM((1,), jnp.int32)],
        compiler_params=_cparams(("parallel", "arbitrary")),
        name="dsa_attention",
    )(a3, a3, w3, a3, a3, b3)


def _out_kernel(x_ref, gate_ref, a_ref, bo_ref, co_ref, wa_ref, wb_ref, wc_ref, g_ref, b_ref, o_ref):
    gate = gate_ref[...].astype(jnp.float32)
    silu = gate / (1.0 + jnp.exp(-gate))

    def gated(ref, lo, hi):
        return (ref[...].astype(jnp.float32) * silu[:, lo:hi]).astype(jnp.bfloat16)

    out = jnp.dot(gated(a_ref, 0, 512), wa_ref[...], preferred_element_type=jnp.float32)
    out = out + jnp.dot(gated(bo_ref, 512, 768), wb_ref[...], preferred_element_type=jnp.float32)
    out = out + jnp.dot(gated(co_ref, 768, 1024), wc_ref[...], preferred_element_type=jnp.float32)
    y = DEEPNORM_ALPHA * x_ref[...] + out
    mu = jnp.mean(y, axis=1, keepdims=True)
    d = y - mu
    var = jnp.mean(d * d, axis=1, keepdims=True)
    o_ref[...] = d * lax.rsqrt(var + LN_EPS) * g_ref[...] + b_ref[...]


def _out_proj(x2d, b2d, a2d, bo2d, co2d, w_out, ln_g, ln_b, tm=512):
    m = x2d.shape[0]
    row = lambda i: (i, 0)
    fixed = lambda i: (0, 0)
    return pl.pallas_call(
        _out_kernel,
        grid=(m // tm,),
        in_specs=[
            pl.BlockSpec((tm, D_MODEL), row),
            pl.BlockSpec((tm, D_MODEL), row),
            pl.BlockSpec((tm, 512), row),
            pl.BlockSpec((tm, 256), row),
            pl.BlockSpec((tm, 256), row),
            pl.BlockSpec((512, D_MODEL), fixed),
            pl.BlockSpec((256, D_MODEL), lambda i: (2, 0)),
            pl.BlockSpec((256, D_MODEL), lambda i: (3, 0)),
            pl.BlockSpec((1, D_MODEL), fixed),
            pl.BlockSpec((1, D_MODEL), fixed),
        ],
        out_specs=pl.BlockSpec((tm, D_MODEL), row),
        out_shape=jax.ShapeDtypeStruct((m, D_MODEL), jnp.float32),
        compiler_params=_cparams(("parallel",)),
        name="out_proj",
    )(x2d, b2d, a2d, bo2d, co2d, w_out, w_out, w_out, ln_g, ln_b)


def _rope_tables(seq):
    inv = ROPE_THETA ** (-jnp.arange(0, HEAD_DIM, 2, dtype=jnp.float32) / HEAD_DIM)
    ang = jnp.arange(seq, dtype=jnp.int32).astype(jnp.float32)[:, None] * inv[None, :]
    reps = LANES // (HEAD_DIM // 2)
    return jnp.tile(jnp.cos(ang), (1, reps)), jnp.tile(jnp.sin(ang), (1, reps))


def kernel(x, w_in, b_f, lam_q1, lam_k1, lam_q2, lam_k2, g_subln, w_out, ln_g, ln_b):
    bsz, seq, _ = x.shape
    m = bsz * seq
    cos, sin = _rope_tables(seq)

    w_a = w_in[:, :, A_IDX] * A_SCALE
    w_ar = (w_a[:, :, A_ROT_PARTNER] * A_ROT_SIGN).astype(jnp.bfloat16)
    w_a = w_a.astype(jnp.bfloat16)
    w_b = (w_in[:, :, B_IDX] * B_SCALE).astype(jnp.bfloat16)
    w_c = (w_in[:, :, C_IDX] * C_SCALE).astype(jnp.bfloat16)
    w_o = w_out.astype(jnp.bfloat16)
    bf_rows = jnp.zeros((DEPTH, 1, LANES), jnp.float32).at[:, 0, F_LANE0:F_LANE0 + FOX_HEADS].set(b_f)

    x2d = x.reshape(m, D_MODEL)
    for l in range(DEPTH):
        lam_init = 0.8 - 0.6 * math.exp(-0.3 * l)
        lam_rows = jnp.zeros((8, LANES), jnp.float32)
        lam_rows = lam_rows.at[0, :HEAD_DIM].set(lam_q1[l]).at[1, :HEAD_DIM].set(lam_k1[l])
        lam_rows = lam_rows.at[2, :HEAD_DIM].set(lam_q2[l]).at[3, :HEAD_DIM].set(lam_k2[l])
        lam_rows = lam_rows.at[4, :].set(lam_init)

        a2d, b2d = _proj_both(x2d, w_a[l], w_ar[l], w_b[l], cos, sin, seq)
        wf2d, aq2d, ak2d = _proj_small(x2d, w_c[l], bf_rows[l], seq)

        a3 = a2d.reshape(bsz, seq, A_WIDTH)
        b3 = b2d.reshape(bsz, seq, B_WIDTH)
        wf3 = wf2d.reshape(bsz, seq, LANES)

        diff_o = _diff_attention(a3, b3, lam_rows, g_subln[l].reshape(1, LANES))
        fox_o = _fox_attention(b3, aq2d.reshape(bsz, seq, LANES), ak2d.reshape(bsz, seq, LANES))
        dsa_o = _dsa_attention(a3, b3, wf3)

        x2d = _out_proj(x2d, b2d, diff_o.reshape(m, 512), dsa_o.reshape(m, 256), fox_o.reshape(m, 256),
                        w_o[l], ln_g[l].reshape(1, D_MODEL), ln_b[l].reshape(1, D_MODEL))
    return x2d.reshape(bsz, seq, D_MODEL)
```
